```python
import math
import jax
import jax.numpy as jnp
from jax import lax
import numpy as np

D_MODEL = 2048
BATCH = 32
SEQ = 256
DEPTH = 2
DEC_BATCH = 4
DEC_SEQ = 2048
PAST_LEN = 512

GRID_W = 64
N_AB = (DEPTH + 1) // 2
N_CD = DEPTH // 2
D_MIX = D_MODEL
D_HALF = D_MIX // 2
S5_GROUP = 16
S5_GROUPS = D_HALF // S5_GROUP
S5_STATE = 64
NA_HEADS = 16
NA_HD = D_HALF // NA_HEADS
NA_KH = 8
NA_KW = 16
NA_QB = 16
HY_BANDS = 16
HY_EMB = 1 + 2 * HY_BANDS
HY_FFN = 64
RW_HEADS = 16
RW_HD = D_HALF // RW_HEADS
RW_W_RANK = 64
RW_A_RANK = 64
RW_G_RANK = 128
RW_LORA = RW_W_RANK + RW_A_RANK + RW_G_RANK
N_EXPERTS = 64
TOP_K = 8
N_ROUTE_GROUPS = 8
TOPK_ROUTE_GROUPS = 4
D_EXPERT = 512
ROUTED_SCALE = 2.5
MOE_BLOCK = 128
ATTN_QBLOCK = 128
IN_AB = 4 * D_HALF
IN_CD = 6 * D_HALF + RW_LORA
EPS = 1e-6
GN_EPS = 64e-5
NEG_INF = -1e30

kernel_name = 'hybrid_diffusion_trunk_step'


def rmsnorm(x, g):
    xf = x.astype(jnp.float32)
    y = xf * lax.rsqrt(jnp.mean(xf * xf, axis=-1, keepdims=True) + EPS)
    return (y * g.astype(jnp.float32)).astype(x.dtype)


def ada_mod(cond, w_mod, b_mod):
    m = jax.nn.silu(cond) @ w_mod + b_mod
    return [t[:, None, :] for t in jnp.split(m, 6, axis=-1)]


def modulate(x, g, shift, scale):
    return rmsnorm(x, g) * (1 + scale) + shift


def heads_first(t, n_heads):
    b, l, _ = t.shape
    return t.reshape(b, l, n_heads, -1).transpose(0, 2, 1, 3)


def merge_heads(t):
    b, nh, l, hd = t.shape
    return t.transpose(0, 2, 1, 3).reshape(b, l, nh * hd)


def centred_conv3(x, w, b):
    xp = jnp.pad(x, ((0, 0), (1, 1), (0, 0)))
    return xp[:, :-2] * w[0] + xp[:, 1:-1] * w[1] + xp[:, 2:] * w[2] + b


def centred_shift(x):
    xp = jnp.pad(x, ((0, 0), (1, 1), (0, 0)))
    return 0.5 * (xp[:, :-2] + xp[:, 2:]) - x


def s5_linear_scan(lam_bar, bu, x0, reverse):
    edge = -1 if reverse else 0
    bu = bu.at[:, edge].add(lam_bar * x0)
    a = jnp.broadcast_to(lam_bar, bu.shape)

    def combine(e1, e2):
        a1, b1 = e1
        a2, b2 = e2
        return a1 * a2, a2 * b1 + b2

    _, xs = lax.associative_scan(combine, (a, bu), axis=1, reverse=reverse)
    return xs


def s5_mixer(u, x0, lam_re, lam_im, log_step, b_re, b_im, c_re, c_im, d_skip, w_glu, b_glu):
    f32 = jnp.float32
    bsz, length, _ = u.shape
    uf = u.astype(f32).reshape(bsz, length, S5_GROUPS, S5_GROUP)
    lam = lax.complex(lam_re.astype(f32), lam_im.astype(f32))
    lam_bar = jnp.exp(lam * jnp.exp(log_step.astype(f32))[..., None])
    b_bar = ((lam_bar - 1.0) / lam)[..., None] * lax.complex(b_re.astype(f32), b_im.astype(f32))
    y = uf * d_skip.astype(f32).reshape(S5_GROUPS, S5_GROUP)
    finals = []
    for direction in range(2):
        bb = b_bar[direction]
        bu = lax.complex(jnp.einsum('gpc,blgc->blgp', bb.real, uf), jnp.einsum('gpc,blgc->blgp', bb.imag, uf))
        init = lax.complex(x0[:, direction, 0].astype(f32), x0[:, direction, 1].astype(f32))
        xs = s5_linear_scan(lam_bar[direction], bu, init, reverse=(direction == 1))
        y = y + jnp.einsum('gcp,blgp->blgc', c_re[direction].astype(f32), xs.real) - jnp.einsum('gcp,blgp->blgc', c_im[direction].astype(f32), xs.imag)
        fin = xs[:, 0] if direction == 1 else xs[:, -1]
        finals.append(jnp.stack([fin.real, fin.imag], axis=1))
    y = jax.nn.gelu(y.reshape(bsz, length, D_HALF))
    y = y * jax.nn.sigmoid(y @ w_glu.astype(f32) + b_glu.astype(f32))
    return y.astype(u.dtype), jnp.stack(finals, axis=1)


def context_attention(q, k, v):
    bsz, nh, t, hd = q.shape
    nb = t // ATTN_QBLOCK
    qb = jnp.moveaxis(q.reshape(bsz, nh, nb, ATTN_QBLOCK, hd), 2, 0)

    def block(qi):
        s = jnp.einsum('bhqd,bhkd->bhqk', qi, k).astype(jnp.float32) * hd ** -0.5
        p = jax.nn.softmax(s, axis=-1).astype(v.dtype)
        return jnp.einsum('bhqk,bhkd->bhqd', p, v)

    o = lax.map(block, qb)
    return jnp.moveaxis(o, 0, 2).reshape(bsz, nh, t, hd)


def neighbourhood_attention(q, k, v, ck, cv, rpb):
    bsz, nh, length, hd = q.shape
    rows = length // GRID_W
    kh = min(NA_KH, rows)
    ncb = GRID_W // NA_QB
    bw = NA_QB + NA_KW
    band_start = np.clip(np.arange(ncb) * NA_QB - NA_KW // 2, 0, GRID_W - bw)
    band_cols = band_start[:, None] + np.arange(bw)[None, :]
    q_cols = np.arange(GRID_W).reshape(ncb, NA_QB)
    c0 = np.clip(q_cols - NA_KW // 2, 0, GRID_W - NA_KW)
    rel = band_cols[:, None, :] - q_cols[:, :, None]
    col_ok = (band_cols[:, None, :] >= c0[:, :, None]) & (band_cols[:, None, :] < c0[:, :, None] + NA_KW)
    nk = kh * bw
    mask = np.broadcast_to(col_ok[:, :, None, :], (ncb, NA_QB, kh, bw)).reshape(ncb, NA_QB, nk)
    dc_idx = np.clip(rel + NA_KW - 1, 0, 2 * NA_KW - 2)
    qg = q.reshape(bsz, nh, rows, GRID_W, hd)
    kg = k.reshape(bsz, nh, rows, GRID_W, hd)
    vg = v.reshape(bsz, nh, rows, GRID_W, hd)
    scale = hd ** -0.5

    def row(r):
        rs = jnp.clip(r - kh // 2, 0, rows - kh)
        kr = lax.dynamic_slice_in_dim(kg, rs, kh, axis=2)[:, :, :, band_cols]
        vr = lax.dynamic_slice_in_dim(vg, rs, kh, axis=2)[:, :, :, band_cols]
        kr = kr.transpose(0, 1, 3, 2, 4, 5).reshape(bsz, nh, ncb, nk, hd)
        vr = vr.transpose(0, 1, 3, 2, 4, 5).reshape(bsz, nh, ncb, nk, hd)
        qr = lax.dynamic_index_in_dim(qg, r, axis=2, keepdims=False).reshape(bsz, nh, ncb, NA_QB, hd)
        dr_idx = rs - r + jnp.arange(kh) + NA_KH - 1
        bias = rpb[:, dr_idx][:, :, dc_idx].transpose(0, 2, 3, 1, 4).reshape(nh, ncb, NA_QB, nk)
        s_win = jnp.einsum('bhnqd,bhnkd->bhnqk', qr, kr).astype(jnp.float32) * scale + bias.astype(jnp.float32)
        s_win = jnp.where(mask, s_win, NEG_INF)
        s_ctx = jnp.einsum('bhnqd,bhkd->bhnqk', qr, ck).astype(jnp.float32) * scale
        p = jax.nn.softmax(jnp.concatenate([s_win, s_ctx], axis=-1), axis=-1).astype(v.dtype)
        o = jnp.einsum('bhnqk,bhnkd->bhnqd', p[..., :nk], vr) + jnp.einsum('bhnqk,bhkd->bhnqd', p[..., nk:], cv)
        return o.reshape(bsz, nh, GRID_W, hd)

    o = lax.map(row, jnp.arange(rows))
    return jnp.moveaxis(o, 0, 2).reshape(bsz, nh, length, hd)


def hyena_filter(length, w1, b1, freq, w2, b2, w3, decay):
    f32 = jnp.float32
    t = jnp.arange(length, dtype=f32) / length
    bands = jnp.linspace(1e-4, HY_BANDS - 1, HY_BANDS, dtype=f32)
    ang = 2 * math.pi * t[:, None] * bands[None, :]
    z = jnp.concatenate([t[:, None], jnp.cos(ang), jnp.sin(ang)], axis=-1)
    hdn = jnp.sin(freq[0].astype(f32) * (z @ w1.astype(f32) + b1.astype(f32)))
    hdn = jnp.sin(freq[1].astype(f32) * (hdn @ w2.astype(f32) + b2.astype(f32)))
    filt = hdn @ w3.astype(f32)
    offset = jnp.abs(jnp.arange(length, dtype=f32) - length // 2) / length
    filt = filt * jnp.exp(-offset[:, None] * jnp.abs(decay.astype(f32)))
    return filt / (jnp.sum(jnp.abs(filt), axis=0, keepdims=True) + EPS)


def centred_long_conv(u, filt):
    length = u.shape[1]
    uf = jnp.fft.rfft(u, n=2 * length, axis=1)
    hf = jnp.fft.rfft(filt, n=2 * length, axis=0)
    y = jnp.fft.irfft(uf * hf[None], n=2 * length, axis=1)
    return y[:, length // 2: length // 2 + length]


def hyena_mixer(zh, short_w, short_b, w1, b1, freq, w2, b2, w3, decay, bias):
    zh = centred_conv3(zh, short_w, short_b)
    x0, x1, v = jnp.split(zh, 3, axis=-1)
    filt = hyena_filter(zh.shape[1], w1, b1, freq, w2, b2, w3, decay)
    gated = (x1 * v).astype(jnp.float32)
    y = centred_long_conv(gated, filt) + gated * bias.astype(jnp.float32)
    return (x0.astype(jnp.float32) * y).astype(zh.dtype)


def rwkv_scan(s0, r, w, k, v, a_vec, b_vec, reverse):
    def step(s, inp):
        r_t, w_t, k_t, v_t, a_t, b_t = inp
        sa = jnp.einsum('bhvk,bhk->bhv', s, a_t)
        s = s * w_t[:, :, None, :] + sa[..., None] * b_t[:, :, None, :] + v_t[..., None] * k_t[:, :, None, :]
        return s, jnp.einsum('bhvk,bhk->bhv', s, r_t)

    xs = tuple(jnp.moveaxis(t, 1, 0) for t in (r, w, k, v, a_vec, b_vec))
    s_fin, ys = lax.scan(step, s0, xs, reverse=reverse)
    return s_fin, jnp.moveaxis(ys, 0, 1)


def rwkv_mixer(zr, s0, mu, w0, w2, a0, a2, g2, k_k, k_a, r_k, ln_g, ln_b):
    f32 = jnp.float32
    dtype = zr.dtype
    b, l, _ = zr.shape
    zr = (zr + centred_shift(zr) * mu).astype(f32)
    r, k, v, wl, al, gl = jnp.split(zr, [D_HALF, 2 * D_HALF, 3 * D_HALF, 3 * D_HALF + RW_W_RANK, 3 * D_HALF + RW_W_RANK + RW_A_RANK], axis=-1)

    def hs(t):
        return t.reshape(b, l, RW_HEADS, RW_HD)

    g = jax.nn.sigmoid(gl) @ g2.astype(f32)
    kk = hs(k * k_k.astype(f32))
    kk = kk * lax.rsqrt(jnp.sum(kk * kk, axis=-1, keepdims=True) + EPS)
    y = jnp.zeros((b, l, RW_HEADS, RW_HD), f32)
    finals = []
    for d in range(2):
        logw = -jax.nn.softplus(-(w0[d].astype(f32) + jnp.tanh(wl) @ w2[d].astype(f32))) - 0.5
        decay = jnp.exp(-jnp.exp(logw))
        a = jax.nn.sigmoid(a0[d].astype(f32) + al @ a2[d].astype(f32))
        kd = k * (1 + (a - 1) * k_a.astype(f32))
        s_fin, yd = rwkv_scan(s0[:, d].astype(f32), hs(r), hs(decay), hs(kd), hs(v), -kk, kk * hs(a), reverse=(d == 1))
        y = y + yd
        finals.append(s_fin)
    mean = jnp.mean(y, axis=-1, keepdims=True)
    var = jnp.mean(jnp.square(y - mean), axis=-1, keepdims=True)
    y = ((y - mean) * lax.rsqrt(var + GN_EPS)).reshape(b, l, D_HALF) * ln_g.astype(f32) + ln_b.astype(f32)
    bonus = jnp.sum(hs(r) * hs(k) * r_k.astype(f32), axis=-1, keepdims=True) * hs(v)
    out = (y + bonus.reshape(b, l, D_HALF)) * g
    return out.astype(dtype), jnp.stack(finals, axis=1)


def moe_ffn(h, w_router, b_router, w_gate, w_up, w_down, ws_gate, ws_up, ws_down):
    bsz, t, d = h.shape
    x = h.reshape(-1, d)
    n = x.shape[0]
    scores = jax.nn.sigmoid((x @ w_router).astype(jnp.float32))
    sel = scores + b_router.astype(jnp.float32)
    grp_score = jnp.sum(lax.top_k(sel.reshape(n, N_ROUTE_GROUPS, -1), 2)[0], axis=-1)
    _, top_g = lax.top_k(grp_score, TOPK_ROUTE_GROUPS)
    gmask = jnp.sum(jax.nn.one_hot(top_g, N_ROUTE_GROUPS, dtype=jnp.float32), axis=1) > 0
    emask = jnp.repeat(gmask, N_EXPERTS // N_ROUTE_GROUPS, axis=1)
    _, top_e = lax.top_k(jnp.where(emask, sel, NEG_INF), TOP_K)
    gates = jnp.take_along_axis(scores, top_e, axis=1)
    gates = gates / jnp.sum(gates, axis=-1, keepdims=True) * ROUTED_SCALE
    n_assign = n * TOP_K
    e_flat = top_e.reshape(-1)
    tok_flat = jnp.repeat(jnp.arange(n, dtype=jnp.int32), TOP_K)
    order = jnp.argsort(e_flat)
    e_s, tok_s, g_s = e_flat[order], tok_flat[order], gates.reshape(-1)[order]
    counts = jnp.bincount(e_flat, length=N_EXPERTS)
    starts = jnp.cumsum(counts) - counts
    padded = (counts + MOE_BLOCK - 1) // MOE_BLOCK * MOE_BLOCK
    pad_end = jnp.cumsum(padded)
    pad_start = pad_end - padded
    dest = pad_start[e_s] + (jnp.arange(n_assign, dtype=jnp.int32) - starts[e_s])
    nb = -(-n_assign // MOE_BLOCK) + N_EXPERTS
    slots = nb * MOE_BLOCK
    slot_tok = jnp.full((slots,), n, jnp.int32).at[dest].set(tok_s)
    slot_gate = jnp.zeros((slots,), jnp.float32).at[dest].set(g_s)
    block_exp = jnp.minimum(jnp.searchsorted(pad_end, jnp.arange(nb, dtype=jnp.int32) * MOE_BLOCK, side='right'), N_EXPERTS - 1)
    xpad = jnp.concatenate([x, jnp.zeros((1, d), x.dtype)], axis=0)

    def run_block(args):
        tok, e = args
        xi = xpad[tok]
        return (jax.nn.silu(xi @ w_gate[e]) * (xi @ w_up[e])) @ w_down[e]

    yb = lax.map(run_block, (slot_tok.reshape(nb, MOE_BLOCK), block_exp)).reshape(slots, d)
    routed = jax.ops.segment_sum(yb * slot_gate[:, None].astype(yb.dtype), slot_tok, num_segments=n + 1)[:n]
    shared = (jax.nn.silu(x @ ws_gate) * (x @ ws_up)) @ ws_down
    return (routed + shared).reshape(bsz, t, d)


def ab_project(h, w_in, q_g, k_g):
    u, q, k, v = jnp.split(h @ w_in, 4, axis=-1)
    q = rmsnorm(heads_first(q, NA_HEADS), q_g)
    k = rmsnorm(heads_first(k, NA_HEADS), k_g)
    return u, q, k, heads_first(v, NA_HEADS)


def ab_context(h, w_in, w_out, q_g, k_g, s5p):
    u, q, k, v = ab_project(h, w_in, q_g, k_g)
    zero = jnp.zeros((h.shape[0], 2, 2, S5_GROUPS, S5_STATE), jnp.float32)
    y_a, s5_fin = s5_mixer(u, zero, *s5p)
    y_b = merge_heads(context_attention(q, k, v))
    return jnp.concatenate([y_a, y_b], axis=-1) @ w_out, k, v, s5_fin


def ab_latent(h, w_in, w_out, q_g, k_g, rpb, ck, cv, s5_state, s5p):
    u, q, k, v = ab_project(h, w_in, q_g, k_g)
    y_a, _ = s5_mixer(u, s5_state, *s5p)
    y_b = merge_heads(neighbourhood_attention(q, k, v, ck, cv, rpb))
    return jnp.concatenate([y_a, y_b], axis=-1) @ w_out


def cd_context(h, w_in, w_out, hyp, rwp):
    z = h @ w_in
    zero = jnp.zeros((h.shape[0], 2, RW_HEADS, RW_HD, RW_HD), jnp.float32)
    y_c = hyena_mixer(z[..., :3 * D_HALF], *hyp)
    y_d, rw_fin = rwkv_mixer(z[..., 3 * D_HALF:], zero, *rwp)
    return jnp.concatenate([y_c, y_d], axis=-1) @ w_out, rw_fin


def cd_latent(h, w_in, w_out, rw_state, hyp, rwp):
    z = h @ w_in
    y_c = hyena_mixer(z[..., :3 * D_HALF], *hyp)
    y_d, _ = rwkv_mixer(z[..., 3 * D_HALF:], rw_state, *rwp)
    return jnp.concatenate([y_c, y_d], axis=-1) @ w_out


def setup_inputs(seed: int = 0) -> dict:
    key = jax.random.key(seed)
    keys = iter(jax.random.split(key, 128))
    f32 = jnp.float32

    def nrm(shape, scale):
        return jax.random.normal(next(keys), shape, f32) * scale

    def unif(shape, lo, hi):
        return jax.random.uniform(next(keys), shape, f32, lo, hi)

    def gain(shape):
        return 1.0 + nrm(shape, 0.02)

    d, e, de = D_MODEL, N_EXPERTS, D_EXPERT
    return {
        'x_prompt': nrm((BATCH, SEQ, d), 1.0),
        'x_sample': nrm((DEC_BATCH, DEC_SEQ, d), 1.0),
        'cache_na_k': nrm((DEC_BATCH, N_AB, NA_HEADS, PAST_LEN, NA_HD), 1.0),
        'cache_na_v': nrm((DEC_BATCH, N_AB, NA_HEADS, PAST_LEN, NA_HD), 1.0),
        'state_s5': nrm((DEC_BATCH, N_AB, 2, 2, S5_GROUPS, S5_STATE), 0.05),
        'state_rwkv': nrm((DEC_BATCH, N_CD, 2, RW_HEADS, RW_HD, RW_HD), 0.1),
        'c': nrm((DEC_BATCH, d), 1.0),
        'c_ctx': nrm((d,), 1.0),
        'ada_w': nrm((DEPTH, d, 6 * d), 0.5 * d ** -0.5),
        'ada_b': nrm((DEPTH, 6 * d), 0.02),
        'norm_mix': gain((DEPTH, d)),
        'norm_ffn': gain((DEPTH, d)),
        'ab_w_in': nrm((N_AB, d, IN_AB), d ** -0.5),
        'ab_w_out': nrm((N_AB, D_MIX, d), D_MIX ** -0.5),
        's5_lam_re': -0.5 + nrm((N_AB, 2, S5_GROUPS, S5_STATE), 0.01),
        's5_lam_im': math.pi * jnp.arange(S5_STATE, dtype=f32) + nrm((N_AB, 2, S5_GROUPS, S5_STATE), 0.01),
        's5_log_step': unif((N_AB, 2, S5_GROUPS), math.log(1e-3), math.log(1e-1)),
        's5_b_re': nrm((N_AB, 2, S5_GROUPS, S5_STATE, S5_GROUP), (2 * S5_GROUP) ** -0.5),
        's5_b_im': nrm((N_AB, 2, S5_GROUPS, S5_STATE, S5_GROUP), (2 * S5_GROUP) ** -0.5),
        's5_c_re': nrm((N_AB, 2, S5_GROUPS, S5_GROUP, S5_STATE), (2 * S5_STATE) ** -0.5),
        's5_c_im': nrm((N_AB, 2, S5_GROUPS, S5_GROUP, S5_STATE), (2 * S5_STATE) ** -0.5),
        's5_d': nrm((N_AB, D_HALF), 1.0),
        's5_w_glu': nrm((N_AB, D_HALF, D_HALF), D_HALF ** -0.5),
        's5_b_glu': nrm((N_AB, D_HALF), 0.02),
        'na_q_norm': gain((N_AB, NA_HD)),
        'na_k_norm': gain((N_AB, NA_HD)),
        'na_rpb': nrm((N_AB, NA_HEADS, 2 * NA_KH - 1, 2 * NA_KW - 1), 0.1),
        'cd_w_in': nrm((N_CD, d, IN_CD), d ** -0.5),
        'cd_w_out': nrm((N_CD, D_MIX, d), D_MIX ** -0.5),
        'hy_short_w': nrm((N_CD, 3, 3 * D_HALF), 0.5),
        'hy_short_b': nrm((N_CD, 3 * D_HALF), 0.02),
        'hy_w1': nrm((N_CD, HY_EMB, HY_FFN), HY_EMB ** -0.5),
        'hy_b1': nrm((N_CD, HY_FFN), 0.02),
        'hy_freq': gain((N_CD, 2, HY_FFN)),
        'hy_w2': nrm((N_CD, HY_FFN, HY_FFN), HY_FFN ** -0.5),
        'hy_b2': nrm((N_CD, HY_FFN), 0.02),
        'hy_w3': nrm((N_CD, HY_FFN, D_HALF), HY_FFN ** -0.5),
        'hy_decay': unif((N_CD, D_HALF), 3.0, 15.0),
        'hy_bias': nrm((N_CD, D_HALF), 0.5),
        'rw_mu': unif((N_CD, 3 * D_HALF + RW_LORA), 0.0, 1.0),
        'rw_w0': unif((N_CD, 2, D_HALF), -6.0, -1.0),
        'rw_w2': nrm((N_CD, 2, RW_W_RANK, D_HALF), 0.1 * RW_W_RANK ** -0.5),
        'rw_a0': nrm((N_CD, 2, D_HALF), 0.1),
        'rw_a2': nrm((N_CD, 2, RW_A_RANK, D_HALF), RW_A_RANK ** -0.5),
        'rw_g2': nrm((N_CD, RW_G_RANK, D_HALF), RW_G_RANK ** -0.5),
        'rw_k_k': 0.85 + nrm((N_CD, D_HALF), 0.02),
        'rw_k_a': gain((N_CD, D_HALF)),
        'rw_r_k': nrm((N_CD, RW_HEADS, RW_HD), 0.1),
        'rw_ln_g': gain((N_CD, D_HALF)),
        'rw_ln_b': nrm((N_CD, D_HALF), 0.02),
        'moe_router': nrm((DEPTH, d, e), d ** -0.5),
        'moe_router_bias': nrm((DEPTH, e), 0.01),
        'moe_w_gate': nrm((DEPTH, e, d, de), d ** -0.5),
        'moe_w_up': nrm((DEPTH, e, d, de), d ** -0.5),
        'moe_w_down': nrm((DEPTH, e, de, d), de ** -0.5),
        'moe_ws_gate': nrm((DEPTH, d, de), d ** -0.5),
        'moe_ws_up': nrm((DEPTH, d, de), d ** -0.5),
        'moe_ws_down': nrm((DEPTH, de, d), de ** -0.5),
    }


def reference(x_prompt, x_sample, cache_na_k, cache_na_v, state_s5, state_rwkv, c, c_ctx,
              ada_w, ada_b, norm_mix, norm_ffn,
              ab_w_in, ab_w_out, s5_lam_re, s5_lam_im, s5_log_step, s5_b_re, s5_b_im, s5_c_re, s5_c_im,
              s5_d, s5_w_glu, s5_b_glu, na_q_norm, na_k_norm, na_rpb,
              cd_w_in, cd_w_out, hy_short_w, hy_short_b, hy_w1, hy_b1, hy_freq, hy_w2, hy_b2, hy_w3,
              hy_decay, hy_bias,
              rw_mu, rw_w0, rw_w2, rw_a0, rw_a2, rw_g2, rw_k_k, rw_k_a, rw_r_k, rw_ln_g, rw_ln_b,
              moe_router, moe_router_bias, moe_w_gate, moe_w_up, moe_w_down, moe_ws_gate, moe_ws_up, moe_ws_down):
    def s5p(i):
        return (s5_lam_re[i], s5_lam_im[i], s5_log_step[i], s5_b_re[i], s5_b_im[i], s5_c_re[i], s5_c_im[i],
                s5_d[i], s5_w_glu[i], s5_b_glu[i])

    def hyp(i):
        return (hy_short_w[i], hy_short_b[i], hy_w1[i], hy_b1[i], hy_freq[i], hy_w2[i], hy_b2[i], hy_w3[i],
                hy_decay[i], hy_bias[i])

    def rwp(i):
        return (rw_mu[i], rw_w0[i], rw_w2[i], rw_a0[i], rw_a2[i], rw_g2[i], rw_k_k[i], rw_k_a[i], rw_r_k[i],
                rw_ln_g[i], rw_ln_b[i])

    def moe(l, h):
        return moe_ffn(h, moe_router[l], moe_router_bias[l], moe_w_gate[l], moe_w_up[l], moe_w_down[l],
                       moe_ws_gate[l], moe_ws_up[l], moe_ws_down[l])

    x = x_prompt
    cond_ctx = c_ctx[None, :]
    new_k, new_v, new_s5, new_rw = [], [], [], []
    for l in range(DEPTH):
        sh_m, sc_m, g_m, sh_f, sc_f, g_f = ada_mod(cond_ctx, ada_w[l], ada_b[l])
        h = modulate(x, norm_mix[l], sh_m, sc_m)
        i = l // 2
        if l % 2 == 0:
            o, k_c, v_c, s5_fin = ab_context(h, ab_w_in[i], ab_w_out[i], na_q_norm[i], na_k_norm[i], s5p(i))
            new_k.append(k_c)
            new_v.append(v_c)
            new_s5.append(s5_fin)
        else:
            o, rw_fin = cd_context(h, cd_w_in[i], cd_w_out[i], hyp(i), rwp(i))
            new_rw.append(rw_fin)
        x = x + g_m * o
        h = modulate(x, norm_ffn[l], sh_f, sc_f)
        x = x + g_f * moe(l, h)
    y_prompt = x

    x = x_sample
    for l in range(DEPTH):
        sh_m, sc_m, g_m, sh_f, sc_f, g_f = ada_mod(c, ada_w[l], ada_b[l])
        h = modulate(x, norm_mix[l], sh_m, sc_m)
        i = l // 2
        if l % 2 == 0:
            o = ab_latent(h, ab_w_in[i], ab_w_out[i], na_q_norm[i], na_k_norm[i], na_rpb[i],
                          cache_na_k[:, i], cache_na_v[:, i], state_s5[:, i], s5p(i))
        else:
            o = cd_latent(h, cd_w_in[i], cd_w_out[i], state_rwkv[:, i], hyp(i), rwp(i))
        x = x + g_m * o
        h = modulate(x, norm_ffn[l], sh_f, sc_f)
        x = x + g_f * moe(l, h)
    y_sample = x

    new_na_k = jnp.stack(new_k, axis=1)
    new_na_v = jnp.stack(new_v, axis=1)
    new_s5_state = jnp.stack(new_s5, axis=1)
    new_rwkv_state = jnp.stack(new_rw, axis=1)
    return (y_prompt, y_sample, new_na_k, new_na_v, new_s5_state, new_rwkv_state)
```

```python
import functools
import math

import numpy as np
import jax
import jax.numpy as jnp
from jax import lax
from jax.experimental import pallas as pl
from jax.experimental.pallas import tpu as pltpu

F32 = jnp.float32
BF16 = jnp.bfloat16

D_MODEL = 2048
D_HALF = 1024
SEQ = 256
DEC_SEQ = 2048
GRID_W = 64
S5_GROUP = 16
S5_GROUPS = 64
S5_STATE = 64
S5_LANES = S5_GROUPS * S5_STATE
NA_HEADS = 16
NA_HD = 64
NA_KH = 8
NA_KW = 16
HY_BANDS = 16
RW_HEADS = 16
RW_HD = 64
RW_W_RANK = 64
RW_A_RANK = 64
RW_G_RANK = 128
RW_LORA = 256
RW_IN = 3 * D_HALF + RW_LORA
RW_IN_PAD = 3584
N_EXPERTS = 64
TOP_K = 8
N_ROUTE_GROUPS = 8
TOPK_ROUTE_GROUPS = 4
D_EXPERT = 512
ROUTED_SCALE = 2.5
EPS = 1e-6
GN_EPS = 64e-5
NEG_INF = -1e30

TOK_TILE = 256
MOE_ROWS = 256
VMEM_LIMIT = 56 * 1024 * 1024


def _params(n_axes, vmem=VMEM_LIMIT):
    return pltpu.CompilerParams(dimension_semantics=("arbitrary",) * n_axes, vmem_limit_bytes=vmem)


def _dot(a, b):
    return jnp.dot(a, b, preferred_element_type=F32)


def _dot_nt(a, b):
    return lax.dot_general(a, b, (((1,), (1,)), ((), ())), preferred_element_type=F32)


def _split_bf16(x):
    hi = x.astype(BF16)
    lo = (x - hi.astype(F32)).astype(BF16)
    return hi, lo


def _cond_row(i, n_ctx_tiles, tiles_per_sample):
    return jnp.where(i < n_ctx_tiles, 0, 1 + (i - n_ctx_tiles) // tiles_per_sample)


def _ada_kernel(c_ref, w_ref, b_ref, o_ref):
    c = c_ref[...]
    s = (c * jax.nn.sigmoid(c)).astype(BF16)
    o_ref[0] = _dot(s, w_ref[0].astype(BF16)) + b_ref[0]


def ada_table(cond8, ada_w, ada_b):
    depth, d, n = ada_w.shape
    tn = 1024
    return pl.pallas_call(
        _ada_kernel,
        grid=(depth, n // tn),
        in_specs=[pl.BlockSpec((8, d), lambda l, j: (0, 0)),
                  pl.BlockSpec((1, d, tn), lambda l, j: (l, 0, j)),
                  pl.BlockSpec((1, 1, tn), lambda l, j: (l, 0, j))],
        out_specs=pl.BlockSpec((1, 8, tn), lambda l, j: (l, 0, j)),
        out_shape=jax.ShapeDtypeStruct((depth, 8, n), F32),
        compiler_params=_params(2),
        name="ada_table",
    )(cond8, ada_w, ada_b.reshape(depth, 1, n))


def _modulate_kernel(x_ref, g_ref, mod_ref, h_ref, *, shift_idx, scale_idx):
    x = x_ref[...]
    y = x * lax.rsqrt(jnp.mean(x * x, axis=-1, keepdims=True) + EPS)
    m = mod_ref[0]
    h = y * g_ref[...] * (1.0 + m[scale_idx:scale_idx + 1]) + m[shift_idx:shift_idx + 1]
    h_ref[...] = h.astype(BF16)


def _modulate_router_kernel(x_ref, g_ref, mod_ref, wr_ref, h_ref, lg_ref, *, shift_idx, scale_idx):
    x = x_ref[...]
    y = x * lax.rsqrt(jnp.mean(x * x, axis=-1, keepdims=True) + EPS)
    m = mod_ref[0]
    h = y * g_ref[...] * (1.0 + m[scale_idx:scale_idx + 1]) + m[shift_idx:shift_idx + 1]
    h_hi, h_lo = _split_bf16(h)
    h_ref[...] = h_hi
    w_hi, w_lo = _split_bf16(wr_ref[...])
    lg_ref[...] = _dot_nt(w_hi, h_hi) + _dot_nt(w_hi, h_lo) + _dot_nt(w_lo, h_hi)


def modulate(x, gain, mods, shift_idx, scale_idx, n_ctx, w_router_t=None):
    n, d = x.shape
    tm = TOK_TILE
    row = functools.partial(_cond_row, n_ctx_tiles=n_ctx // tm, tiles_per_sample=DEC_SEQ // tm)
    in_specs = [pl.BlockSpec((tm, d), lambda i: (i, 0)),
                pl.BlockSpec((1, d), lambda i: (0, 0)),
                pl.BlockSpec((1, 6, d), lambda i: (row(i), 0, 0))]
    if w_router_t is None:
        return pl.pallas_call(
            functools.partial(_modulate_kernel, shift_idx=shift_idx, scale_idx=scale_idx),
            grid=(n // tm,),
            in_specs=in_specs,
            out_specs=pl.BlockSpec((tm, d), lambda i: (i, 0)),
            out_shape=jax.ShapeDtypeStruct((n, d), BF16),
            compiler_params=_params(1),
            name="modulate",
        )(x, gain.reshape(1, d), mods)
    e = w_router_t.shape[0]
    return pl.pallas_call(
        functools.partial(_modulate_router_kernel, shift_idx=shift_idx, scale_idx=scale_idx),
        grid=(n // tm,),
        in_specs=in_specs + [pl.BlockSpec((e, d), lambda i: (0, 0))],
        out_specs=[pl.BlockSpec((tm, d), lambda i: (i, 0)),
                   pl.BlockSpec((e, tm), lambda i: (0, i))],
        out_shape=[jax.ShapeDtypeStruct((n, d), BF16), jax.ShapeDtypeStruct((e, n), F32)],
        compiler_params=_params(1),
        name="modulate_router",
    )(x, gain.reshape(1, d), mods, w_router_t)


def _proj_kernel(x_ref, w_ref, o_ref, wbf_ref):
    @pl.when(pl.program_id(1) == 0)
    def _():
        wbf_ref[...] = w_ref[...].astype(BF16)

    o_ref[...] = _dot(x_ref[...], wbf_ref[...]).astype(o_ref.dtype)


def project(x, w, tn, tm=512, out_dtype=F32):
    m, k = x.shape
    n = w.shape[1]
    return pl.pallas_call(
        _proj_kernel,
        grid=(n // tn, m // tm),
        in_specs=[pl.BlockSpec((tm, k), lambda j, i: (i, 0)),
                  pl.BlockSpec((k, tn), lambda j, i: (0, j))],
        out_specs=pl.BlockSpec((tm, tn), lambda j, i: (i, j)),
        out_shape=jax.ShapeDtypeStruct((m, n), out_dtype),
        scratch_shapes=[pltpu.VMEM((k, tn), BF16)],
        compiler_params=_params(2),
        name="project",
    )(x, w)


def _proj_residual_kernel(y_ref, w_ref, x_ref, mod_ref, o_ref, wbf_ref, *, gate_idx):
    @pl.when(pl.program_id(1) == 0)
    def _():
        wbf_ref[...] = w_ref[...].astype(BF16)

    g = mod_ref[0][gate_idx:gate_idx + 1]
    o_ref[...] = x_ref[...] + g * _dot(y_ref[...], wbf_ref[...])


def project_residual(y, w, x, mods, gate_idx, n_ctx, tn=512, tm=512):
    m, k = y.shape
    n = w.shape[1]
    row = functools.partial(_cond_row, n_ctx_tiles=n_ctx // tm, tiles_per_sample=DEC_SEQ // tm)
    return pl.pallas_call(
        functools.partial(_proj_residual_kernel, gate_idx=gate_idx),
        grid=(n // tn, m // tm),
        in_specs=[pl.BlockSpec((tm, k), lambda j, i: (i, 0)),
                  pl.BlockSpec((k, tn), lambda j, i: (0, j)),
                  pl.BlockSpec((tm, tn), lambda j, i: (i, j)),
                  pl.BlockSpec((1, 6, tn), lambda j, i: (row(i), 0, j))],
        out_specs=pl.BlockSpec((tm, tn), lambda j, i: (i, j)),
        out_shape=jax.ShapeDtypeStruct((m, n), F32),
        scratch_shapes=[pltpu.VMEM((k, tn), BF16)],
        compiler_params=_params(2),
        name="project_residual",
    )(y, w, x, mods)


S5_CHUNK = 64
S5_ROWS = 8
S5_BLK = 8
S5_SCAN_LANES = 1024


def _s5_kernel(u_ref, wbr_ref, wbi_ref, wcr_ref, wci_ref, lam_ref, x0_ref, y_ref, fin_ref,
               bur_ref, bui_ref, st_ref, *, n_chunks):
    d = pl.program_id(0)
    c = pl.program_id(2)
    tc = S5_CHUNK
    cin = S5_BLK * S5_GROUP
    cst = S5_BLK * S5_STATE

    @pl.when(c == 0)
    def _():
        st_ref[...] = x0_ref[0]

    u = u_ref[...].reshape(tc * S5_ROWS, D_HALF).astype(BF16)
    for k in range(S5_GROUPS // S5_BLK):
        uk = u[:, k * cin:(k + 1) * cin]
        bur_ref[:, k * cst:(k + 1) * cst] = _dot(uk, wbr_ref[0, k])
        bui_ref[:, k * cst:(k + 1) * cst] = _dot(uk, wbi_ref[0, k])

    for j in range(S5_LANES // S5_SCAN_LANES):
        sl = slice(j * S5_SCAN_LANES, (j + 1) * S5_SCAN_LANES)
        lr = jnp.broadcast_to(lam_ref[0, 0:1, sl], (S5_ROWS, S5_SCAN_LANES))
        li = jnp.broadcast_to(lam_ref[0, 1:2, sl], (S5_ROWS, S5_SCAN_LANES))

        def step(i, carry, sl=sl, lr=lr, li=li):
            sr, si = carry
            t = jnp.where(d == 0, i, tc - 1 - i)
            row = pl.multiple_of(t * S5_ROWS, S5_ROWS)
            nr = lr * sr - li * si + bur_ref[pl.ds(row, S5_ROWS), sl]
            ni = lr * si + li * sr + bui_ref[pl.ds(row, S5_ROWS), sl]
            bur_ref[pl.ds(row, S5_ROWS), sl] = nr
            bui_ref[pl.ds(row, S5_ROWS), sl] = ni
            return nr, ni

        sr, si = lax.fori_loop(0, tc, step, (st_ref[0, :, sl], st_ref[1, :, sl]), unroll=4)
        st_ref[0, :, sl] = sr
        st_ref[1, :, sl] = si

    xr = bur_ref[...].astype(BF16)
    xi = bui_ref[...].astype(BF16)
    for k in range(S5_GROUPS // S5_BLK):
        yk = _dot(xr[:, k * cst:(k + 1) * cst], wcr_ref[0, k]) + _dot(xi[:, k * cst:(k + 1) * cst], wci_ref[0, k])
        y_ref[0, :, :, k * cin:(k + 1) * cin] = yk.reshape(tc, S5_ROWS, cin)

    @pl.when(c == n_chunks - 1)
    def _():
        fin_ref[0] = st_ref[...]


def s5_scan(u_t, x0, wbr, wbi, wcr, wci, lam):
    length, bsz, _ = u_t.shape
    n_chunks = length // S5_CHUNK
    nblk = S5_GROUPS // S5_BLK
    cin = S5_BLK * S5_GROUP
    cst = S5_BLK * S5_STATE

    def chunk(d, c):
        return jnp.where(d == 0, c, n_chunks - 1 - c)

    return pl.pallas_call(
        functools.partial(_s5_kernel, n_chunks=n_chunks),
        grid=(2, bsz // S5_ROWS, n_chunks),
        in_specs=[pl.BlockSpec((S5_CHUNK, S5_ROWS, D_HALF), lambda d, b, c: (chunk(d, c), b, 0)),
                  pl.BlockSpec((1, nblk, cin, cst), lambda d, b, c: (d, 0, 0, 0)),
                  pl.BlockSpec((1, nblk, cin, cst), lambda d, b, c: (d, 0, 0, 0)),
                  pl.BlockSpec((1, nblk, cst, cin), lambda d, b, c: (d, 0, 0, 0)),
                  pl.BlockSpec((1, nblk, cst, cin), lambda d, b, c: (d, 0, 0, 0)),
                  pl.BlockSpec((1, 2, S5_LANES), lambda d, b, c: (d, 0, 0)),
                  pl.BlockSpec((1, 2, S5_ROWS, S5_LANES), lambda d, b, c: (d, 0, b, 0))],
        out_specs=[pl.BlockSpec((1, S5_CHUNK, S5_ROWS, D_HALF), lambda d, b, c: (d, chunk(d, c), b, 0)),
                   pl.BlockSpec((1, 2, S5_ROWS, S5_LANES), lambda d, b, c: (d, 0, b, 0))],
        out_shape=[jax.ShapeDtypeStruct((2, length, bsz, D_HALF), F32),
                   jax.ShapeDtypeStruct((2, 2, bsz, S5_LANES), F32)],
        scratch_shapes=[pltpu.VMEM((S5_CHUNK * S5_ROWS, S5_LANES), F32),
                        pltpu.VMEM((S5_CHUNK * S5_ROWS, S5_LANES), F32),
                        pltpu.VMEM((2, S5_ROWS, S5_LANES), F32)],
        compiler_params=_params(3),
        name="s5_scan",
    )(u_t, wbr, wbi, wcr, wci, lam, x0)


def s5_weights(lam_re, lam_im, log_step, b_re, b_im, c_re, c_im):
    lam = lax.complex(lam_re.astype(F32), lam_im.astype(F32))
    lam_bar = jnp.exp(lam * jnp.exp(log_step.astype(F32))[..., None])
    b_bar = ((lam_bar - 1.0) / lam)[..., None] * lax.complex(b_re.astype(F32), b_im.astype(F32))
    nblk = S5_GROUPS // S5_BLK
    eye = jnp.eye(S5_BLK, dtype=F32)

    def embed_in(b):
        b = b.reshape(2, nblk, S5_BLK, S5_STATE, S5_GROUP)
        w = jnp.einsum('dkgpc,gh->dkgchp', b, eye)
        return w.reshape(2, nblk, S5_BLK * S5_GROUP, S5_BLK * S5_STATE).astype(BF16)

    def embed_out(cm):
        cm = cm.reshape(2, nblk, S5_BLK, S5_GROUP, S5_STATE)
        w = jnp.einsum('dkgcp,gh->dkgphc', cm, eye)
        return w.reshape(2, nblk, S5_BLK * S5_STATE, S5_BLK * S5_GROUP).astype(BF16)

    lam_rows = jnp.stack([lam_bar.real.reshape(2, S5_LANES), lam_bar.imag.reshape(2, S5_LANES)], axis=1)
    return (embed_in(b_bar.real), embed_in(b_bar.imag),
            embed_out(c_re.astype(F32)), embed_out(-c_im.astype(F32)), lam_rows)


def _glu_kernel(yf_ref, yb_ref, u_ref, d_ref, w_ref, b_ref, o_ref, wbf_ref):
    @pl.when(pl.program_id(0) == 0)
    def _():
        wbf_ref[...] = w_ref[...].astype(BF16)

    y = jax.nn.gelu(u_ref[...] * d_ref[...] + yf_ref[0] + yb_ref[0])
    o_ref[...] = (y * jax.nn.sigmoid(_dot(y.astype(BF16), wbf_ref[...]) + b_ref[...])).astype(BF16)


def s5_glu(y_dirs, u, d_skip, w_glu, b_glu, tm=512):
    m, n = u.shape
    return pl.pallas_call(
        _glu_kernel,
        grid=(m // tm,),
        in_specs=[pl.BlockSpec((1, tm, n), lambda i: (0, i, 0)),
                  pl.BlockSpec((1, tm, n), lambda i: (1, i, 0)),
                  pl.BlockSpec((tm, n), lambda i: (i, 0)),
                  pl.BlockSpec((1, n), lambda i: (0, 0)),
                  pl.BlockSpec((n, n), lambda i: (0, 0)),
                  pl.BlockSpec((1, n), lambda i: (0, 0))],
        out_specs=pl.BlockSpec((tm, n), lambda i: (i, 0)),
        out_shape=jax.ShapeDtypeStruct((m, n), BF16),
        scratch_shapes=[pltpu.VMEM((n, n), BF16)],
        compiler_params=_params(1),
        name="s5_glu",
    )(y_dirs, y_dirs, u, d_skip.reshape(1, n), w_glu, b_glu.reshape(1, n))


def s5_mixer(u, x0, s5p, weights):
    bsz, length, _ = u.shape
    bp = -(-bsz // S5_ROWS) * S5_ROWS
    wbr, wbi, wcr, wci, lam = weights
    d_skip, w_glu, b_glu = s5p
    u_t = jnp.swapaxes(u, 0, 1)
    if x0 is None:
        x0_t = jnp.zeros((2, 2, bp, S5_LANES), F32)
    else:
        x0_t = x0.astype(F32).reshape(bsz, 2, 2, S5_LANES).transpose(1, 2, 0, 3)
    if bp != bsz:
        u_t = jnp.pad(u_t, ((0, 0), (0, bp - bsz), (0, 0)))
        x0_t = jnp.pad(x0_t, ((0, 0), (0, 0), (0, bp - bsz), (0, 0)))
    y_dirs, fin = s5_scan(u_t, x0_t, wbr, wbi, wcr, wci, lam)
    y = s5_glu(y_dirs.reshape(2, length * bp, D_HALF), u_t.reshape(length * bp, D_HALF), d_skip, w_glu, b_glu)
    y = jnp.swapaxes(y.reshape(length, bp, D_HALF)[:, :bsz], 0, 1)
    fin = fin[:, :, :bsz].transpose(2, 0, 1, 3).reshape(bsz, 2, 2, S5_GROUPS, S5_STATE)
    return y, fin


def _head_rms(x, g):
    return x * lax.rsqrt(jnp.mean(x * x, axis=-1, keepdims=True) + EPS) * g


def _ctx_attn_kernel(q_ref, k_ref, v_ref, qg_ref, kg_ref, o_ref, kn_ref):
    scale = NA_HD ** -0.5
    for h in range(NA_HEADS):
        q = _head_rms(q_ref[0, h], qg_ref[...])
        k = _head_rms(k_ref[0, h], kg_ref[...])
        kn_ref[0, h] = k
        s = _dot_nt(q.astype(BF16), k.astype(BF16)) * scale
        p = jnp.exp(s - jnp.max(s, axis=-1, keepdims=True))
        l = jnp.sum(p, axis=-1, keepdims=True)
        o = _dot(p.astype(BF16), v_ref[0, h].astype(BF16)) / l
        o_ref[0, :, h * NA_HD:(h + 1) * NA_HD] = o.astype(BF16)


def context_attention(q, k, v, q_g, k_g):
    bsz, nh, t, hd = q.shape
    blk = pl.BlockSpec((1, nh, t, hd), lambda b: (b, 0, 0, 0))
    vec = pl.BlockSpec((1, hd), lambda b: (0, 0))
    return pl.pallas_call(
        _ctx_attn_kernel,
        grid=(bsz,),
        in_specs=[blk, blk, blk, vec, vec],
        out_specs=[pl.BlockSpec((1, t, nh * hd), lambda b: (b, 0, 0)), blk],
        out_shape=[jax.ShapeDtypeStruct((bsz, t, nh * hd), BF16), jax.ShapeDtypeStruct((bsz, nh, t, hd), F32)],
        compiler_params=_params(1),
        name="context_attention",
    )(q, k, v, q_g.reshape(1, hd), k_g.reshape(1, hd))


NA_PAIR = 2


def _na_kernel(q_ref, k_ref, v_ref, ck_ref, cv_ref, bias_ref, qg_ref, kg_ref, o_ref, kn_ref, vn_ref, *, rows):
    scale = NA_HD ** -0.5
    win = NA_KH * GRID_W
    for hh in range(NA_PAIR):
        kn_ref[...] = _head_rms(k_ref[0, hh], kg_ref[...]).astype(BF16)
        vn_ref[...] = v_ref[0, hh].astype(BF16)
        ck = ck_ref[0, hh].astype(BF16)
        cv = cv_ref[0, hh].astype(BF16)
        for r in range(rows):
            rs = min(max(r - NA_KH // 2, 0), rows - NA_KH)
            cls = rs - r + NA_KH - 1
            q = _head_rms(q_ref[0, hh, r * GRID_W:(r + 1) * GRID_W, :], qg_ref[...]).astype(BF16)
            kw = kn_ref[rs * GRID_W:rs * GRID_W + win, :]
            vw = vn_ref[rs * GRID_W:rs * GRID_W + win, :]
            s_w = _dot_nt(q, kw) * scale + bias_ref[hh, cls]
            s_c = _dot_nt(q, ck) * scale
            m = jnp.maximum(jnp.max(s_w, axis=-1, keepdims=True), jnp.max(s_c, axis=-1, keepdims=True))
            p_w = jnp.exp(s_w - m)
            p_c = jnp.exp(s_c - m)
            l = jnp.sum(p_w, axis=-1, keepdims=True) + jnp.sum(p_c, axis=-1, keepdims=True)
            o = (_dot(p_w.astype(BF16), vw) + _dot(p_c.astype(BF16), cv)) / l
            o_ref[0, r * GRID_W:(r + 1) * GRID_W, hh * NA_HD:(hh + 1) * NA_HD] = o.astype(BF16)


def na_bias_table(rpb, rows):
    qc = np.arange(GRID_W)
    kc = np.arange(GRID_W)
    c0 = np.clip(qc - NA_KW // 2, 0, GRID_W - NA_KW)
    ok = (kc[None, :] >= c0[:, None]) & (kc[None, :] < c0[:, None] + NA_KW)
    dc = np.clip(kc[None, :] - qc[:, None] + NA_KW - 1, 0, 2 * NA_KW - 2)
    dr = np.arange(NA_KH)[:, None] + np.arange(NA_KH)[None, :]
    b = rpb.astype(F32)[:, dr][:, :, :, dc]
    b = jnp.where(ok[None, None, None], b, NEG_INF)
    return b.transpose(0, 1, 3, 2, 4).reshape(rpb.shape[0], NA_KH, GRID_W, NA_KH * GRID_W)


def neighbourhood_attention(q, k, v, ck, cv, rpb, q_g, k_g):
    bsz, nh, length, hd = q.shape
    past = ck.shape[2]
    rows = length // GRID_W
    bias = na_bias_table(rpb, rows)
    blk = pl.BlockSpec((1, NA_PAIR, length, hd), lambda b, h: (b, h, 0, 0))
    cblk = pl.BlockSpec((1, NA_PAIR, past, hd), lambda b, h: (b, h, 0, 0))
    vec = pl.BlockSpec((1, hd), lambda b, h: (0, 0))
    return pl.pallas_call(
        functools.partial(_na_kernel, rows=rows),
        grid=(bsz, nh // NA_PAIR),
        in_specs=[blk, blk, blk, cblk, cblk,
                  pl.BlockSpec((NA_PAIR, NA_KH, GRID_W, NA_KH * GRID_W), lambda b, h: (h, 0, 0, 0)),
                  vec, vec],
        out_specs=pl.BlockSpec((1, length, NA_PAIR * hd), lambda b, h: (b, 0, h)),
        out_shape=jax.ShapeDtypeStruct((bsz, length, nh * hd), BF16),
        scratch_shapes=[pltpu.VMEM((length, hd), BF16), pltpu.VMEM((length, hd), BF16)],
        compiler_params=_params(2),
        name="neighbourhood_attention",
    )(q, k, v, ck, cv, bias, q_g.reshape(1, hd), k_g.reshape(1, hd))


def _heads_first(t, bsz, length, nh):
    return t.reshape(bsz, length, nh, -1).transpose(0, 2, 1, 3)


HY_EMB_PAD = 64


def _dot3(a, b):
    a_hi, a_lo = _split_bf16(a)
    b_hi, b_lo = _split_bf16(b)
    return _dot(a_hi, b_hi) + _dot(a_hi, b_lo) + _dot(a_lo, b_hi)


def _hyena_filter_kernel(z_ref, w1_ref, b1_ref, f_ref, w2_ref, b2_ref, w3_ref, dec_ref, o_ref):
    length = z_ref.shape[0]
    h = jnp.sin(f_ref[0:1, :] * (_dot3(z_ref[...], w1_ref[...]) + b1_ref[...]))
    h = jnp.sin(f_ref[1:2, :] * (_dot3(h, w2_ref[...]) + b2_ref[...]))
    filt = _dot3(h, w3_ref[...])
    t = lax.broadcasted_iota(jnp.int32, filt.shape, 0).astype(F32)
    offset = jnp.abs(t - float(length // 2)) / float(length)
    filt = filt * jnp.exp(-offset * jnp.abs(dec_ref[...]))
    o_ref[...] = filt / (jnp.sum(jnp.abs(filt), axis=0, keepdims=True) + EPS)


def _hyena_embedding(length):
    t = np.arange(length, dtype=np.float32) / np.float32(length)
    bands = np.linspace(1e-4, HY_BANDS - 1, HY_BANDS, dtype=np.float32)
    ang = (2 * math.pi * t[:, None] * bands[None, :]).astype(np.float32)
    z = np.concatenate([t[:, None], np.cos(ang), np.sin(ang)], axis=-1).astype(np.float32)
    return np.pad(z, ((0, 0), (0, HY_EMB_PAD - z.shape[1])))


def hyena_filter(length, w1, b1, freq, w2, b2, w3, decay):
    z = jnp.asarray(_hyena_embedding(length))
    w1p = jnp.pad(w1.astype(F32), ((0, HY_EMB_PAD - w1.shape[0]), (0, 0)))
    n = w3.shape[1]
    return pl.pallas_call(
        _hyena_filter_kernel,
        out_shape=jax.ShapeDtypeStruct((length, n), F32),
        compiler_params=_params(0),
        name="hyena_filter",
    )(z, w1p, b1.reshape(1, -1), freq, w2, b2.reshape(1, -1), w3, decay.reshape(1, n))


def _dft_tile(length):
    return min(length, 512)


def _dft_matrices(length):
    n = 2 * length
    tf = _dft_tile(length)
    f = np.arange(length)
    s = np.arange(length)
    ang = 2 * np.pi * ((f[:, None] * s[None, :]) % n) / n
    fc = np.cos(ang)
    fs = -np.sin(ang)
    fs[0] = np.where(s % 2 == 0, 1.0, -1.0)
    nt = np.arange(length) + length // 2
    ang_i = 2 * np.pi * ((nt[:, None] * f[None, :]) % n) / n
    wf = np.where(f == 0, 1.0, 2.0)[None, :]
    gc = wf * np.cos(ang_i) / n
    gs = -wf * np.sin(ang_i) / n
    gs[:, 0] = np.where(nt % 2 == 0, 1.0, -1.0) / n
    n_tiles = length // tf
    fwd = np.concatenate([fc.reshape(n_tiles, tf, length), fs.reshape(n_tiles, tf, length)], axis=1)
    inv = np.concatenate([gc.reshape(length, n_tiles, tf), gs.reshape(length, n_tiles, tf)], axis=2)
    return (jnp.asarray(fwd.reshape(n_tiles * 2 * tf, length), BF16),
            jnp.asarray(inv.transpose(1, 0, 2), BF16))


def _shift_rows(x, up):
    n = x.shape[0]
    row = lax.broadcasted_iota(jnp.int32, x.shape, 0)
    if up:
        return jnp.where(row == n - 1, 0.0, pltpu.roll(x, n - 1, 0))
    return jnp.where(row == 0, 0.0, pltpu.roll(x, 1, 0))


def _conv3(x, w, b):
    return _shift_rows(x, False) * w[0:1] + x * w[1:2] + _shift_rows(x, True) * w[2:3] + b


def _hyena_conv_kernel(x0_ref, x1_ref, v_ref, w0_ref, w1_ref, wv_ref, b0_ref, b1_ref, bv_ref, bias_ref,
                       f_ref, g_ref, hf_ref, o_ref, gated_ref, gbf_ref, acc_ref, *, n_tiles, tf):
    j = pl.program_id(2)

    @pl.when(j == 0)
    def _():
        gated = _conv3(x1_ref[0], w1_ref[...], b1_ref[...]) * _conv3(v_ref[0], wv_ref[...], bv_ref[...])
        gated_ref[...] = gated
        gbf_ref[...] = gated.astype(BF16)
        acc_ref[...] = jnp.zeros_like(acc_ref)

    u = _dot(f_ref[...], gbf_ref[...])
    ur, ui = u[:tf], u[tf:]
    hr, hi = hf_ref[:tf, :], hf_ref[tf:, :]
    packed = (lax.broadcasted_iota(jnp.int32, ur.shape, 0) == 0) & (j == 0)
    yr = ur * hr - jnp.where(packed, 0.0, ui * hi)
    yi = jnp.where(packed, ui * hi, ur * hi + ui * hr)
    y = jnp.concatenate([yr, yi], axis=0).astype(BF16)
    acc_ref[...] += _dot(g_ref[0], y)

    @pl.when(j == n_tiles - 1)
    def _():
        gated = gated_ref[...]
        y = acc_ref[...] + gated * bias_ref[...]
        o_ref[0] = (_conv3(x0_ref[0], w0_ref[...], b0_ref[...]) * y).astype(BF16)


def hyena_mixer(zh, filt, short_w, short_b, bias):
    bsz, length, _ = zh.shape
    fwd, inv = _dft_matrices(length)
    tf = _dft_tile(length)
    n_tiles = length // tf
    hf = project(fwd, filt, tn=512, tm=min(512, 2 * length))
    cn = 256 if length > 512 else 512
    nct = D_HALF // cn

    def zblk(k):
        return pl.BlockSpec((1, length, cn), lambda b, c, j, k=k: (b, 0, k * nct + c))

    def wblk(k):
        return pl.BlockSpec((3, cn), lambda b, c, j, k=k: (0, k * nct + c))

    def bblk(k):
        return pl.BlockSpec((1, cn), lambda b, c, j, k=k: (0, k * nct + c))

    sb = short_b.reshape(1, -1)
    return pl.pallas_call(
        functools.partial(_hyena_conv_kernel, n_tiles=n_tiles, tf=tf),
        grid=(bsz, nct, n_tiles),
        in_specs=[zblk(0), zblk(1), zblk(2), wblk(0), wblk(1), wblk(2), bblk(0), bblk(1), bblk(2),
                  pl.BlockSpec((1, cn), lambda b, c, j: (0, c)),
                  pl.BlockSpec((2 * tf, length), lambda b, c, j: (j, 0)),
                  pl.BlockSpec((1, length, 2 * tf), lambda b, c, j: (j, 0, 0)),
                  pl.BlockSpec((2 * tf, cn), lambda b, c, j: (j, c))],
        out_specs=pl.BlockSpec((1, length, cn), lambda b, c, j: (b, 0, c)),
        out_shape=jax.ShapeDtypeStruct((bsz, length, D_HALF), BF16),
        scratch_shapes=[pltpu.VMEM((length, cn), F32), pltpu.VMEM((length, cn), BF16), pltpu.VMEM((length, cn), F32)],
        compiler_params=_params(3),
        name="hyena_conv",
    )(zh, zh, zh, short_w, short_w, short_w, sb, sb, sb, bias.reshape(1, -1), fwd, inv, hf)


RW_TL = 256
RW_PRE_TL = 128
RW_HB = 4
RW_LANE_CHUNK = 128
RW_UNROLL = 16


def _rwkv_pre_kernel(z_ref, zp_ref, zn_ref, mu_ref, wl_ref, w0_ref, a0_ref, kk_ref, ka_ref, ones_ref,
                     r_ref, k_ref, v_ref, g_ref, an_ref, w0o_ref, kd0_ref, b0_ref, w1o_ref, kd1_ref, b1_ref,
                     *, n_tiles):
    i = pl.program_id(1)
    z = z_ref[0]
    row = lax.broadcasted_iota(jnp.int32, z.shape, 0)
    prev = jnp.where(i > 0, zp_ref[0, 7:8, :], 0.0)
    nxt = jnp.where(i < n_tiles - 1, zn_ref[0, 0:1, :], 0.0)
    zm1 = jnp.where(row == 0, prev, pltpu.roll(z, 1, 0))
    zp1 = jnp.where(row == z.shape[0] - 1, nxt, pltpu.roll(z, z.shape[0] - 1, 0))
    x = z + (0.5 * (zm1 + zp1) - z) * mu_ref[...]
    r = x[:, 0:D_HALF]
    k = x[:, D_HALF:2 * D_HALF]
    v = x[:, 2 * D_HALF:3 * D_HALF]
    lo = x[:, 3 * D_HALF:3 * D_HALF + RW_LORA]
    lane = lax.broadcasted_iota(jnp.int32, lo.shape, 1)
    act = jnp.where(lane < RW_W_RANK, jnp.tanh(lo),
                    jnp.where(lane < RW_W_RANK + RW_A_RANK, lo, jax.nn.sigmoid(lo)))
    up = _dot(act.astype(BF16), wl_ref[...])
    g = up[:, 4 * D_HALF:5 * D_HALF]
    kk = k * kk_ref[...]
    sq_hi, sq_lo = _split_bf16(kk * kk)
    kk = kk * lax.rsqrt(_dot(sq_hi, ones_ref[...]) + _dot(sq_lo, ones_ref[...]) + EPS)

    def put(ref, val):
        for h in range(RW_HEADS):
            ref[0, h] = val[:, h * RW_HD:(h + 1) * RW_HD]

    put(r_ref, r)
    put(k_ref, k)
    put(v_ref, v)
    put(g_ref, g)
    put(an_ref, -kk)
    for d, (wo, kdo, bo) in enumerate(((w0o_ref, kd0_ref, b0_ref), (w1o_ref, kd1_ref, b1_ref))):
        logw = -jax.nn.softplus(-(w0_ref[d:d + 1, :] + up[:, d * D_HALF:(d + 1) * D_HALF])) - 0.5
        a = jax.nn.sigmoid(a0_ref[d:d + 1, :] + up[:, (2 + d) * D_HALF:(3 + d) * D_HALF])
        put(wo, jnp.exp(-jnp.exp(logw)))
        put(kdo, k * (1.0 + (a - 1.0) * ka_ref[...]))
        put(bo, kk * a)


def rwkv_pre(zr, mu, w_lora, w0, a0, k_k, k_a):
    bsz, length, width = zr.shape
    tl = RW_PRE_TL
    n_tiles = length // tl
    ones_blk = jnp.asarray(np.kron(np.eye(RW_HEADS), np.ones((RW_HD, RW_HD))), BF16)
    vec = lambda n: pl.BlockSpec((n, D_HALF), lambda b, i: (0, 0))
    out_blk = pl.BlockSpec((1, RW_HEADS, tl, RW_HD), lambda b, i: (b, 0, i, 0))
    out_sds = jax.ShapeDtypeStruct((bsz, RW_HEADS, length, RW_HD), F32)
    return pl.pallas_call(
        functools.partial(_rwkv_pre_kernel, n_tiles=n_tiles),
        grid=(bsz, n_tiles),
        in_specs=[pl.BlockSpec((1, tl, width), lambda b, i: (b, i, 0)),
                  pl.BlockSpec((1, 8, width), lambda b, i: (b, jnp.maximum(i * (tl // 8) - 1, 0), 0)),
                  pl.BlockSpec((1, 8, width), lambda b, i: (b, jnp.minimum((i + 1) * (tl // 8), length // 8 - 1), 0)),
                  pl.BlockSpec((1, width), lambda b, i: (0, 0)),
                  pl.BlockSpec((RW_LORA, 5 * D_HALF), lambda b, i: (0, 0)),
                  vec(2), vec(2), vec(1), vec(1),
                  pl.BlockSpec((D_HALF, D_HALF), lambda b, i: (0, 0))],
        out_specs=[out_blk] * 11,
        out_shape=[out_sds] * 11,
        compiler_params=_params(2),
        name="rwkv_pre",
    )(zr, zr, zr, mu, w_lora, w0, a0, k_k.reshape(1, -1), k_a.reshape(1, -1), ones_blk)


def _rwkv_scan_kernel(r_ref, w_ref, k_ref, a_ref, b_ref, vt_ref, s0_ref, yt_ref, fin_ref, st_ref, *, reverse, n_chunks):
    c = pl.program_id(2)

    @pl.when(c == 0)
    def _():
        st_ref[...] = s0_ref[0]

    lane = lax.broadcasted_iota(jnp.int32, (RW_HD, RW_LANE_CHUNK), 1)
    n_groups = RW_TL // RW_UNROLL
    yt_ref[...] = jnp.zeros_like(yt_ref)

    def group(gi, states):
        g = (n_groups - 1 - gi) if reverse else gi
        t0 = pl.multiple_of(g * RW_UNROLL, RW_UNROLL)
        base = pl.multiple_of((t0 // RW_LANE_CHUNK) * RW_LANE_CHUNK, RW_LANE_CHUNK)
        off = t0 - base
        vts = [pltpu.roll(vt_ref[0, h, :, pl.ds(base, RW_LANE_CHUNK)], (RW_LANE_CHUNK - off) % RW_LANE_CHUNK, 1)
               for h in range(RW_HB)]
        ys = [yt_ref[0, h, :, pl.ds(base, RW_LANE_CHUNK)] for h in range(RW_HB)]
        states = list(states)
        order = range(RW_UNROLL - 1, -1, -1) if reverse else range(RW_UNROLL)
        for j in order:
            for h in range(RW_HB):
                t = pl.ds(t0 + j, 1)
                s = states[h]
                sa = jnp.sum(s * a_ref[0, h, t, :], axis=1, keepdims=True)
                s = s * w_ref[0, h, t, :] + sa * b_ref[0, h, t, :] + vts[h][:, j:j + 1] * k_ref[0, h, t, :]
                y = jnp.sum(s * r_ref[0, h, t, :], axis=1, keepdims=True)
                ys[h] = jnp.where(lane == off + j, y, ys[h])
                states[h] = s
        for h in range(RW_HB):
            yt_ref[0, h, :, pl.ds(base, RW_LANE_CHUNK)] = ys[h]
        return tuple(states)

    states = lax.fori_loop(0, n_groups, group, tuple(st_ref[h] for h in range(RW_HB)))
    for h in range(RW_HB):
        st_ref[h] = states[h]

    @pl.when(c == n_chunks - 1)
    def _():
        fin_ref[0] = st_ref[...]


def rwkv_scan(r, w, k, a, b, vt, s0, reverse):
    bsz, nh, length, hd = r.shape
    n_chunks = length // RW_TL

    def chunk(c):
        return (n_chunks - 1 - c) if reverse else c

    row_blk = pl.BlockSpec((1, RW_HB, RW_TL, hd), lambda bi, h, c: (bi, h, chunk(c), 0))
    col_blk = pl.BlockSpec((1, RW_HB, hd, RW_TL), lambda bi, h, c: (bi, h, 0, chunk(c)))
    st_blk = pl.BlockSpec((1, RW_HB, hd, hd), lambda bi, h, c: (bi, h, 0, 0))
    return pl.pallas_call(
        functools.partial(_rwkv_scan_kernel, reverse=reverse, n_chunks=n_chunks),
        grid=(bsz, nh // RW_HB, n_chunks),
        in_specs=[row_blk] * 5 + [col_blk, st_blk],
        out_specs=[col_blk, st_blk],
        out_shape=[jax.ShapeDtypeStruct((bsz, nh, hd, length), F32), jax.ShapeDtypeStruct((bsz, nh, hd, hd), F32)],
        scratch_shapes=[pltpu.VMEM((RW_HB, hd, hd), F32)],
        compiler_params=_params(3),
        name="rwkv_scan_rev" if reverse else "rwkv_scan_fwd",
    )(r, w, k, a, b, vt, s0)


def _rwkv_post_kernel(yf_ref, yb_ref, r_ref, k_ref, v_ref, g_ref, rk_ref, lg_ref, lb_ref, o_ref):
    for h in range(RW_HEADS):
        y = yf_ref[0, h] + yb_ref[0, h]
        mean = jnp.mean(y, axis=-1, keepdims=True)
        var = jnp.mean(jnp.square(y - mean), axis=-1, keepdims=True)
        y = (y - mean) * lax.rsqrt(var + GN_EPS) * lg_ref[h:h + 1, :] + lb_ref[h:h + 1, :]
        v = v_ref[0, h]
        bonus = jnp.sum(r_ref[0, h] * k_ref[0, h] * rk_ref[h:h + 1, :], axis=-1, keepdims=True) * v
        o_ref[0, :, h * RW_HD:(h + 1) * RW_HD] = ((y + bonus) * g_ref[0, h]).astype(BF16)


def rwkv_post(yf, yb, r, k, v, g, r_k, ln_g, ln_b):
    bsz, nh, length, hd = r.shape
    tl = RW_TL
    blk = pl.BlockSpec((1, nh, tl, hd), lambda b, i: (b, 0, i, 0))
    vec = pl.BlockSpec((nh, hd), lambda b, i: (0, 0))
    return pl.pallas_call(
        _rwkv_post_kernel,
        grid=(bsz, length // tl),
        in_specs=[blk] * 6 + [vec] * 3,
        out_specs=pl.BlockSpec((1, tl, nh * hd), lambda b, i: (b, i, 0)),
        out_shape=jax.ShapeDtypeStruct((bsz, length, nh * hd), BF16),
        compiler_params=_params(2),
        name="rwkv_post",
    )(yf, yb, r, k, v, g, r_k, ln_g.reshape(nh, hd), ln_b.reshape(nh, hd))


def rwkv_lora_weights(w2, a2, g2):
    w = jnp.zeros((RW_LORA, 5 * D_HALF), F32)
    for d in range(2):
        w = w.at[0:RW_W_RANK, d * D_HALF:(d + 1) * D_HALF].set(w2[d].astype(F32))
        w = w.at[RW_W_RANK:RW_W_RANK + RW_A_RANK, (2 + d) * D_HALF:(3 + d) * D_HALF].set(a2[d].astype(F32))
    w = w.at[RW_W_RANK + RW_A_RANK:, 4 * D_HALF:].set(g2.astype(F32))
    return w.astype(BF16)


def rwkv_mixer(zr, s0, rwp):
    mu, w0, w2, a0, a2, g2, k_k, k_a, r_k, ln_g, ln_b = rwp
    bsz = zr.shape[0]
    mu_p = jnp.pad(mu.astype(F32), (0, RW_IN_PAD - RW_IN)).reshape(1, RW_IN_PAD)
    r, k, v, g, an, wd0, kd0, b0, wd1, kd1, b1 = rwkv_pre(zr, mu_p, rwkv_lora_weights(w2, a2, g2), w0, a0, k_k, k_a)
    vt = jnp.swapaxes(v, 2, 3)
    if s0 is None:
        s0 = jnp.zeros((bsz, 2, RW_HEADS, RW_HD, RW_HD), F32)
    ytf, fin_f = rwkv_scan(r, wd0, kd0, an, b0, vt, s0[:, 0].astype(F32), reverse=False)
    ytb, fin_b = rwkv_scan(r, wd1, kd1, an, b1, vt, s0[:, 1].astype(F32), reverse=True)
    y = rwkv_post(jnp.swapaxes(ytf, 2, 3), jnp.swapaxes(ytb, 2, 3), r, k, v, g, r_k, ln_g, ln_b)
    return y, jnp.stack([fin_f, fin_b], axis=1)


ROUTE_TILE = 512


def _rank_before(vals, n):
    idx = lax.broadcasted_iota(jnp.int32, vals.shape, 0)
    cnt = jnp.zeros(vals.shape, F32)
    for e in range(n):
        row = vals[e:e + 1]
        cnt = cnt + jnp.where((row > vals) | ((row == vals) & (idx > e)), 1.0, 0.0)
    return cnt


def _route_kernel(lg_ref, bias_ref, tril_ref, triu_ref, te_ref, gt_ref, rk_ref, cnt_ref, carry_ref):
    @pl.when(pl.program_id(0) == 0)
    def _():
        carry_ref[...] = jnp.zeros_like(carry_ref)

    t = lg_ref.shape[1]
    per = N_EXPERTS // N_ROUTE_GROUPS
    scores = jax.nn.sigmoid(lg_ref[...])
    sel = scores + bias_ref[...]
    sel3 = sel.reshape(N_ROUTE_GROUPS, per, t)
    m1 = jnp.max(sel3, axis=1, keepdims=True)
    within = lax.broadcasted_iota(jnp.int32, sel3.shape, 1)
    first = jnp.min(jnp.where(sel3 == m1, within, per), axis=1, keepdims=True)
    m2 = jnp.max(jnp.where(within == first, -jnp.inf, sel3), axis=1, keepdims=True)
    group_score = (m1 + m2).reshape(N_ROUTE_GROUPS, t)
    group_ok = _rank_before(group_score, N_ROUTE_GROUPS) < TOPK_ROUTE_GROUPS
    expert_ok = jnp.broadcast_to(group_ok.reshape(N_ROUTE_GROUPS, 1, t), sel3.shape).reshape(N_EXPERTS, t)
    masked = jnp.where(expert_ok, sel, NEG_INF)
    chosen = _rank_before(masked, N_EXPERTS) < TOP_K
    gates = jnp.where(chosen, scores, 0.0)
    gates = gates / jnp.sum(gates, axis=0, keepdims=True) * ROUTED_SCALE
    onehot = jnp.where(chosen, 1.0, 0.0).astype(BF16)
    order = _dot(tril_ref[...], onehot)
    before = carry_ref[:, 0:1] + _dot(onehot, triu_ref[...])
    carry_ref[...] = carry_ref[...] + jnp.sum(jnp.where(chosen, 1.0, 0.0), axis=1, keepdims=True)
    eidx = lax.broadcasted_iota(jnp.int32, chosen.shape, 0).astype(F32)
    for j in range(TOP_K):
        pick = chosen & (order == float(j + 1))
        te_ref[j:j + 1, :] = jnp.sum(jnp.where(pick, eidx, 0.0), axis=0, keepdims=True).astype(jnp.int32)
        gt_ref[j:j + 1, :] = jnp.sum(jnp.where(pick, gates, 0.0), axis=0, keepdims=True)
        rk_ref[j:j + 1, :] = jnp.sum(jnp.where(pick, before, 0.0), axis=0, keepdims=True).astype(jnp.int32)
    cnt_ref[...] = carry_ref[...]


def route(logits_t, b_router):
    e, n = logits_t.shape
    t = ROUTE_TILE
    tril = jnp.asarray(np.tril(np.ones((e, e))), BF16)
    triu = jnp.asarray(np.triu(np.ones((t, t)), 1), BF16)
    out_blk = pl.BlockSpec((TOP_K, t), lambda i: (0, i))
    te, gt, rk, cnt = pl.pallas_call(
        _route_kernel,
        grid=(n // t,),
        in_specs=[pl.BlockSpec((e, t), lambda i: (0, i)),
                  pl.BlockSpec((e, 1), lambda i: (0, 0)),
                  pl.BlockSpec((e, e), lambda i: (0, 0)),
                  pl.BlockSpec((t, t), lambda i: (0, 0))],
        out_specs=[out_blk, out_blk, out_blk, pl.BlockSpec((e, 128), lambda i: (0, 0))],
        out_shape=[jax.ShapeDtypeStruct((TOP_K, n), jnp.int32), jax.ShapeDtypeStruct((TOP_K, n), F32),
                   jax.ShapeDtypeStruct((TOP_K, n), jnp.int32), jax.ShapeDtypeStruct((e, 128), F32)],
        scratch_shapes=[pltpu.VMEM((e, 128), F32)],
        compiler_params=_params(1),
        name="route",
    )(logits_t, b_router.astype(F32).reshape(e, 1), tril, triu)
    return te, gt, rk, cnt[:, 0]


def _expert_kernel(be_ref, nu_ref, x_ref, wg_ref, wu_ref, wd_ref, o_ref, wgb_ref, wub_ref, wdb_ref):
    i = pl.program_id(0)
    used = i < nu_ref[0]
    fresh = (i == 0) | (be_ref[i] != be_ref[jnp.maximum(i - 1, 0)])

    @pl.when(used & fresh)
    def _():
        wgb_ref[...] = wg_ref[0].astype(BF16)
        wub_ref[...] = wu_ref[0].astype(BF16)
        wdb_ref[...] = wd_ref[0].astype(BF16)

    @pl.when(used)
    def _():
        x = x_ref[...]
        gate = _dot(x, wgb_ref[...])
        h = (gate * jax.nn.sigmoid(gate) * _dot(x, wub_ref[...])).astype(BF16)
        o_ref[...] = _dot(h, wdb_ref[...]).astype(o_ref.dtype)

    @pl.when(jnp.logical_not(used))
    def _():
        o_ref[...] = jnp.zeros_like(o_ref)


def expert_ffn(xs, block_expert, n_used, w_gate, w_up, w_down):
    s, d = xs.shape
    de = w_gate.shape[2]
    nb = s // MOE_ROWS
    grid_spec = pltpu.PrefetchScalarGridSpec(
        num_scalar_prefetch=2,
        grid=(nb,),
        in_specs=[pl.BlockSpec((MOE_ROWS, d), lambda i, be, nu: (i, 0)),
                  pl.BlockSpec((1, d, de), lambda i, be, nu: (be[i], 0, 0)),
                  pl.BlockSpec((1, d, de), lambda i, be, nu: (be[i], 0, 0)),
                  pl.BlockSpec((1, de, d), lambda i, be, nu: (be[i], 0, 0))],
        out_specs=pl.BlockSpec((MOE_ROWS, d), lambda i, be, nu: (i, 0)),
        scratch_shapes=[pltpu.VMEM((d, de), BF16), pltpu.VMEM((d, de), BF16), pltpu.VMEM((de, d), BF16)],
    )
    return pl.pallas_call(
        _expert_kernel,
        grid_spec=grid_spec,
        out_shape=jax.ShapeDtypeStruct((s, d), BF16),
        compiler_params=_params(1),
        name="expert_ffn",
    )(block_expert, n_used, xs, w_gate, w_up, w_down)


def _combine_kernel(x_ref, yg_ref, gt_ref, sh_ref, mod_ref, o_ref, *, gate_idx):
    routed = jnp.zeros(x_ref.shape, F32)
    for j in range(TOP_K):
        routed = routed + gt_ref[:, j:j + 1] * yg_ref[j].astype(F32)
    g = mod_ref[0][gate_idx:gate_idx + 1]
    o_ref[...] = x_ref[...] + g * (routed + sh_ref[...].astype(F32))


def moe_combine(x, yg, gates, shared, mods, gate_idx, n_ctx, tm=128):
    n, d = x.shape
    row = functools.partial(_cond_row, n_ctx_tiles=n_ctx // tm, tiles_per_sample=DEC_SEQ // tm)
    return pl.pallas_call(
        functools.partial(_combine_kernel, gate_idx=gate_idx),
        grid=(n // tm,),
        in_specs=[pl.BlockSpec((tm, d), lambda i: (i, 0)),
                  pl.BlockSpec((TOP_K, tm, d), lambda i: (0, i, 0)),
                  pl.BlockSpec((tm, TOP_K), lambda i: (i, 0)),
                  pl.BlockSpec((tm, d), lambda i: (i, 0)),
                  pl.BlockSpec((1, 6, d), lambda i: (row(i), 0, 0))],
        out_specs=pl.BlockSpec((tm, d), lambda i: (i, 0)),
        out_shape=jax.ShapeDtypeStruct((n, d), F32),
        compiler_params=_params(1),
        name="moe_combine",
    )(x, yg, gates, shared, mods)


def moe_ffn(x, h, logits_t, mods, gate_idx, n_ctx, b_router, w_gate, w_up, w_down, ws_gate, ws_up, ws_down):
    n, d = h.shape
    top_e, gates, rank, counts = route(logits_t, b_router)
    counts = counts.astype(jnp.int32)
    padded = (counts + MOE_ROWS - 1) // MOE_ROWS * MOE_ROWS
    pad_end = jnp.cumsum(padded)
    pad_start = pad_end - padded
    n_slots = n * TOP_K + N_EXPERTS * MOE_ROWS
    nb = n_slots // MOE_ROWS
    pos = pad_start[top_e] + rank
    tok = jnp.broadcast_to(jnp.arange(n, dtype=jnp.int32)[None, :], pos.shape)
    slot_tok = jnp.full((n_slots,), n, jnp.int32).at[pos.reshape(-1)].set(tok.reshape(-1), unique_indices=True)
    block_expert = jnp.minimum(
        jnp.searchsorted(pad_end, jnp.arange(nb, dtype=jnp.int32) * MOE_ROWS, side='right'), N_EXPERTS - 1).astype(jnp.int32)
    n_used = (pad_end[-1:] // MOE_ROWS).astype(jnp.int32)
    h_pad = jnp.concatenate([h, jnp.zeros((1, d), h.dtype)], axis=0)
    xs = h_pad[slot_tok]
    yb = expert_ffn(xs, block_expert, n_used, w_gate, w_up, w_down)
    shared = expert_ffn(h, jnp.zeros((n // MOE_ROWS,), jnp.int32), jnp.full((1,), n // MOE_ROWS, jnp.int32),
                        ws_gate[None], ws_up[None], ws_down[None])
    yg = yb[pos]
    return moe_combine(x, yg, gates.T, shared, mods, gate_idx, n_ctx)


def _layer_ab(h, n_ctx, bsz_ctx, bsz_lat, p):
    z = project(h, p['w_in'], tn=512)
    s5w = s5_weights(*p['s5_disc'])
    outs = []
    extras = {}
    for name, lo, bsz, length in (('ctx', 0, bsz_ctx, SEQ), ('lat', n_ctx, bsz_lat, DEC_SEQ)):
        zg = z[lo:lo + bsz * length].reshape(bsz, length, 4 * D_HALF)
        u = zg[..., :D_HALF]
        q, k, v = (_heads_first(zg[..., (1 + j) * D_HALF:(2 + j) * D_HALF], bsz, length, NA_HEADS) for j in range(3))
        if name == 'ctx':
            y_a, s5_fin = s5_mixer(u, None, p['s5_glu'], s5w)
            y_b, k_n = context_attention(q, k, v, p['q_g'], p['k_g'])
            extras = {'k': k_n, 'v': v, 's5': s5_fin}
        else:
            y_a, _ = s5_mixer(u, p['s5_state'], p['s5_glu'], s5w)
            y_b = neighbourhood_attention(q, k, v, p['ck'], p['cv'], p['rpb'], p['q_g'], p['k_g'])
        outs.append(jnp.concatenate([y_a, y_b], axis=-1).reshape(bsz * length, 2 * D_HALF))
    return jnp.concatenate(outs, axis=0), extras


def _layer_cd(h, n_ctx, bsz_ctx, bsz_lat, p):
    w_in = p['w_in']
    zh_all = project(h, w_in[:, :3 * D_HALF], tn=512)
    w_rw = jnp.pad(w_in[:, 3 * D_HALF:], ((0, 0), (0, RW_IN_PAD - RW_IN)))
    zr_all = project(h, w_rw, tn=512)
    outs = []
    extras = {}
    for name, lo, bsz, length in (('ctx', 0, bsz_ctx, SEQ), ('lat', n_ctx, bsz_lat, DEC_SEQ)):
        zh = zh_all[lo:lo + bsz * length].reshape(bsz, length, 3 * D_HALF)
        zr = zr_all[lo:lo + bsz * length].reshape(bsz, length, RW_IN_PAD)
        filt = hyena_filter(length, *p['hy_filter'])
        y_c = hyena_mixer(zh, filt, p['hy_short_w'], p['hy_short_b'], p['hy_bias'])
        y_d, rw_fin = rwkv_mixer(zr, None if name == 'ctx' else p['rw_state'], p['rwp'])
        if name == 'ctx':
            extras = {'rw': rw_fin}
        outs.append(jnp.concatenate([y_c, y_d], axis=-1).reshape(bsz * length, 2 * D_HALF))
    return jnp.concatenate(outs, axis=0), extras


def kernel(x_prompt, x_sample, cache_na_k, cache_na_v, state_s5, state_rwkv, c, c_ctx, ada_w, ada_b, norm_mix, norm_ffn, ab_w_in, ab_w_out, s5_lam_re, s5_lam_im, s5_log_step, s5_b_re, s5_b_im, s5_c_re, s5_c_im, s5_d, s5_w_glu, s5_b_glu, na_q_norm, na_k_norm, na_rpb, cd_w_in, cd_w_out, hy_short_w, hy_short_b, hy_w1, hy_b1, hy_freq, hy_w2, hy_b2, hy_w3, hy_decay, hy_bias, rw_mu, rw_w0, rw_w2, rw_a0, rw_a2, rw_g2, rw_k_k, rw_k_a, rw_r_k, rw_ln_g, rw_ln_b, moe_router, moe_router_bias, moe_w_gate, moe_w_up, moe_w_down, moe_ws_gate, moe_ws_up, moe_ws_down):
    bsz_ctx, seq, d = x_prompt.shape
    bsz_lat, dec_seq, _ = x_sample.shape
    assert (seq, dec_seq, d) == (SEQ, DEC_SEQ, D_MODEL) and bsz_lat + 1 <= 8
    depth = ada_w.shape[0]
    n_ctx = bsz_ctx * seq
    x = jnp.concatenate([x_prompt.reshape(n_ctx, d), x_sample.reshape(bsz_lat * dec_seq, d)], axis=0).astype(F32)

    cond = jnp.concatenate([c_ctx[None, :], c, jnp.zeros((8 - 1 - bsz_lat, d), c.dtype)], axis=0).astype(F32)
    mods_all = ada_table(cond, ada_w, ada_b).reshape(depth, 8, 6, d)

    new_k, new_v, new_s5, new_rw = [], [], [], []
    for l in range(depth):
        mods = mods_all[l]
        i = l // 2
        h = modulate(x, norm_mix[l], mods, 0, 1, n_ctx)
        if l % 2 == 0:
            p = {'w_in': ab_w_in[i], 'q_g': na_q_norm[i], 'k_g': na_k_norm[i], 'rpb': na_rpb[i],
                 'ck': cache_na_k[:, i], 'cv': cache_na_v[:, i], 's5_state': state_s5[:, i],
                 's5_disc': (s5_lam_re[i], s5_lam_im[i], s5_log_step[i], s5_b_re[i], s5_b_im[i], s5_c_re[i], s5_c_im[i]),
                 's5_glu': (s5_d[i], s5_w_glu[i], s5_b_glu[i])}
            y, ex = _layer_ab(h, n_ctx, bsz_ctx, bsz_lat, p)
            new_k.append(ex['k'])
            new_v.append(ex['v'])
            new_s5.append(ex['s5'])
            w_out = ab_w_out[i]
        else:
            p = {'w_in': cd_w_in[i], 'hy_filter': (hy_w1[i], hy_b1[i], hy_freq[i], hy_w2[i], hy_b2[i], hy_w3[i], hy_decay[i]),
                 'hy_short_w': hy_short_w[i], 'hy_short_b': hy_short_b[i], 'hy_bias': hy_bias[i],
                 'rw_state': state_rwkv[:, i],
                 'rwp': (rw_mu[i], rw_w0[i], rw_w2[i], rw_a0[i], rw_a2[i], rw_g2[i], rw_k_k[i], rw_k_a[i], rw_r_k[i],
                         rw_ln_g[i], rw_ln_b[i])}
            y, ex = _layer_cd(h, n_ctx, bsz_ctx, bsz_lat, p)
            new_rw.append(ex['rw'])
            w_out = cd_w_out[i]
        x = project_residual(y, w_out, x, mods, 2, n_ctx)
        h, logits_t = modulate(x, norm_ffn[l], mods, 3, 4, n_ctx, w_router_t=moe_router[l].T)
        x = moe_ffn(x, h, logits_t, mods, 5, n_ctx, moe_router_bias[l], moe_w_gate[l], moe_w_up[l], moe_w_down[l],
                    moe_ws_gate[l], moe_ws_up[l], moe_ws_down[l])

    y_prompt = x[:n_ctx].reshape(bsz_ctx, seq, d)
    y_sample = x[n_ctx:].reshape(bsz_lat, dec_seq, d)
    return (y_prompt, y_sample, jnp.stack(new_k, axis=1), jnp.stack(new_v, axis=1),
            jnp.stack(new_s5, axis=1), jnp.stack(new_rw, axis=1))
```

```python
import functools
import math

import numpy as np
import jax
import jax.numpy as jnp
from jax import lax
from jax.experimental import pallas as pl
from jax.experimental.pallas import tpu as pltpu

F32 = jnp.float32
BF16 = jnp.bfloat16

D_MODEL = 2048
D_HALF = 1024
SEQ = 256
DEC_SEQ = 2048
GRID_W = 64
S5_GROUP = 16
S5_GROUPS = 64
S5_STATE = 64
S5_LANES = S5_GROUPS * S5_STATE
NA_HEADS = 16
NA_HD = 64
NA_KH = 8
NA_KW = 16
HY_BANDS = 16
RW_HEADS = 16
RW_HD = 64
RW_W_RANK = 64
RW_A_RANK = 64
RW_G_RANK = 128
RW_LORA = 256
RW_IN = 3 * D_HALF + RW_LORA
RW_IN_PAD = 3584
N_EXPERTS = 64
TOP_K = 8
N_ROUTE_GROUPS = 8
TOPK_ROUTE_GROUPS = 4
D_EXPERT = 512
ROUTED_SCALE = 2.5
EPS = 1e-6
GN_EPS = 64e-5
NEG_INF = -1e30

TOK_TILE = 256
MOE_ROWS = 256
VMEM_LIMIT = 56 * 1024 * 1024


def _params(n_axes, vmem=VMEM_LIMIT):
    return pltpu.CompilerParams(dimension_semantics=("arbitrary",) * n_axes, vmem_limit_bytes=vmem)


def _dot(a, b):
    return jnp.dot(a, b, preferred_element_type=F32)


def _dot_nt(a, b):
    return lax.dot_general(a, b, (((1,), (1,)), ((), ())), preferred_element_type=F32)


def _split_bf16(x):
    hi = x.astype(BF16)
    lo = (x - hi.astype(F32)).astype(BF16)
    return hi, lo


def _cond_row(i, n_ctx_tiles, tiles_per_sample):
    return jnp.where(i < n_ctx_tiles, 0, 1 + (i - n_ctx_tiles) // tiles_per_sample)


def _ada_kernel(c_ref, w_ref, b_ref, o_ref):
    c = c_ref[...]
    s = (c * jax.nn.sigmoid(c)).astype(BF16)
    o_ref[0] = _dot(s, w_ref[0].astype(BF16)) + b_ref[0]


def ada_table(cond8, ada_w, ada_b):
    depth, d, n = ada_w.shape
    tn = 1024
    return pl.pallas_call(
        _ada_kernel,
        grid=(depth, n // tn),
        in_specs=[pl.BlockSpec((8, d), lambda l, j: (0, 0)),
                  pl.BlockSpec((1, d, tn), lambda l, j: (l, 0, j)),
                  pl.BlockSpec((1, 1, tn), lambda l, j: (l, 0, j))],
        out_specs=pl.BlockSpec((1, 8, tn), lambda l, j: (l, 0, j)),
        out_shape=jax.ShapeDtypeStruct((depth, 8, n), F32),
        compiler_params=_params(2),
        name="ada_table",
    )(cond8, ada_w, ada_b.reshape(depth, 1, n))


def _modulate_kernel(x_ref, g_ref, mod_ref, h_ref, *, shift_idx, scale_idx):
    x = x_ref[...]
    y = x * lax.rsqrt(jnp.mean(x * x, axis=-1, keepdims=True) + EPS)
    m = mod_ref[0]
    h = y * g_ref[...] * (1.0 + m[scale_idx:scale_idx + 1]) + m[shift_idx:shift_idx + 1]
    h_ref[...] = h.astype(BF16)


def _modulate_router_kernel(x_ref, g_ref, mod_ref, wr_ref, h_ref, lg_ref, *, shift_idx, scale_idx):
    x = x_ref[...]
    y = x * lax.rsqrt(jnp.mean(x * x, axis=-1, keepdims=True) + EPS)
    m = mod_ref[0]
    h = y * g_ref[...] * (1.0 + m[scale_idx:scale_idx + 1]) + m[shift_idx:shift_idx + 1]
    h_hi, h_lo = _split_bf16(h)
    h_ref[...] = h_hi
    w_hi, w_lo = _split_bf16(wr_ref[...])
    lg_ref[...] = _dot_nt(w_hi, h_hi) + _dot_nt(w_hi, h_lo) + _dot_nt(w_lo, h_hi)


def modulate(x, gain, mods, shift_idx, scale_idx, n_ctx, w_router_t=None):
    n, d = x.shape
    tm = TOK_TILE
    row = functools.partial(_cond_row, n_ctx_tiles=n_ctx // tm, tiles_per_sample=DEC_SEQ // tm)
    in_specs = [pl.BlockSpec((tm, d), lambda i: (i, 0)),
                pl.BlockSpec((1, d), lambda i: (0, 0)),
                pl.BlockSpec((1, 6, d), lambda i: (row(i), 0, 0))]
    if w_router_t is None:
        return pl.pallas_call(
            functools.partial(_modulate_kernel, shift_idx=shift_idx, scale_idx=scale_idx),
            grid=(n // tm,),
            in_specs=in_specs,
            out_specs=pl.BlockSpec((tm, d), lambda i: (i, 0)),
            out_shape=jax.ShapeDtypeStruct((n, d), BF16),
            compiler_params=_params(1),
            name="modulate",
        )(x, gain.reshape(1, d), mods)
    e = w_router_t.shape[0]
    return pl.pallas_call(
        functools.partial(_modulate_router_kernel, shift_idx=shift_idx, scale_idx=scale_idx),
        grid=(n // tm,),
        in_specs=in_specs + [pl.BlockSpec((e, d), lambda i: (0, 0))],
        out_specs=[pl.BlockSpec((tm, d), lambda i: (i, 0)),
                   pl.BlockSpec((e, tm), lambda i: (0, i))],
        out_shape=[jax.ShapeDtypeStruct((n, d), BF16), jax.ShapeDtypeStruct((e, n), F32)],
        compiler_params=_params(1),
        name="modulate_router",
    )(x, gain.reshape(1, d), mods, w_router_t)


def _proj_kernel(x_ref, w_ref, o_ref, wbf_ref):
    @pl.when(pl.program_id(1) == 0)
    def _():
        wbf_ref[...] = w_ref[...].astype(BF16)

    o_ref[...] = _dot(x_ref[...], wbf_ref[...]).astype(o_ref.dtype)


def project(x, w, tn, tm=512, out_dtype=F32):
    m, k = x.shape
    n = w.shape[1]
    return pl.pallas_call(
        _proj_kernel,
        grid=(n // tn, m // tm),
        in_specs=[pl.BlockSpec((tm, k), lambda j, i: (i, 0)),
                  pl.BlockSpec((k, tn), lambda j, i: (0, j))],
        out_specs=pl.BlockSpec((tm, tn), lambda j, i: (i, j)),
        out_shape=jax.ShapeDtypeStruct((m, n), out_dtype),
        scratch_shapes=[pltpu.VMEM((k, tn), BF16)],
        compiler_params=_params(2),
        name="project",
    )(x, w)


def _proj_residual_kernel(y_ref, w_ref, x_ref, mod_ref, o_ref, wbf_ref, *, gate_idx):
    @pl.when(pl.program_id(1) == 0)
    def _():
        wbf_ref[...] = w_ref[...].astype(BF16)

    g = mod_ref[0][gate_idx:gate_idx + 1]
    o_ref[...] = x_ref[...] + g * _dot(y_ref[...], wbf_ref[...])


def project_residual(y, w, x, mods, gate_idx, n_ctx, tn=512, tm=512):
    m, k = y.shape
    n = w.shape[1]
    row = functools.partial(_cond_row, n_ctx_tiles=n_ctx // tm, tiles_per_sample=DEC_SEQ // tm)
    return pl.pallas_call(
        functools.partial(_proj_residual_kernel, gate_idx=gate_idx),
        grid=(n // tn, m // tm),
        in_specs=[pl.BlockSpec((tm, k), lambda j, i: (i, 0)),
                  pl.BlockSpec((k, tn), lambda j, i: (0, j)),
                  pl.BlockSpec((tm, tn), lambda j, i: (i, j)),
                  pl.BlockSpec((1, 6, tn), lambda j, i: (row(i), 0, j))],
        out_specs=pl.BlockSpec((tm, tn), lambda j, i: (i, j)),
        out_shape=jax.ShapeDtypeStruct((m, n), F32),
        scratch_shapes=[pltpu.VMEM((k, tn), BF16)],
        compiler_params=_params(2),
        name="project_residual",
    )(y, w, x, mods)


S5_CHUNK = 64
S5_ROWS = 8
S5_BLK = 8
S5_SCAN_LANES = 1024


def _s5_kernel(u_ref, wbr_ref, wbi_ref, wcr_ref, wci_ref, lam_ref, x0_ref, y_ref, fin_ref,
               bur_ref, bui_ref, st_ref, *, n_chunks):
    d = pl.program_id(0)
    c = pl.program_id(2)
    tc = S5_CHUNK
    cin = S5_BLK * S5_GROUP
    cst = S5_BLK * S5_STATE

    @pl.when(c == 0)
    def _():
        st_ref[...] = x0_ref[0]

    u = u_ref[...].reshape(tc * S5_ROWS, D_HALF).astype(BF16)
    for k in range(S5_GROUPS // S5_BLK):
        uk = u[:, k * cin:(k + 1) * cin]
        bur_ref[:, k * cst:(k + 1) * cst] = _dot(uk, wbr_ref[0, k])
        bui_ref[:, k * cst:(k + 1) * cst] = _dot(uk, wbi_ref[0, k])

    for j in range(S5_LANES // S5_SCAN_LANES):
        sl = slice(j * S5_SCAN_LANES, (j + 1) * S5_SCAN_LANES)
        lr = jnp.broadcast_to(lam_ref[0, 0:1, sl], (S5_ROWS, S5_SCAN_LANES))
        li = jnp.broadcast_to(lam_ref[0, 1:2, sl], (S5_ROWS, S5_SCAN_LANES))

        def step(i, carry, sl=sl, lr=lr, li=li):
            sr, si = carry
            t = jnp.where(d == 0, i, tc - 1 - i)
            row = pl.multiple_of(t * S5_ROWS, S5_ROWS)
            nr = lr * sr - li * si + bur_ref[pl.ds(row, S5_ROWS), sl]
            ni = lr * si + li * sr + bui_ref[pl.ds(row, S5_ROWS), sl]
            bur_ref[pl.ds(row, S5_ROWS), sl] = nr
            bui_ref[pl.ds(row, S5_ROWS), sl] = ni
            return nr, ni

        sr, si = lax.fori_loop(0, tc, step, (st_ref[0, :, sl], st_ref[1, :, sl]), unroll=4)
        st_ref[0, :, sl] = sr
        st_ref[1, :, sl] = si

    xr = bur_ref[...].astype(BF16)
    xi = bui_ref[...].astype(BF16)
    for k in range(S5_GROUPS // S5_BLK):
        yk = _dot(xr[:, k * cst:(k + 1) * cst], wcr_ref[0, k]) + _dot(xi[:, k * cst:(k + 1) * cst], wci_ref[0, k])
        y_ref[0, :, :, k * cin:(k + 1) * cin] = yk.reshape(tc, S5_ROWS, cin)

    @pl.when(c == n_chunks - 1)
    def _():
        fin_ref[0] = st_ref[...]


def s5_scan(u_t, x0, wbr, wbi, wcr, wci, lam):
    length, bsz, _ = u_t.shape
    n_chunks = length // S5_CHUNK
    nblk = S5_GROUPS // S5_BLK
    cin = S5_BLK * S5_GROUP
    cst = S5_BLK * S5_STATE

    def chunk(d, c):
        return jnp.where(d == 0, c, n_chunks - 1 - c)

    return pl.pallas_call(
        functools.partial(_s5_kernel, n_chunks=n_chunks),
        grid=(2, bsz // S5_ROWS, n_chunks),
        in_specs=[pl.BlockSpec((S5_CHUNK, S5_ROWS, D_HALF), lambda d, b, c: (chunk(d, c), b, 0)),
                  pl.BlockSpec((1, nblk, cin, cst), lambda d, b, c: (d, 0, 0, 0)),
                  pl.BlockSpec((1, nblk, cin, cst), lambda d, b, c: (d, 0, 0, 0)),
                  pl.BlockSpec((1, nblk, cst, cin), lambda d, b, c: (d, 0, 0, 0)),
                  pl.BlockSpec((1, nblk, cst, cin), lambda d, b, c: (d, 0, 0, 0)),
                  pl.BlockSpec((1, 2, S5_LANES), lambda d, b, c: (d, 0, 0)),
                  pl.BlockSpec((1, 2, S5_ROWS, S5_LANES), lambda d, b, c: (d, 0, b, 0))],
        out_specs=[pl.BlockSpec((1, S5_CHUNK, S5_ROWS, D_HALF), lambda d, b, c: (d, chunk(d, c), b, 0)),
                   pl.BlockSpec((1, 2, S5_ROWS, S5_LANES), lambda d, b, c: (d, 0, b, 0))],
        out_shape=[jax.ShapeDtypeStruct((2, length, bsz, D_HALF), F32),
                   jax.ShapeDtypeStruct((2, 2, bsz, S5_LANES), F32)],
        scratch_shapes=[pltpu.VMEM((S5_CHUNK * S5_ROWS, S5_LANES), F32),
                        pltpu.VMEM((S5_CHUNK * S5_ROWS, S5_LANES), F32),
                        pltpu.VMEM((2, S5_ROWS, S5_LANES), F32)],
        compiler_params=_params(3),
        name="s5_scan",
    )(u_t, wbr, wbi, wcr, wci, lam, x0)


def s5_weights(lam_re, lam_im, log_step, b_re, b_im, c_re, c_im):
    lam = lax.complex(lam_re.astype(F32), lam_im.astype(F32))
    lam_bar = jnp.exp(lam * jnp.exp(log_step.astype(F32))[..., None])
    b_bar = ((lam_bar - 1.0) / lam)[..., None] * lax.complex(b_re.astype(F32), b_im.astype(F32))
    nblk = S5_GROUPS // S5_BLK
    eye = jnp.eye(S5_BLK, dtype=F32)

    def embed_in(b):
        b = b.reshape(2, nblk, S5_BLK, S5_STATE, S5_GROUP)
        w = jnp.einsum('dkgpc,gh->dkgchp', b, eye)
        return w.reshape(2, nblk, S5_BLK * S5_GROUP, S5_BLK * S5_STATE).astype(BF16)

    def embed_out(cm):
        cm = cm.reshape(2, nblk, S5_BLK, S5_GROUP, S5_STATE)
        w = jnp.einsum('dkgcp,gh->dkgphc', cm, eye)
        return w.reshape(2, nblk, S5_BLK * S5_STATE, S5_BLK * S5_GROUP).astype(BF16)

    lam_rows = jnp.stack([lam_bar.real.reshape(2, S5_LANES), lam_bar.imag.reshape(2, S5_LANES)], axis=1)
    return (embed_in(b_bar.real), embed_in(b_bar.imag),
            embed_out(c_re.astype(F32)), embed_out(-c_im.astype(F32)), lam_rows)


def _glu_kernel(yf_ref, yb_ref, u_ref, d_ref, w_ref, b_ref, o_ref, wbf_ref):
    @pl.when(pl.program_id(0) == 0)
    def _():
        wbf_ref[...] = w_ref[...].astype(BF16)

    y = jax.nn.gelu(u_ref[...] * d_ref[...] + yf_ref[0] + yb_ref[0])
    o_ref[...] = (y * jax.nn.sigmoid(_dot(y.astype(BF16), wbf_ref[...]) + b_ref[...])).astype(BF16)


def s5_glu(y_dirs, u, d_skip, w_glu, b_glu, tm=512):
    m, n = u.shape
    return pl.pallas_call(
        _glu_kernel,
        grid=(m // tm,),
        in_specs=[pl.BlockSpec((1, tm, n), lambda i: (0, i, 0)),
                  pl.BlockSpec((1, tm, n), lambda i: (1, i, 0)),
                  pl.BlockSpec((tm, n), lambda i: (i, 0)),
                  pl.BlockSpec((1, n), lambda i: (0, 0)),
                  pl.BlockSpec((n, n), lambda i: (0, 0)),
                  pl.BlockSpec((1, n), lambda i: (0, 0))],
        out_specs=pl.BlockSpec((tm, n), lambda i: (i, 0)),
        out_shape=jax.ShapeDtypeStruct((m, n), BF16),
        scratch_shapes=[pltpu.VMEM((n, n), BF16)],
        compiler_params=_params(1),
        name="s5_glu",
    )(y_dirs, y_dirs, u, d_skip.reshape(1, n), w_glu, b_glu.reshape(1, n))


def s5_mixer(u, x0, s5p, weights):
    bsz, length, _ = u.shape
    bp = -(-bsz // S5_ROWS) * S5_ROWS
    wbr, wbi, wcr, wci, lam = weights
    d_skip, w_glu, b_glu = s5p
    u_t = jnp.swapaxes(u, 0, 1)
    if x0 is None:
        x0_t = jnp.zeros((2, 2, bp, S5_LANES), F32)
    else:
        x0_t = x0.astype(F32).reshape(bsz, 2, 2, S5_LANES).transpose(1, 2, 0, 3)
    if bp != bsz:
        u_t = jnp.pad(u_t, ((0, 0), (0, bp - bsz), (0, 0)))
        x0_t = jnp.pad(x0_t, ((0, 0), (0, 0), (0, bp - bsz), (0, 0)))
    y_dirs, fin = s5_scan(u_t, x0_t, wbr, wbi, wcr, wci, lam)
    y = s5_glu(y_dirs.reshape(2, length * bp, D_HALF), u_t.reshape(length * bp, D_HALF), d_skip, w_glu, b_glu)
    y = jnp.swapaxes(y.reshape(length, bp, D_HALF)[:, :bsz], 0, 1)
    fin = fin[:, :, :bsz].transpose(2, 0, 1, 3).reshape(bsz, 2, 2, S5_GROUPS, S5_STATE)
    return y, fin


def _head_rms(x, g):
    return x * lax.rsqrt(jnp.mean(x * x, axis=-1, keepdims=True) + EPS) * g


def _ctx_attn_kernel(q_ref, k_ref, v_ref, qg_ref, kg_ref, o_ref, kn_ref):
    scale = NA_HD ** -0.5
    for h in range(NA_HEADS):
        q = _head_rms(q_ref[0, h], qg_ref[...])
        k = _head_rms(k_ref[0, h], kg_ref[...])
        kn_ref[0, h] = k
        s = _dot_nt(q.astype(BF16), k.astype(BF16)) * scale
        p = jnp.exp(s - jnp.max(s, axis=-1, keepdims=True))
        l = jnp.sum(p, axis=-1, keepdims=True)
        o = _dot(p.astype(BF16), v_ref[0, h].astype(BF16)) / l
        o_ref[0, :, h * NA_HD:(h + 1) * NA_HD] = o.astype(BF16)


def context_attention(q, k, v, q_g, k_g):
    bsz, nh, t, hd = q.shape
    blk = pl.BlockSpec((1, nh, t, hd), lambda b: (b, 0, 0, 0))
    vec = pl.BlockSpec((1, hd), lambda b: (0, 0))
    return pl.pallas_call(
        _ctx_attn_kernel,
        grid=(bsz,),
        in_specs=[blk, blk, blk, vec, vec],
        out_specs=[pl.BlockSpec((1, t, nh * hd), lambda b: (b, 0, 0)), blk],
        out_shape=[jax.ShapeDtypeStruct((bsz, t, nh * hd), BF16), jax.ShapeDtypeStruct((bsz, nh, t, hd), F32)],
        compiler_params=_params(1),
        name="context_attention",
    )(q, k, v, q_g.reshape(1, hd), k_g.reshape(1, hd))


NA_PAIR = 2


def _na_kernel(q_ref, k_ref, v_ref, ck_ref, cv_ref, bias_ref, qg_ref, kg_ref, o_ref, kn_ref, vn_ref, *, rows):
    scale = NA_HD ** -0.5
    win = NA_KH * GRID_W
    for hh in range(NA_PAIR):
        kn_ref[...] = _head_rms(k_ref[0, hh], kg_ref[...]).astype(BF16)
        vn_ref[...] = v_ref[0, hh].astype(BF16)
        ck = ck_ref[0, hh].astype(BF16)
        cv = cv_ref[0, hh].astype(BF16)
        for r in range(rows):
            rs = min(max(r - NA_KH // 2, 0), rows - NA_KH)
            cls = rs - r + NA_KH - 1
            q = _head_rms(q_ref[0, hh, r * GRID_W:(r + 1) * GRID_W, :], qg_ref[...]).astype(BF16)
            kw = kn_ref[rs * GRID_W:rs * GRID_W + win, :]
            vw = vn_ref[rs * GRID_W:rs * GRID_W + win, :]
            s_w = _dot_nt(q, kw) * scale + bias_ref[hh, cls]
            s_c = _dot_nt(q, ck) * scale
            m = jnp.maximum(jnp.max(s_w, axis=-1, keepdims=True), jnp.max(s_c, axis=-1, keepdims=True))
            p_w = jnp.exp(s_w - m)
            p_c = jnp.exp(s_c - m)
            l = jnp.sum(p_w, axis=-1, keepdims=True) + jnp.sum(p_c, axis=-1, keepdims=True)
            o = (_dot(p_w.astype(BF16), vw) + _dot(p_c.astype(BF16), cv)) / l
            o_ref[0, r * GRID_W:(r + 1) * GRID_W, hh * NA_HD:(hh + 1) * NA_HD] = o.astype(BF16)


def na_bias_table(rpb, rows):
    qc = np.arange(GRID_W)
    kc = np.arange(GRID_W)
    c0 = np.clip(qc - NA_KW // 2, 0, GRID_W - NA_KW)
    ok = (kc[None, :] >= c0[:, None]) & (kc[None, :] < c0[:, None] + NA_KW)
    dc = np.clip(kc[None, :] - qc[:, None] + NA_KW - 1, 0, 2 * NA_KW - 2)
    dr = np.arange(NA_KH)[:, None] + np.arange(NA_KH)[None, :]
    b = rpb.astype(F32)[:, dr][:, :, :, dc]
    b = jnp.where(ok[None, None, None], b, NEG_INF)
    return b.transpose(0, 1, 3, 2, 4).reshape(rpb.shape[0], NA_KH, GRID_W, NA_KH * GRID_W)


def neighbourhood_attention(q, k, v, ck, cv, rpb, q_g, k_g):
    bsz, nh, length, hd = q.shape
    past = ck.shape[2]
    rows = length // GRID_W
    bias = na_bias_table(rpb, rows)
    blk = pl.BlockSpec((1, NA_PAIR, length, hd), lambda b, h: (b, h, 0, 0))
    cblk = pl.BlockSpec((1, NA_PAIR, past, hd), lambda b, h: (b, h, 0, 0))
    vec = pl.BlockSpec((1, hd), lambda b, h: (0, 0))
    return pl.pallas_call(
        functools.partial(_na_kernel, rows=rows),
        grid=(bsz, nh // NA_PAIR),
        in_specs=[blk, blk, blk, cblk, cblk,
                  pl.BlockSpec((NA_PAIR, NA_KH, GRID_W, NA_KH * GRID_W), lambda b, h: (h, 0, 0, 0)),
                  vec, vec],
        out_specs=pl.BlockSpec((1, length, NA_PAIR * hd), lambda b, h: (b, 0, h)),
        out_shape=jax.ShapeDtypeStruct((bsz, length, nh * hd), BF16),
        scratch_shapes=[pltpu.VMEM((length, hd), BF16), pltpu.VMEM((length, hd), BF16)],
        compiler_params=_params(2),
        name="neighbourhood_attention",
    )(q, k, v, ck, cv, bias, q_g.reshape(1, hd), k_g.reshape(1, hd))


def _heads_first(t, bsz, length, nh):
    return t.reshape(bsz, length, nh, -1).transpose(0, 2, 1, 3)


HY_EMB_PAD = 64


def _dot3(a, b):
    a_hi, a_lo = _split_bf16(a)
    b_hi, b_lo = _split_bf16(b)
    return _dot(a_hi, b_hi) + _dot(a_hi, b_lo) + _dot(a_lo, b_hi)


def _hyena_filter_kernel(z_ref, w1_ref, b1_ref, f_ref, w2_ref, b2_ref, w3_ref, dec_ref, o_ref):
    length = z_ref.shape[0]
    h = jnp.sin(f_ref[0:1, :] * (_dot3(z_ref[...], w1_ref[...]) + b1_ref[...]))
    h = jnp.sin(f_ref[1:2, :] * (_dot3(h, w2_ref[...]) + b2_ref[...]))
    filt = _dot3(h, w3_ref[...])
    t = lax.broadcasted_iota(jnp.int32, filt.shape, 0).astype(F32)
    offset = jnp.abs(t - float(length // 2)) / float(length)
    filt = filt * jnp.exp(-offset * jnp.abs(dec_ref[...]))
    o_ref[...] = filt / (jnp.sum(jnp.abs(filt), axis=0, keepdims=True) + EPS)


def _hyena_embedding(length):
    t = np.arange(length, dtype=np.float32) / np.float32(length)
    bands = np.linspace(1e-4, HY_BANDS - 1, HY_BANDS, dtype=np.float32)
    ang = (2 * math.pi * t[:, None] * bands[None, :]).astype(np.float32)
    z = np.concatenate([t[:, None], np.cos(ang), np.sin(ang)], axis=-1).astype(np.float32)
    return np.pad(z, ((0, 0), (0, HY_EMB_PAD - z.shape[1])))


def hyena_filter(length, w1, b1, freq, w2, b2, w3, decay):
    z = jnp.asarray(_hyena_embedding(length))
    w1p = jnp.pad(w1.astype(F32), ((0, HY_EMB_PAD - w1.shape[0]), (0, 0)))
    n = w3.shape[1]
    return pl.pallas_call(
        _hyena_filter_kernel,
        out_shape=jax.ShapeDtypeStruct((length, n), F32),
        compiler_params=_params(0),
        name="hyena_filter",
    )(z, w1p, b1.reshape(1, -1), freq, w2, b2.reshape(1, -1), w3, decay.reshape(1, n))


def _dft_tile(length):
    return min(length, 512)


def _dft_matrices(length):
    n = 2 * length
    tf = _dft_tile(length)
    f = np.arange(length)
    s = np.arange(length)
    ang = 2 * np.pi * ((f[:, None] * s[None, :]) % n) / n
    fc = np.cos(ang)
    fs = -np.sin(ang)
    fs[0] = np.where(s % 2 == 0, 1.0, -1.0)
    nt = np.arange(length) + length // 2
    ang_i = 2 * np.pi * ((nt[:, None] * f[None, :]) % n) / n
    wf = np.where(f == 0, 1.0, 2.0)[None, :]
    gc = wf * np.cos(ang_i) / n
    gs = -wf * np.sin(ang_i) / n
    gs[:, 0] = np.where(nt % 2 == 0, 1.0, -1.0) / n
    n_tiles = length // tf
    fwd = np.concatenate([fc.reshape(n_tiles, tf, length), fs.reshape(n_tiles, tf, length)], axis=1)
    inv = np.concatenate([gc.reshape(length, n_tiles, tf), gs.reshape(length, n_tiles, tf)], axis=2)
    return (jnp.asarray(fwd.reshape(n_tiles * 2 * tf, length), BF16),
            jnp.asarray(inv.transpose(1, 0, 2), BF16))


def _shift_rows(x, up):
    n = x.shape[0]
    row = lax.broadcasted_iota(jnp.int32, x.shape, 0)
    if up:
        return jnp.where(row == n - 1, 0.0, pltpu.roll(x, n - 1, 0))
    return jnp.where(row == 0, 0.0, pltpu.roll(x, 1, 0))


def _conv3(x, w, b):
    return _shift_rows(x, False) * w[0:1] + x * w[1:2] + _shift_rows(x, True) * w[2:3] + b


def _hyena_conv_kernel(x0_ref, x1_ref, v_ref, w0_ref, w1_ref, wv_ref, b0_ref, b1_ref, bv_ref, bias_ref,
                       f_ref, g_ref, hf_ref, o_ref, gated_ref, gbf_ref, acc_ref, *, n_tiles, tf):
    j = pl.program_id(2)

    @pl.when(j == 0)
    def _():
        gated = _conv3(x1_ref[0], w1_ref[...], b1_ref[...]) * _conv3(v_ref[0], wv_ref[...], bv_ref[...])
        gated_ref[...] = gated
        gbf_ref[...] = gated.astype(BF16)
        acc_ref[...] = jnp.zeros_like(acc_ref)

    u = _dot(f_ref[...], gbf_ref[...])
    ur, ui = u[:tf], u[tf:]
    hr, hi = hf_ref[:tf, :], hf_ref[tf:, :]
    packed = (lax.broadcasted_iota(jnp.int32, ur.shape, 0) == 0) & (j == 0)
    yr = ur * hr - jnp.where(packed, 0.0, ui * hi)
    yi = jnp.where(packed, ui * hi, ur * hi + ui * hr)
    y = jnp.concatenate([yr, yi], axis=0).astype(BF16)
    acc_ref[...] += _dot(g_ref[0], y)

    @pl.when(j == n_tiles - 1)
    def _():
        gated = gated_ref[...]
        y = acc_ref[...] + gated * bias_ref[...]
        o_ref[0] = (_conv3(x0_ref[0], w0_ref[...], b0_ref[...]) * y).astype(BF16)


def hyena_mixer(zh, filt, short_w, short_b, bias):
    bsz, length, _ = zh.shape
    fwd, inv = _dft_matrices(length)
    tf = _dft_tile(length)
    n_tiles = length // tf
    hf = project(fwd, filt, tn=512, tm=min(512, 2 * length))
    cn = 256 if length > 512 else 512
    nct = D_HALF // cn

    def zblk(k):
        return pl.BlockSpec((1, length, cn), lambda b, c, j, k=k: (b, 0, k * nct + c))

    def wblk(k):
        return pl.BlockSpec((3, cn), lambda b, c, j, k=k: (0, k * nct + c))

    def bblk(k):
        return pl.BlockSpec((1, cn), lambda b, c, j, k=k: (0, k * nct + c))

    sb = short_b.reshape(1, -1)
    return pl.pallas_call(
        functools.partial(_hyena_conv_kernel, n_tiles=n_tiles, tf=tf),
        grid=(bsz, nct, n_tiles),
        in_specs=[zblk(0), zblk(1), zblk(2), wblk(0), wblk(1), wblk(2), bblk(0), bblk(1), bblk(2),
                  pl.BlockSpec((1, cn), lambda b, c, j: (0, c)),
                  pl.BlockSpec((2 * tf, length), lambda b, c, j: (j, 0)),
                  pl.BlockSpec((1, length, 2 * tf), lambda b, c, j: (j, 0, 0)),
                  pl.BlockSpec((2 * tf, cn), lambda b, c, j: (j, c))],
        out_specs=pl.BlockSpec((1, length, cn), lambda b, c, j: (b, 0, c)),
        out_shape=jax.ShapeDtypeStruct((bsz, length, D_HALF), BF16),
        scratch_shapes=[pltpu.VMEM((length, cn), F32), pltpu.VMEM((length, cn), BF16), pltpu.VMEM((length, cn), F32)],
        compiler_params=_params(3),
        name="hyena_conv",
    )(zh, zh, zh, short_w, short_w, short_w, sb, sb, sb, bias.reshape(1, -1), fwd, inv, hf)


RW_TL = 128
RW_PRE_TL = 128
RW_NB = 4
RW_PAIRS = RW_HEADS // 2
RW_YBLK = 64
LANES = 128


def _segment_ones():
    k = np.arange(2 * LANES)[:, None]
    n = np.arange(LANES)[None, :]
    return jnp.asarray((k % LANES) // RW_HD == n // RW_HD, BF16)


def _hi_lo(x):
    hi, lo = _split_bf16(x)
    return jnp.concatenate([hi, lo], axis=1)


def _head_sums(x, seg_ref):
    tiles = [_dot(_hi_lo(x[:, j * LANES:(j + 1) * LANES]), seg_ref[...]) for j in range(x.shape[1] // LANES)]
    return jnp.concatenate(tiles, axis=1)


def _rwkv_pre_kernel(z_ref, zp_ref, zn_ref, mu_ref, wl_ref, w0_ref, a0_ref, kk_ref, ka_ref, seg_ref,
                     r_ref, k_ref, v_ref, g_ref, an_ref, w0o_ref, kd0_ref, b0_ref, w1o_ref, kd1_ref, b1_ref,
                     *, n_tiles):
    i = pl.program_id(1)
    z = z_ref[0]
    row = lax.broadcasted_iota(jnp.int32, z.shape, 0)
    prev = jnp.where(i > 0, zp_ref[0, 7:8, :], 0.0)
    nxt = jnp.where(i < n_tiles - 1, zn_ref[0, 0:1, :], 0.0)
    zm1 = jnp.where(row == 0, prev, pltpu.roll(z, 1, 0))
    zp1 = jnp.where(row == z.shape[0] - 1, nxt, pltpu.roll(z, z.shape[0] - 1, 0))
    x = z + (0.5 * (zm1 + zp1) - z) * mu_ref[...]
    r = x[:, 0:D_HALF]
    k = x[:, D_HALF:2 * D_HALF]
    v = x[:, 2 * D_HALF:3 * D_HALF]
    lo = x[:, 3 * D_HALF:3 * D_HALF + RW_LORA]
    lane = lax.broadcasted_iota(jnp.int32, lo.shape, 1)
    act = jnp.where(lane < RW_W_RANK, jnp.tanh(lo),
                    jnp.where(lane < RW_W_RANK + RW_A_RANK, lo, jax.nn.sigmoid(lo)))
    up = _dot(act.astype(BF16), wl_ref[...])
    g = up[:, 4 * D_HALF:5 * D_HALF]
    kk = k * kk_ref[...]
    kk = kk * lax.rsqrt(_head_sums(kk * kk, seg_ref) + EPS)

    def put(ref, val):
        ref[0] = val

    put(r_ref, r)
    put(k_ref, k)
    put(v_ref, v)
    put(g_ref, g)
    put(an_ref, -kk)
    for d, (wo, kdo, bo) in enumerate(((w0o_ref, kd0_ref, b0_ref), (w1o_ref, kd1_ref, b1_ref))):
        logw = -jax.nn.softplus(-(w0_ref[d:d + 1, :] + up[:, d * D_HALF:(d + 1) * D_HALF])) - 0.5
        a = jax.nn.sigmoid(a0_ref[d:d + 1, :] + up[:, (2 + d) * D_HALF:(3 + d) * D_HALF])
        put(wo, jnp.exp(-jnp.exp(logw)))
        put(kdo, k * (1.0 + (a - 1.0) * ka_ref[...]))
        put(bo, kk * a)


def rwkv_pre(zr, mu, w_lora, w0, a0, k_k, k_a):
    bsz, length, width = zr.shape
    tl = RW_PRE_TL
    n_tiles = length // tl
    vec = lambda n: pl.BlockSpec((n, D_HALF), lambda b, i: (0, 0))
    out_blk = pl.BlockSpec((1, tl, D_HALF), lambda b, i: (b, i, 0))
    out_sds = jax.ShapeDtypeStruct((bsz, length, D_HALF), F32)
    return pl.pallas_call(
        functools.partial(_rwkv_pre_kernel, n_tiles=n_tiles),
        grid=(bsz, n_tiles),
        in_specs=[pl.BlockSpec((1, tl, width), lambda b, i: (b, i, 0)),
                  pl.BlockSpec((1, 8, width), lambda b, i: (b, jnp.maximum(i * (tl // 8) - 1, 0), 0)),
                  pl.BlockSpec((1, 8, width), lambda b, i: (b, jnp.minimum((i + 1) * (tl // 8), length // 8 - 1), 0)),
                  pl.BlockSpec((1, width), lambda b, i: (0, 0)),
                  pl.BlockSpec((RW_LORA, 5 * D_HALF), lambda b, i: (0, 0)),
                  vec(2), vec(2), vec(1), vec(1),
                  pl.BlockSpec((2 * LANES, LANES), lambda b, i: (0, 0))],
        out_specs=[out_blk] * 11,
        out_shape=[out_sds] * 11,
        compiler_params=_params(2),
        name="rwkv_pre",
    )(zr, zr, zr, mu, w_lora, w0, a0, k_k.reshape(1, -1), k_a.reshape(1, -1), _segment_ones())


def _rwkv_scan_kernel(r_ref, w_ref, k_ref, a_ref, b_ref, v_ref, s0_ref, seg_ref, y_ref, fin_ref,
                      st_ref, lhs_ref, *, reverse, n_chunks):
    c = pl.program_id(1)

    @pl.when(c == 0)
    def _():
        st_ref[...] = s0_ref[...]

    y_ref[...] = jnp.zeros_like(y_ref)
    lane = lax.broadcasted_iota(jnp.int32, (RW_HD, LANES), 1)
    lane_in_head = lane % RW_HD
    diag = jnp.where(lane_in_head == lax.broadcasted_iota(jnp.int32, (RW_HD, LANES), 0), 1.0, 0.0)

    nt = RW_PAIRS

    def row(ref, cn, t, p):
        return ref[cn, t, p:p + 1, :]

    def head_sums(cn, lo, hi):
        return _dot(lhs_ref[cn, lo:hi].reshape((hi - lo) * RW_HD, 2 * LANES), seg_ref[...])

    def tile(sums, q):
        return sums[q * RW_HD:(q + 1) * RW_HD]

    def step(i, carry):
        t = (RW_TL - 1 - i) if reverse else i
        hit = lane_in_head == t % RW_YBLK
        chains = range(RW_NB)
        for cn in chains:
            for p in range(nt):
                lhs_ref[cn, p] = _hi_lo(st_ref[cn, p] * row(a_ref, cn, t, p))
                lhs_ref[cn, nt + p] = _hi_lo(diag * row(v_ref, cn, t, p))
        sums = [head_sums(cn, 0, 2 * nt) for cn in chains]
        for cn in chains:
            for p in range(nt):
                s = (st_ref[cn, p] * row(w_ref, cn, t, p) + tile(sums[cn], p) * row(b_ref, cn, t, p)
                     + tile(sums[cn], nt + p) * row(k_ref, cn, t, p))
                st_ref[cn, p] = s
                lhs_ref[cn, 2 * nt + p] = _hi_lo(s * row(r_ref, cn, t, p))
        ys = [head_sums(cn, 2 * nt, 3 * nt) for cn in chains]
        for cn in chains:
            for p in range(nt):
                y_ref[cn, t // RW_YBLK, p] = jnp.where(hit, tile(ys[cn], p), y_ref[cn, t // RW_YBLK, p])
        return carry

    lax.fori_loop(0, RW_TL, step, 0)

    @pl.when(c == n_chunks - 1)
    def _():
        fin_ref[...] = st_ref[...]


def rwkv_scan(r, w, k, a, b, v, s0, reverse):
    bsz, length = r.shape[:2]
    n_chunks = length // RW_TL

    def chunk(c):
        return (n_chunks - 1 - c) if reverse else c

    row_blk = pl.BlockSpec((RW_NB, RW_TL, RW_PAIRS, LANES), lambda bi, c: (bi, chunk(c), 0, 0))
    st_blk = pl.BlockSpec((RW_NB, RW_PAIRS, RW_HD, LANES), lambda bi, c: (bi, 0, 0, 0))
    y_blk = pl.BlockSpec((RW_NB, RW_TL // RW_YBLK, RW_PAIRS, RW_HD, LANES), lambda bi, c: (bi, chunk(c), 0, 0, 0))
    return pl.pallas_call(
        functools.partial(_rwkv_scan_kernel, reverse=reverse, n_chunks=n_chunks),
        grid=(bsz // RW_NB, n_chunks),
        in_specs=[row_blk] * 6 + [st_blk, pl.BlockSpec((2 * LANES, LANES), lambda bi, c: (0, 0))],
        out_specs=[y_blk, st_blk],
        out_shape=[jax.ShapeDtypeStruct((bsz, length // RW_YBLK, RW_PAIRS, RW_HD, LANES), F32),
                   jax.ShapeDtypeStruct((bsz, RW_PAIRS, RW_HD, LANES), F32)],
        scratch_shapes=[pltpu.VMEM((RW_NB, RW_PAIRS, RW_HD, LANES), F32),
                        pltpu.VMEM((RW_NB, 3 * RW_PAIRS, RW_HD, 2 * LANES), BF16)],
        compiler_params=_params(2),
        name="rwkv_scan_rev" if reverse else "rwkv_scan_fwd",
    )(r, w, k, a, b, v, s0, _segment_ones())


def _rwkv_post_kernel(yf_ref, yb_ref, r_ref, k_ref, v_ref, g_ref, rk_ref, lg_ref, lb_ref, seg_ref, o_ref):
    y = yf_ref[0] + yb_ref[0]
    mean = _head_sums(y, seg_ref) * (1.0 / RW_HD)
    yc = y - mean
    var = _head_sums(yc * yc, seg_ref) * (1.0 / RW_HD)
    y = yc * lax.rsqrt(var + GN_EPS) * lg_ref[...] + lb_ref[...]
    bonus = _head_sums(r_ref[0] * k_ref[0] * rk_ref[...], seg_ref) * v_ref[0]
    o_ref[0] = ((y + bonus) * g_ref[0]).astype(BF16)


def rwkv_post(yf, yb, r, k, v, g, r_k, ln_g, ln_b):
    bsz, length, n = r.shape
    tl = RW_TL
    blk = pl.BlockSpec((1, tl, n), lambda b, i: (b, i, 0))
    vec = pl.BlockSpec((1, n), lambda b, i: (0, 0))
    return pl.pallas_call(
        _rwkv_post_kernel,
        grid=(bsz, length // tl),
        in_specs=[blk] * 6 + [vec] * 3 + [pl.BlockSpec((2 * LANES, LANES), lambda b, i: (0, 0))],
        out_specs=blk,
        out_shape=jax.ShapeDtypeStruct((bsz, length, n), BF16),
        compiler_params=_params(2),
        name="rwkv_post",
    )(yf, yb, r, k, v, g, r_k.reshape(1, n), ln_g.reshape(1, n), ln_b.reshape(1, n), _segment_ones())


def rwkv_lora_weights(w2, a2, g2):
    w = jnp.zeros((RW_LORA, 5 * D_HALF), F32)
    for d in range(2):
        w = w.at[0:RW_W_RANK, d * D_HALF:(d + 1) * D_HALF].set(w2[d].astype(F32))
        w = w.at[RW_W_RANK:RW_W_RANK + RW_A_RANK, (2 + d) * D_HALF:(3 + d) * D_HALF].set(a2[d].astype(F32))
    w = w.at[RW_W_RANK + RW_A_RANK:, 4 * D_HALF:].set(g2.astype(F32))
    return w.astype(BF16)


def rwkv_mixer(zr, s0, rwp):
    mu, w0, w2, a0, a2, g2, k_k, k_a, r_k, ln_g, ln_b = rwp
    bsz, length = zr.shape[:2]
    mu_p = jnp.pad(mu.astype(F32), (0, RW_IN_PAD - RW_IN)).reshape(1, RW_IN_PAD)
    r, k, v, g, an, wd0, kd0, b0, wd1, kd1, b1 = rwkv_pre(zr, mu_p, rwkv_lora_weights(w2, a2, g2), w0, a0, k_k, k_a)
    if s0 is None:
        s0 = jnp.zeros((bsz, 2, RW_HEADS, RW_HD, RW_HD), F32)

    def rows(t):
        return t.reshape(bsz, length, RW_PAIRS, LANES)

    def pack_state(s):
        return s.astype(F32).reshape(bsz, RW_PAIRS, 2, RW_HD, RW_HD).transpose(0, 1, 3, 2, 4).reshape(
            bsz, RW_PAIRS, RW_HD, LANES)

    def unpack_state(s):
        return s.reshape(bsz, RW_PAIRS, RW_HD, 2, RW_HD).transpose(0, 1, 3, 2, 4).reshape(bsz, RW_HEADS, RW_HD, RW_HD)

    def unpack_y(y):
        y = y.reshape(bsz, length // RW_YBLK, RW_PAIRS, RW_HD, 2, RW_YBLK)
        return y.transpose(0, 1, 5, 2, 4, 3).reshape(bsz, length, D_HALF)

    yf, fin_f = rwkv_scan(rows(r), rows(wd0), rows(kd0), rows(an), rows(b0), rows(v), pack_state(s0[:, 0]), reverse=False)
    yb, fin_b = rwkv_scan(rows(r), rows(wd1), rows(kd1), rows(an), rows(b1), rows(v), pack_state(s0[:, 1]), reverse=True)
    y = rwkv_post(unpack_y(yf), unpack_y(yb), r, k, v, g, r_k, ln_g, ln_b)
    return y, jnp.stack([unpack_state(fin_f), unpack_state(fin_b)], axis=1)


ROUTE_TILE = 512


def _rank_before(vals, n):
    idx = lax.broadcasted_iota(jnp.int32, vals.shape, 0)
    cnt = jnp.zeros(vals.shape, F32)
    for e in range(n):
        row = vals[e:e + 1]
        cnt = cnt + jnp.where((row > vals) | ((row == vals) & (idx > e)), 1.0, 0.0)
    return cnt


def _route_kernel(lg_ref, bias_ref, tril_ref, triu_ref, te_ref, gt_ref, rk_ref, cnt_ref, carry_ref):
    @pl.when(pl.program_id(0) == 0)
    def _():
        carry_ref[...] = jnp.zeros_like(carry_ref)

    t = lg_ref.shape[1]
    per = N_EXPERTS // N_ROUTE_GROUPS
    scores = jax.nn.sigmoid(lg_ref[...])
    sel = scores + bias_ref[...]
    sel3 = sel.reshape(N_ROUTE_GROUPS, per, t)
    m1 = jnp.max(sel3, axis=1, keepdims=True)
    within = lax.broadcasted_iota(jnp.int32, sel3.shape, 1)
    first = jnp.min(jnp.where(sel3 == m1, within, per), axis=1, keepdims=True)
    m2 = jnp.max(jnp.where(within == first, -jnp.inf, sel3), axis=1, keepdims=True)
    group_score = (m1 + m2).reshape(N_ROUTE_GROUPS, t)
    group_ok = _rank_before(group_score, N_ROUTE_GROUPS) < TOPK_ROUTE_GROUPS
    expert_ok = jnp.broadcast_to(group_ok.reshape(N_ROUTE_GROUPS, 1, t), sel3.shape).reshape(N_EXPERTS, t)
    masked = jnp.where(expert_ok, sel, NEG_INF)
    chosen = _rank_before(masked, N_EXPERTS) < TOP_K
    gates = jnp.where(chosen, scores, 0.0)
    gates = gates / jnp.sum(gates, axis=0, keepdims=True) * ROUTED_SCALE
    onehot = jnp.where(chosen, 1.0, 0.0).astype(BF16)
    order = _dot(tril_ref[...], onehot)
    before = carry_ref[:, 0:1] + _dot(onehot, triu_ref[...])
    carry_ref[...] = carry_ref[...] + jnp.sum(jnp.where(chosen, 1.0, 0.0), axis=1, keepdims=True)
    eidx = lax.broadcasted_iota(jnp.int32, chosen.shape, 0).astype(F32)
    for j in range(TOP_K):
        pick = chosen & (order == float(j + 1))
        te_ref[j:j + 1, :] = jnp.sum(jnp.where(pick, eidx, 0.0), axis=0, keepdims=True).astype(jnp.int32)
        gt_ref[j:j + 1, :] = jnp.sum(jnp.where(pick, gates, 0.0), axis=0, keepdims=True)
        rk_ref[j:j + 1, :] = jnp.sum(jnp.where(pick, before, 0.0), axis=0, keepdims=True).astype(jnp.int32)
    cnt_ref[...] = carry_ref[...]


def route(logits_t, b_router):
    e, n = logits_t.shape
    t = ROUTE_TILE
    tril = jnp.asarray(np.tril(np.ones((e, e))), BF16)
    triu = jnp.asarray(np.triu(np.ones((t, t)), 1), BF16)
    out_blk = pl.BlockSpec((TOP_K, t), lambda i: (0, i))
    te, gt, rk, cnt = pl.pallas_call(
        _route_kernel,
        grid=(n // t,),
        in_specs=[pl.BlockSpec((e, t), lambda i: (0, i)),
                  pl.BlockSpec((e, 1), lambda i: (0, 0)),
                  pl.BlockSpec((e, e), lambda i: (0, 0)),
                  pl.BlockSpec((t, t), lambda i: (0, 0))],
        out_specs=[out_blk, out_blk, out_blk, pl.BlockSpec((e, 128), lambda i: (0, 0))],
        out_shape=[jax.ShapeDtypeStruct((TOP_K, n), jnp.int32), jax.ShapeDtypeStruct((TOP_K, n), F32),
                   jax.ShapeDtypeStruct((TOP_K, n), jnp.int32), jax.ShapeDtypeStruct((e, 128), F32)],
        scratch_shapes=[pltpu.VMEM((e, 128), F32)],
        compiler_params=_params(1),
        name="route",
    )(logits_t, b_router.astype(F32).reshape(e, 1), tril, triu)
    return te, gt, rk, cnt[:, 0]


def _expert_kernel(be_ref, nu_ref, x_ref, wg_ref, wu_ref, wd_ref, o_ref, wgb_ref, wub_ref, wdb_ref):
    i = pl.program_id(0)
    used = i < nu_ref[0]
    fresh = (i == 0) | (be_ref[i] != be_ref[jnp.maximum(i - 1, 0)])

    @pl.when(used & fresh)
    def _():
        wgb_ref[...] = wg_ref[0, 0].astype(BF16)
        wub_ref[...] = wu_ref[0, 0].astype(BF16)
        wdb_ref[...] = wd_ref[0, 0].astype(BF16)

    @pl.when(used)
    def _():
        x = x_ref[...]
        gate = _dot(x, wgb_ref[...])
        h = (gate * jax.nn.sigmoid(gate) * _dot(x, wub_ref[...])).astype(BF16)
        o_ref[...] = _dot(h, wdb_ref[...]).astype(o_ref.dtype)

    @pl.when(jnp.logical_not(used))
    def _():
        o_ref[...] = jnp.zeros_like(o_ref)


def expert_ffn(xs, block_expert, n_used, w_gate, w_up, w_down, layer):
    s, d = xs.shape
    de = w_gate.shape[3]
    nb = s // MOE_ROWS
    grid_spec = pltpu.PrefetchScalarGridSpec(
        num_scalar_prefetch=2,
        grid=(nb,),
        in_specs=[pl.BlockSpec((MOE_ROWS, d), lambda i, be, nu: (i, 0)),
                  pl.BlockSpec((1, 1, d, de), lambda i, be, nu: (layer, be[i], 0, 0)),
                  pl.BlockSpec((1, 1, d, de), lambda i, be, nu: (layer, be[i], 0, 0)),
                  pl.BlockSpec((1, 1, de, d), lambda i, be, nu: (layer, be[i], 0, 0))],
        out_specs=pl.BlockSpec((MOE_ROWS, d), lambda i, be, nu: (i, 0)),
        scratch_shapes=[pltpu.VMEM((d, de), BF16), pltpu.VMEM((d, de), BF16), pltpu.VMEM((de, d), BF16)],
    )
    return pl.pallas_call(
        _expert_kernel,
        grid_spec=grid_spec,
        out_shape=jax.ShapeDtypeStruct((s, d), BF16),
        compiler_params=_params(1),
        name="expert_ffn",
    )(block_expert, n_used, xs, w_gate, w_up, w_down)


def _combine_kernel(x_ref, yg_ref, gt_ref, sh_ref, mod_ref, o_ref, *, gate_idx):
    routed = jnp.zeros(x_ref.shape, F32)
    for j in range(TOP_K):
        routed = routed + gt_ref[:, j:j + 1] * yg_ref[j].astype(F32)
    g = mod_ref[0][gate_idx:gate_idx + 1]
    o_ref[...] = x_ref[...] + g * (routed + sh_ref[...].astype(F32))


def moe_combine(x, yg, gates, shared, mods, gate_idx, n_ctx, tm=128):
    n, d = x.shape
    row = functools.partial(_cond_row, n_ctx_tiles=n_ctx // tm, tiles_per_sample=DEC_SEQ // tm)
    return pl.pallas_call(
        functools.partial(_combine_kernel, gate_idx=gate_idx),
        grid=(n // tm,),
        in_specs=[pl.BlockSpec((tm, d), lambda i: (i, 0)),
                  pl.BlockSpec((TOP_K, tm, d), lambda i: (0, i, 0)),
                  pl.BlockSpec((tm, TOP_K), lambda i: (i, 0)),
                  pl.BlockSpec((tm, d), lambda i: (i, 0)),
                  pl.BlockSpec((1, 6, d), lambda i: (row(i), 0, 0))],
        out_specs=pl.BlockSpec((tm, d), lambda i: (i, 0)),
        out_shape=jax.ShapeDtypeStruct((n, d), F32),
        compiler_params=_params(1),
        name="moe_combine",
    )(x, yg, gates, shared, mods)


def moe_ffn(x, h, logits_t, mods, gate_idx, n_ctx, layer, b_router, w_gate, w_up, w_down, ws_gate, ws_up, ws_down):
    n, d = h.shape
    top_e, gates, rank, counts = route(logits_t, b_router)
    counts = counts.astype(jnp.int32)
    padded = (counts + MOE_ROWS - 1) // MOE_ROWS * MOE_ROWS
    pad_end = jnp.cumsum(padded)
    pad_start = pad_end - padded
    n_slots = n * TOP_K + N_EXPERTS * MOE_ROWS
    nb = n_slots // MOE_ROWS
    expert_ids = jnp.arange(N_EXPERTS, dtype=jnp.int32)
    start_of = jnp.sum(jnp.where(top_e[..., None] == expert_ids, pad_start, 0), axis=-1)
    pos = start_of + rank
    tok = jnp.broadcast_to(jnp.arange(n, dtype=jnp.int32)[None, :], pos.shape)
    slot_tok = jnp.zeros((n_slots,), jnp.int32).at[pos.reshape(-1)].set(tok.reshape(-1), unique_indices=True)
    block_start = jnp.arange(nb, dtype=jnp.int32) * MOE_ROWS
    block_expert = jnp.minimum(jnp.sum(pad_end[None, :] <= block_start[:, None], axis=1), N_EXPERTS - 1).astype(jnp.int32)
    n_used = (pad_end[-1:] // MOE_ROWS).astype(jnp.int32)
    xs = h[slot_tok]
    yb = expert_ffn(xs, block_expert, n_used, w_gate, w_up, w_down, layer)
    shared = expert_ffn(h, jnp.zeros((n // MOE_ROWS,), jnp.int32), jnp.full((1,), n // MOE_ROWS, jnp.int32),
                        ws_gate[:, None], ws_up[:, None], ws_down[:, None], layer)
    yg = yb[pos]
    return moe_combine(x, yg, gates.T, shared, mods, gate_idx, n_ctx)


def _layer_ab(h, n_ctx, bsz_ctx, bsz_lat, p):
    z = project(h, p['w_in'], tn=512)
    s5w = s5_weights(*p['s5_disc'])
    outs = []
    extras = {}
    for name, lo, bsz, length in (('ctx', 0, bsz_ctx, SEQ), ('lat', n_ctx, bsz_lat, DEC_SEQ)):
        zg = z[lo:lo + bsz * length].reshape(bsz, length, 4 * D_HALF)
        u = zg[..., :D_HALF]
        q, k, v = (_heads_first(zg[..., (1 + j) * D_HALF:(2 + j) * D_HALF], bsz, length, NA_HEADS) for j in range(3))
        if name == 'ctx':
            y_a, s5_fin = s5_mixer(u, None, p['s5_glu'], s5w)
            y_b, k_n = context_attention(q, k, v, p['q_g'], p['k_g'])
            extras = {'k': k_n, 'v': v, 's5': s5_fin}
        else:
            y_a, _ = s5_mixer(u, p['s5_state'], p['s5_glu'], s5w)
            y_b = neighbourhood_attention(q, k, v, p['ck'], p['cv'], p['rpb'], p['q_g'], p['k_g'])
        outs.append(jnp.concatenate([y_a, y_b], axis=-1).reshape(bsz * length, 2 * D_HALF))
    return jnp.concatenate(outs, axis=0), extras


def _layer_cd(h, n_ctx, bsz_ctx, bsz_lat, p):
    w_in = p['w_in']
    zh_all = project(h, w_in[:, :3 * D_HALF], tn=512)
    w_rw = jnp.pad(w_in[:, 3 * D_HALF:], ((0, 0), (0, RW_IN_PAD - RW_IN)))
    zr_all = project(h, w_rw, tn=512)
    outs = []
    extras = {}
    for name, lo, bsz, length in (('ctx', 0, bsz_ctx, SEQ), ('lat', n_ctx, bsz_lat, DEC_SEQ)):
        zh = zh_all[lo:lo + bsz * length].reshape(bsz, length, 3 * D_HALF)
        zr = zr_all[lo:lo + bsz * length].reshape(bsz, length, RW_IN_PAD)
        filt = hyena_filter(length, *p['hy_filter'])
        y_c = hyena_mixer(zh, filt, p['hy_short_w'], p['hy_short_b'], p['hy_bias'])
        y_d, rw_fin = rwkv_mixer(zr, None if name == 'ctx' else p['rw_state'], p['rwp'])
        if name == 'ctx':
            extras = {'rw': rw_fin}
        outs.append(jnp.concatenate([y_c, y_d], axis=-1).reshape(bsz * length, 2 * D_HALF))
    return jnp.concatenate(outs, axis=0), extras


def kernel(x_prompt, x_sample, cache_na_k, cache_na_v, state_s5, state_rwkv, c, c_ctx, ada_w, ada_b, norm_mix, norm_ffn, ab_w_in, ab_w_out, s5_lam_re, s5_lam_im, s5_log_step, s5_b_re, s5_b_im, s5_c_re, s5_c_im, s5_d, s5_w_glu, s5_b_glu, na_q_norm, na_k_norm, na_rpb, cd_w_in, cd_w_out, hy_short_w, hy_short_b, hy_w1, hy_b1, hy_freq, hy_w2, hy_b2, hy_w3, hy_decay, hy_bias, rw_mu, rw_w0, rw_w2, rw_a0, rw_a2, rw_g2, rw_k_k, rw_k_a, rw_r_k, rw_ln_g, rw_ln_b, moe_router, moe_router_bias, moe_w_gate, moe_w_up, moe_w_down, moe_ws_gate, moe_ws_up, moe_ws_down):
    bsz_ctx, seq, d = x_prompt.shape
    bsz_lat, dec_seq, _ = x_sample.shape
    assert (seq, dec_seq, d) == (SEQ, DEC_SEQ, D_MODEL) and bsz_lat + 1 <= 8
    depth = ada_w.shape[0]
    n_ctx = bsz_ctx * seq
    x = jnp.concatenate([x_prompt.reshape(n_ctx, d), x_sample.reshape(bsz_lat * dec_seq, d)], axis=0).astype(F32)

    cond = jnp.concatenate([c_ctx[None, :], c, jnp.zeros((8 - 1 - bsz_lat, d), c.dtype)], axis=0).astype(F32)
    mods_all = ada_table(cond, ada_w, ada_b).reshape(depth, 8, 6, d)

    new_k, new_v, new_s5, new_rw = [], [], [], []
    for l in range(depth):
        mods = mods_all[l]
        i = l // 2
        h = modulate(x, norm_mix[l], mods, 0, 1, n_ctx)
        if l % 2 == 0:
            p = {'w_in': ab_w_in[i], 'q_g': na_q_norm[i], 'k_g': na_k_norm[i], 'rpb': na_rpb[i],
                 'ck': cache_na_k[:, i], 'cv': cache_na_v[:, i], 's5_state': state_s5[:, i],
                 's5_disc': (s5_lam_re[i], s5_lam_im[i], s5_log_step[i], s5_b_re[i], s5_b_im[i], s5_c_re[i], s5_c_im[i]),
                 's5_glu': (s5_d[i], s5_w_glu[i], s5_b_glu[i])}
            y, ex = _layer_ab(h, n_ctx, bsz_ctx, bsz_lat, p)
            new_k.append(ex['k'])
            new_v.append(ex['v'])
            new_s5.append(ex['s5'])
            w_out = ab_w_out[i]
        else:
            p = {'w_in': cd_w_in[i], 'hy_filter': (hy_w1[i], hy_b1[i], hy_freq[i], hy_w2[i], hy_b2[i], hy_w3[i], hy_decay[i]),
                 'hy_short_w': hy_short_w[i], 'hy_short_b': hy_short_b[i], 'hy_bias': hy_bias[i],
                 'rw_state': state_rwkv[:, i],
                 'rwp': (rw_mu[i], rw_w0[i], rw_w2[i], rw_a0[i], rw_a2[i], rw_g2[i], rw_k_k[i], rw_k_a[i], rw_r_k[i],
                         rw_ln_g[i], rw_ln_b[i])}
            y, ex = _layer_cd(h, n_ctx, bsz_ctx, bsz_lat, p)
            new_rw.append(ex['rw'])
            w_out = cd_w_out[i]
        x = project_residual(y, w_out, x, mods, 2, n_ctx)
        h, logits_t = modulate(x, norm_ffn[l], mods, 3, 4, n_ctx, w_router_t=moe_router[l].T)
        x = moe_ffn(x, h, logits_t, mods, 5, n_ctx, l, moe_router_bias[l], moe_w_gate, moe_w_up, moe_w_down,
                    moe_ws_gate, moe_ws_up, moe_ws_down)

    y_prompt = x[:n_ctx].reshape(bsz_ctx, seq, d)
    y_sample = x[n_ctx:].reshape(bsz_lat, dec_seq, d)
    return (y_prompt, y_sample, jnp.stack(new_k, axis=1), jnp.stack(new_v, axis=1),
            jnp.stack(new_s5, axis=1), jnp.stack(new_rw, axis=1))
```

```python
import functools
import math

import numpy as np
import jax
import jax.numpy as jnp
from jax import lax
from jax.experimental import pallas as pl
from jax.experimental.pallas import tpu as pltpu

F32 = jnp.float32
BF16 = jnp.bfloat16

D_MODEL = 2048
D_HALF = 1024
SEQ = 256
DEC_SEQ = 2048
GRID_W = 64
S5_GROUP = 16
S5_GROUPS = 64
S5_STATE = 64
S5_LANES = S5_GROUPS * S5_STATE
NA_HEADS = 16
NA_HD = 64
NA_KH = 8
NA_KW = 16
HY_BANDS = 16
RW_HEADS = 16
RW_HD = 64
RW_W_RANK = 64
RW_A_RANK = 64
RW_G_RANK = 128
RW_LORA = 256
RW_IN = 3 * D_HALF + RW_LORA
RW_IN_PAD = 3584
N_EXPERTS = 64
TOP_K = 8
N_ROUTE_GROUPS = 8
TOPK_ROUTE_GROUPS = 4
D_EXPERT = 512
ROUTED_SCALE = 2.5
EPS = 1e-6
GN_EPS = 64e-5
NEG_INF = -1e30

TOK_TILE = 256
MOE_ROWS = 256
VMEM_LIMIT = 56 * 1024 * 1024


def _params(n_axes, vmem=VMEM_LIMIT):
    return pltpu.CompilerParams(dimension_semantics=("arbitrary",) * n_axes, vmem_limit_bytes=vmem)


def _dot(a, b):
    return jnp.dot(a, b, preferred_element_type=F32)


def _dot_nt(a, b):
    return lax.dot_general(a, b, (((1,), (1,)), ((), ())), preferred_element_type=F32)


def _split_bf16(x):
    hi = x.astype(BF16)
    lo = (x - hi.astype(F32)).astype(BF16)
    return hi, lo


def _cond_row(i, n_ctx_tiles, tiles_per_sample):
    return jnp.where(i < n_ctx_tiles, 0, 1 + (i - n_ctx_tiles) // tiles_per_sample)


def _ada_kernel(c_ref, w_ref, b_ref, o_ref):
    c = c_ref[...]
    s = (c * jax.nn.sigmoid(c)).astype(BF16)
    o_ref[0] = _dot(s, w_ref[0].astype(BF16)) + b_ref[0]


def ada_table(cond8, ada_w, ada_b):
    depth, d, n = ada_w.shape
    tn = 1024
    return pl.pallas_call(
        _ada_kernel,
        grid=(depth, n // tn),
        in_specs=[pl.BlockSpec((8, d), lambda l, j: (0, 0)),
                  pl.BlockSpec((1, d, tn), lambda l, j: (l, 0, j)),
                  pl.BlockSpec((1, 1, tn), lambda l, j: (l, 0, j))],
        out_specs=pl.BlockSpec((1, 8, tn), lambda l, j: (l, 0, j)),
        out_shape=jax.ShapeDtypeStruct((depth, 8, n), F32),
        compiler_params=_params(2),
        name="ada_table",
    )(cond8, ada_w, ada_b.reshape(depth, 1, n))


def _modulate_kernel(x_ref, g_ref, mod_ref, h_ref, *, shift_idx, scale_idx):
    x = x_ref[...]
    y = x * lax.rsqrt(jnp.mean(x * x, axis=-1, keepdims=True) + EPS)
    m = mod_ref[0]
    h = y * g_ref[...] * (1.0 + m[scale_idx:scale_idx + 1]) + m[shift_idx:shift_idx + 1]
    h_ref[...] = h.astype(BF16)


def _modulate_router_kernel(x_ref, g_ref, mod_ref, wr_ref, h_ref, lg_ref, *, shift_idx, scale_idx):
    x = x_ref[...]
    y = x * lax.rsqrt(jnp.mean(x * x, axis=-1, keepdims=True) + EPS)
    m = mod_ref[0]
    h = y * g_ref[...] * (1.0 + m[scale_idx:scale_idx + 1]) + m[shift_idx:shift_idx + 1]
    h_hi, h_lo = _split_bf16(h)
    h_ref[...] = h_hi
    w_hi, w_lo = _split_bf16(wr_ref[...])
    lg_ref[...] = _dot_nt(w_hi, h_hi) + _dot_nt(w_hi, h_lo) + _dot_nt(w_lo, h_hi)


def modulate(x, gain, mods, shift_idx, scale_idx, n_ctx, w_router_t=None):
    n, d = x.shape
    tm = TOK_TILE
    row = functools.partial(_cond_row, n_ctx_tiles=n_ctx // tm, tiles_per_sample=DEC_SEQ // tm)
    in_specs = [pl.BlockSpec((tm, d), lambda i: (i, 0)),
                pl.BlockSpec((1, d), lambda i: (0, 0)),
                pl.BlockSpec((1, 6, d), lambda i: (row(i), 0, 0))]
    if w_router_t is None:
        return pl.pallas_call(
            functools.partial(_modulate_kernel, shift_idx=shift_idx, scale_idx=scale_idx),
            grid=(n // tm,),
            in_specs=in_specs,
            out_specs=pl.BlockSpec((tm, d), lambda i: (i, 0)),
            out_shape=jax.ShapeDtypeStruct((n, d), BF16),
            compiler_params=_params(1),
            name="modulate",
        )(x, gain.reshape(1, d), mods)
    e = w_router_t.shape[0]
    return pl.pallas_call(
        functools.partial(_modulate_router_kernel, shift_idx=shift_idx, scale_idx=scale_idx),
        grid=(n // tm,),
        in_specs=in_specs + [pl.BlockSpec((e, d), lambda i: (0, 0))],
        out_specs=[pl.BlockSpec((tm, d), lambda i: (i, 0)),
                   pl.BlockSpec((e, tm), lambda i: (0, i))],
        out_shape=[jax.ShapeDtypeStruct((n, d), BF16), jax.ShapeDtypeStruct((e, n), F32)],
        compiler_params=_params(1),
        name="modulate_router",
    )(x, gain.reshape(1, d), mods, w_router_t)


def _proj_kernel(x_ref, w_ref, o_ref, wbf_ref):
    @pl.when(pl.program_id(1) == 0)
    def _():
        wbf_ref[...] = w_ref[...].astype(BF16)

    o_ref[...] = _dot(x_ref[...], wbf_ref[...]).astype(o_ref.dtype)


def project(x, w, tn, tm=512, out_dtype=F32):
    m, k = x.shape
    n = w.shape[1]
    return pl.pallas_call(
        _proj_kernel,
        grid=(n // tn, m // tm),
        in_specs=[pl.BlockSpec((tm, k), lambda j, i: (i, 0)),
                  pl.BlockSpec((k, tn), lambda j, i: (0, j))],
        out_specs=pl.BlockSpec((tm, tn), lambda j, i: (i, j)),
        out_shape=jax.ShapeDtypeStruct((m, n), out_dtype),
        scratch_shapes=[pltpu.VMEM((k, tn), BF16)],
        compiler_params=_params(2),
        name="project",
    )(x, w)


def _proj_residual_kernel(y_ref, w_ref, x_ref, mod_ref, o_ref, wbf_ref, *, gate_idx):
    @pl.when(pl.program_id(1) == 0)
    def _():
        wbf_ref[...] = w_ref[...].astype(BF16)

    g = mod_ref[0][gate_idx:gate_idx + 1]
    o_ref[...] = x_ref[...] + g * _dot(y_ref[...], wbf_ref[...])


def project_residual(y, w, x, mods, gate_idx, n_ctx, tn=512, tm=512):
    m, k = y.shape
    n = w.shape[1]
    row = functools.partial(_cond_row, n_ctx_tiles=n_ctx // tm, tiles_per_sample=DEC_SEQ // tm)
    return pl.pallas_call(
        functools.partial(_proj_residual_kernel, gate_idx=gate_idx),
        grid=(n // tn, m // tm),
        in_specs=[pl.BlockSpec((tm, k), lambda j, i: (i, 0)),
                  pl.BlockSpec((k, tn), lambda j, i: (0, j)),
                  pl.BlockSpec((tm, tn), lambda j, i: (i, j)),
                  pl.BlockSpec((1, 6, tn), lambda j, i: (row(i), 0, j))],
        out_specs=pl.BlockSpec((tm, tn), lambda j, i: (i, j)),
        out_shape=jax.ShapeDtypeStruct((m, n), F32),
        scratch_shapes=[pltpu.VMEM((k, tn), BF16)],
        compiler_params=_params(2),
        name="project_residual",
    )(y, w, x, mods)


S5_CHUNK = 64
S5_ROWS = 8
S5_BLK = 8
S5_SCAN_LANES = 1024


def _s5_kernel(u_ref, wbr_ref, wbi_ref, wcr_ref, wci_ref, lam_ref, x0_ref, y_ref, fin_ref,
               bur_ref, bui_ref, st_ref, *, n_chunks):
    d = pl.program_id(0)
    c = pl.program_id(2)
    tc = S5_CHUNK
    cin = S5_BLK * S5_GROUP
    cst = S5_BLK * S5_STATE

    @pl.when(c == 0)
    def _():
        st_ref[...] = x0_ref[0]

    u = u_ref[...].reshape(tc * S5_ROWS, D_HALF).astype(BF16)
    for k in range(S5_GROUPS // S5_BLK):
        uk = u[:, k * cin:(k + 1) * cin]
        bur_ref[:, k * cst:(k + 1) * cst] = _dot(uk, wbr_ref[0, k])
        bui_ref[:, k * cst:(k + 1) * cst] = _dot(uk, wbi_ref[0, k])

    for j in range(S5_LANES // S5_SCAN_LANES):
        sl = slice(j * S5_SCAN_LANES, (j + 1) * S5_SCAN_LANES)
        lr = jnp.broadcast_to(lam_ref[0, 0:1, sl], (S5_ROWS, S5_SCAN_LANES))
        li = jnp.broadcast_to(lam_ref[0, 1:2, sl], (S5_ROWS, S5_SCAN_LANES))

        def step(i, carry, sl=sl, lr=lr, li=li):
            sr, si = carry
            t = jnp.where(d == 0, i, tc - 1 - i)
            row = pl.multiple_of(t * S5_ROWS, S5_ROWS)
            nr = lr * sr - li * si + bur_ref[pl.ds(row, S5_ROWS), sl]
            ni = lr * si + li * sr + bui_ref[pl.ds(row, S5_ROWS), sl]
            bur_ref[pl.ds(row, S5_ROWS), sl] = nr
            bui_ref[pl.ds(row, S5_ROWS), sl] = ni
            return nr, ni

        sr, si = lax.fori_loop(0, tc, step, (st_ref[0, :, sl], st_ref[1, :, sl]), unroll=4)
        st_ref[0, :, sl] = sr
        st_ref[1, :, sl] = si

    xr = bur_ref[...].astype(BF16)
    xi = bui_ref[...].astype(BF16)
    for k in range(S5_GROUPS // S5_BLK):
        yk = _dot(xr[:, k * cst:(k + 1) * cst], wcr_ref[0, k]) + _dot(xi[:, k * cst:(k + 1) * cst], wci_ref[0, k])
        y_ref[0, :, :, k * cin:(k + 1) * cin] = yk.reshape(tc, S5_ROWS, cin)

    @pl.when(c == n_chunks - 1)
    def _():
        fin_ref[0] = st_ref[...]


def s5_scan(u_t, x0, wbr, wbi, wcr, wci, lam):
    length, bsz, _ = u_t.shape
    n_chunks = length // S5_CHUNK
    nblk = S5_GROUPS // S5_BLK
    cin = S5_BLK * S5_GROUP
    cst = S5_BLK * S5_STATE

    def chunk(d, c):
        return jnp.where(d == 0, c, n_chunks - 1 - c)

    return pl.pallas_call(
        functools.partial(_s5_kernel, n_chunks=n_chunks),
        grid=(2, bsz // S5_ROWS, n_chunks),
        in_specs=[pl.BlockSpec((S5_CHUNK, S5_ROWS, D_HALF), lambda d, b, c: (chunk(d, c), b, 0)),
                  pl.BlockSpec((1, nblk, cin, cst), lambda d, b, c: (d, 0, 0, 0)),
                  pl.BlockSpec((1, nblk, cin, cst), lambda d, b, c: (d, 0, 0, 0)),
                  pl.BlockSpec((1, nblk, cst, cin), lambda d, b, c: (d, 0, 0, 0)),
                  pl.BlockSpec((1, nblk, cst, cin), lambda d, b, c: (d, 0, 0, 0)),
                  pl.BlockSpec((1, 2, S5_LANES), lambda d, b, c: (d, 0, 0)),
                  pl.BlockSpec((1, 2, S5_ROWS, S5_LANES), lambda d, b, c: (d, 0, b, 0))],
        out_specs=[pl.BlockSpec((1, S5_CHUNK, S5_ROWS, D_HALF), lambda d, b, c: (d, chunk(d, c), b, 0)),
                   pl.BlockSpec((1, 2, S5_ROWS, S5_LANES), lambda d, b, c: (d, 0, b, 0))],
        out_shape=[jax.ShapeDtypeStruct((2, length, bsz, D_HALF), F32),
                   jax.ShapeDtypeStruct((2, 2, bsz, S5_LANES), F32)],
        scratch_shapes=[pltpu.VMEM((S5_CHUNK * S5_ROWS, S5_LANES), F32),
                        pltpu.VMEM((S5_CHUNK * S5_ROWS, S5_LANES), F32),
                        pltpu.VMEM((2, S5_ROWS, S5_LANES), F32)],
        compiler_params=_params(3),
        name="s5_scan",
    )(u_t, wbr, wbi, wcr, wci, lam, x0)


def s5_weights(lam_re, lam_im, log_step, b_re, b_im, c_re, c_im):
    lam = lax.complex(lam_re.astype(F32), lam_im.astype(F32))
    lam_bar = jnp.exp(lam * jnp.exp(log_step.astype(F32))[..., None])
    b_bar = ((lam_bar - 1.0) / lam)[..., None] * lax.complex(b_re.astype(F32), b_im.astype(F32))
    nblk = S5_GROUPS // S5_BLK
    eye = jnp.eye(S5_BLK, dtype=F32)

    def embed_in(b):
        b = b.reshape(2, nblk, S5_BLK, S5_STATE, S5_GROUP)
        w = jnp.einsum('dkgpc,gh->dkgchp', b, eye)
        return w.reshape(2, nblk, S5_BLK * S5_GROUP, S5_BLK * S5_STATE).astype(BF16)

    def embed_out(cm):
        cm = cm.reshape(2, nblk, S5_BLK, S5_GROUP, S5_STATE)
        w = jnp.einsum('dkgcp,gh->dkgphc', cm, eye)
        return w.reshape(2, nblk, S5_BLK * S5_STATE, S5_BLK * S5_GROUP).astype(BF16)

    lam_rows = jnp.stack([lam_bar.real.reshape(2, S5_LANES), lam_bar.imag.reshape(2, S5_LANES)], axis=1)
    return (embed_in(b_bar.real), embed_in(b_bar.imag),
            embed_out(c_re.astype(F32)), embed_out(-c_im.astype(F32)), lam_rows)


def _glu_kernel(yf_ref, yb_ref, u_ref, d_ref, w_ref, b_ref, o_ref, wbf_ref):
    @pl.when(pl.program_id(0) == 0)
    def _():
        wbf_ref[...] = w_ref[...].astype(BF16)

    y = jax.nn.gelu(u_ref[...] * d_ref[...] + yf_ref[0] + yb_ref[0])
    o_ref[...] = (y * jax.nn.sigmoid(_dot(y.astype(BF16), wbf_ref[...]) + b_ref[...])).astype(BF16)


def s5_glu(y_dirs, u, d_skip, w_glu, b_glu, tm=512):
    m, n = u.shape
    return pl.pallas_call(
        _glu_kernel,
        grid=(m // tm,),
        in_specs=[pl.BlockSpec((1, tm, n), lambda i: (0, i, 0)),
                  pl.BlockSpec((1, tm, n), lambda i: (1, i, 0)),
                  pl.BlockSpec((tm, n), lambda i: (i, 0)),
                  pl.BlockSpec((1, n), lambda i: (0, 0)),
                  pl.BlockSpec((n, n), lambda i: (0, 0)),
                  pl.BlockSpec((1, n), lambda i: (0, 0))],
        out_specs=pl.BlockSpec((tm, n), lambda i: (i, 0)),
        out_shape=jax.ShapeDtypeStruct((m, n), BF16),
        scratch_shapes=[pltpu.VMEM((n, n), BF16)],
        compiler_params=_params(1),
        name="s5_glu",
    )(y_dirs, y_dirs, u, d_skip.reshape(1, n), w_glu, b_glu.reshape(1, n))


def s5_mixer(u, x0, s5p, weights):
    bsz, length, _ = u.shape
    bp = -(-bsz // S5_ROWS) * S5_ROWS
    wbr, wbi, wcr, wci, lam = weights
    d_skip, w_glu, b_glu = s5p
    u_t = jnp.swapaxes(u, 0, 1)
    if x0 is None:
        x0_t = jnp.zeros((2, 2, bp, S5_LANES), F32)
    else:
        x0_t = x0.astype(F32).reshape(bsz, 2, 2, S5_LANES).transpose(1, 2, 0, 3)
    if bp != bsz:
        u_t = jnp.pad(u_t, ((0, 0), (0, bp - bsz), (0, 0)))
        x0_t = jnp.pad(x0_t, ((0, 0), (0, 0), (0, bp - bsz), (0, 0)))
    y_dirs, fin = s5_scan(u_t, x0_t, wbr, wbi, wcr, wci, lam)
    y = s5_glu(y_dirs.reshape(2, length * bp, D_HALF), u_t.reshape(length * bp, D_HALF), d_skip, w_glu, b_glu)
    y = jnp.swapaxes(y.reshape(length, bp, D_HALF)[:, :bsz], 0, 1)
    fin = fin[:, :, :bsz].transpose(2, 0, 1, 3).reshape(bsz, 2, 2, S5_GROUPS, S5_STATE)
    return y, fin


def _head_rms(x, g):
    return x * lax.rsqrt(jnp.mean(x * x, axis=-1, keepdims=True) + EPS) * g


def _ctx_attn_kernel(q_ref, k_ref, v_ref, qg_ref, kg_ref, o_ref, kn_ref):
    scale = NA_HD ** -0.5
    for h in range(NA_HEADS):
        q = _head_rms(q_ref[0, h], qg_ref[...])
        k = _head_rms(k_ref[0, h], kg_ref[...])
        kn_ref[0, h] = k
        s = _dot_nt(q.astype(BF16), k.astype(BF16)) * scale
        p = jnp.exp(s - jnp.max(s, axis=-1, keepdims=True))
        l = jnp.sum(p, axis=-1, keepdims=True)
        o = _dot(p.astype(BF16), v_ref[0, h].astype(BF16)) / l
        o_ref[0, :, h * NA_HD:(h + 1) * NA_HD] = o.astype(BF16)


def context_attention(q, k, v, q_g, k_g):
    bsz, nh, t, hd = q.shape
    blk = pl.BlockSpec((1, nh, t, hd), lambda b: (b, 0, 0, 0))
    vec = pl.BlockSpec((1, hd), lambda b: (0, 0))
    return pl.pallas_call(
        _ctx_attn_kernel,
        grid=(bsz,),
        in_specs=[blk, blk, blk, vec, vec],
        out_specs=[pl.BlockSpec((1, t, nh * hd), lambda b: (b, 0, 0)), blk],
        out_shape=[jax.ShapeDtypeStruct((bsz, t, nh * hd), BF16), jax.ShapeDtypeStruct((bsz, nh, t, hd), F32)],
        compiler_params=_params(1),
        name="context_attention",
    )(q, k, v, q_g.reshape(1, hd), k_g.reshape(1, hd))


NA_PAIR = 2


def _na_kernel(q_ref, k_ref, v_ref, ck_ref, cv_ref, bias_ref, qg_ref, kg_ref, o_ref, kn_ref, vn_ref, *, rows):
    scale = NA_HD ** -0.5
    win = NA_KH * GRID_W
    for hh in range(NA_PAIR):
        kn_ref[...] = _head_rms(k_ref[0, hh], kg_ref[...]).astype(BF16)
        vn_ref[...] = v_ref[0, hh].astype(BF16)
        ck = ck_ref[0, hh].astype(BF16)
        cv = cv_ref[0, hh].astype(BF16)
        for r in range(rows):
            rs = min(max(r - NA_KH // 2, 0), rows - NA_KH)
            cls = rs - r + NA_KH - 1
            q = _head_rms(q_ref[0, hh, r * GRID_W:(r + 1) * GRID_W, :], qg_ref[...]).astype(BF16)
            kw = kn_ref[rs * GRID_W:rs * GRID_W + win, :]
            vw = vn_ref[rs * GRID_W:rs * GRID_W + win, :]
            s_w = _dot_nt(q, kw) * scale + bias_ref[hh, cls]
            s_c = _dot_nt(q, ck) * scale
            m = jnp.maximum(jnp.max(s_w, axis=-1, keepdims=True), jnp.max(s_c, axis=-1, keepdims=True))
            p_w = jnp.exp(s_w - m)
            p_c = jnp.exp(s_c - m)
            l = jnp.sum(p_w, axis=-1, keepdims=True) + jnp.sum(p_c, axis=-1, keepdims=True)
            o = (_dot(p_w.astype(BF16), vw) + _dot(p_c.astype(BF16), cv)) / l
            o_ref[0, r * GRID_W:(r + 1) * GRID_W, hh * NA_HD:(hh + 1) * NA_HD] = o.astype(BF16)


def na_bias_table(rpb, rows):
    qc = np.arange(GRID_W)
    kc = np.arange(GRID_W)
    c0 = np.clip(qc - NA_KW // 2, 0, GRID_W - NA_KW)
    ok = (kc[None, :] >= c0[:, None]) & (kc[None, :] < c0[:, None] + NA_KW)
    dc = np.clip(kc[None, :] - qc[:, None] + NA_KW - 1, 0, 2 * NA_KW - 2)
    dr = np.arange(NA_KH)[:, None] + np.arange(NA_KH)[None, :]
    b = rpb.astype(F32)[:, dr][:, :, :, dc]
    b = jnp.where(ok[None, None, None], b, NEG_INF)
    return b.transpose(0, 1, 3, 2, 4).reshape(rpb.shape[0], NA_KH, GRID_W, NA_KH * GRID_W)


def neighbourhood_attention(q, k, v, ck, cv, rpb, q_g, k_g):
    bsz, nh, length, hd = q.shape
    past = ck.shape[2]
    rows = length // GRID_W
    bias = na_bias_table(rpb, rows)
    blk = pl.BlockSpec((1, NA_PAIR, length, hd), lambda b, h: (b, h, 0, 0))
    cblk = pl.BlockSpec((1, NA_PAIR, past, hd), lambda b, h: (b, h, 0, 0))
    vec = pl.BlockSpec((1, hd), lambda b, h: (0, 0))
    return pl.pallas_call(
        functools.partial(_na_kernel, rows=rows),
        grid=(bsz, nh // NA_PAIR),
        in_specs=[blk, blk, blk, cblk, cblk,
                  pl.BlockSpec((NA_PAIR, NA_KH, GRID_W, NA_KH * GRID_W), lambda b, h: (h, 0, 0, 0)),
                  vec, vec],
        out_specs=pl.BlockSpec((1, length, NA_PAIR * hd), lambda b, h: (b, 0, h)),
        out_shape=jax.ShapeDtypeStruct((bsz, length, nh * hd), BF16),
        scratch_shapes=[pltpu.VMEM((length, hd), BF16), pltpu.VMEM((length, hd), BF16)],
        compiler_params=_params(2),
        name="neighbourhood_attention",
    )(q, k, v, ck, cv, bias, q_g.reshape(1, hd), k_g.reshape(1, hd))


def _heads_first(t, bsz, length, nh):
    return t.reshape(bsz, length, nh, -1).transpose(0, 2, 1, 3)


HY_EMB_PAD = 64


def _dot3(a, b):
    a_hi, a_lo = _split_bf16(a)
    b_hi, b_lo = _split_bf16(b)
    return _dot(a_hi, b_hi) + _dot(a_hi, b_lo) + _dot(a_lo, b_hi)


def _hyena_filter_kernel(z_ref, w1_ref, b1_ref, f_ref, w2_ref, b2_ref, w3_ref, dec_ref, o_ref):
    length = z_ref.shape[0]
    h = jnp.sin(f_ref[0:1, :] * (_dot3(z_ref[...], w1_ref[...]) + b1_ref[...]))
    h = jnp.sin(f_ref[1:2, :] * (_dot3(h, w2_ref[...]) + b2_ref[...]))
    filt = _dot3(h, w3_ref[...])
    t = lax.broadcasted_iota(jnp.int32, filt.shape, 0).astype(F32)
    offset = jnp.abs(t - float(length // 2)) / float(length)
    filt = filt * jnp.exp(-offset * jnp.abs(dec_ref[...]))
    o_ref[...] = filt / (jnp.sum(jnp.abs(filt), axis=0, keepdims=True) + EPS)


def _hyena_embedding(length):
    t = np.arange(length, dtype=np.float32) / np.float32(length)
    bands = np.linspace(1e-4, HY_BANDS - 1, HY_BANDS, dtype=np.float32)
    ang = (2 * math.pi * t[:, None] * bands[None, :]).astype(np.float32)
    z = np.concatenate([t[:, None], np.cos(ang), np.sin(ang)], axis=-1).astype(np.float32)
    return np.pad(z, ((0, 0), (0, HY_EMB_PAD - z.shape[1])))


def hyena_filter(length, w1, b1, freq, w2, b2, w3, decay):
    z = jnp.asarray(_hyena_embedding(length))
    w1p = jnp.pad(w1.astype(F32), ((0, HY_EMB_PAD - w1.shape[0]), (0, 0)))
    n = w3.shape[1]
    return pl.pallas_call(
        _hyena_filter_kernel,
        out_shape=jax.ShapeDtypeStruct((length, n), F32),
        compiler_params=_params(0),
        name="hyena_filter",
    )(z, w1p, b1.reshape(1, -1), freq, w2, b2.reshape(1, -1), w3, decay.reshape(1, n))


def _dft_tile(length):
    return min(length, 512)


def _dft_matrices(length):
    n = 2 * length
    tf = _dft_tile(length)
    f = np.arange(length)
    s = np.arange(length)
    ang = 2 * np.pi * ((f[:, None] * s[None, :]) % n) / n
    fc = np.cos(ang)
    fs = -np.sin(ang)
    fs[0] = np.where(s % 2 == 0, 1.0, -1.0)
    nt = np.arange(length) + length // 2
    ang_i = 2 * np.pi * ((nt[:, None] * f[None, :]) % n) / n
    wf = np.where(f == 0, 1.0, 2.0)[None, :]
    gc = wf * np.cos(ang_i) / n
    gs = -wf * np.sin(ang_i) / n
    gs[:, 0] = np.where(nt % 2 == 0, 1.0, -1.0) / n
    n_tiles = length // tf
    fwd = np.concatenate([fc.reshape(n_tiles, tf, length), fs.reshape(n_tiles, tf, length)], axis=1)
    inv = np.concatenate([gc.reshape(length, n_tiles, tf), gs.reshape(length, n_tiles, tf)], axis=2)
    return (jnp.asarray(fwd.reshape(n_tiles * 2 * tf, length), BF16),
            jnp.asarray(inv.transpose(1, 0, 2), BF16))


def _shift_rows(x, up):
    n = x.shape[0]
    row = lax.broadcasted_iota(jnp.int32, x.shape, 0)
    if up:
        return jnp.where(row == n - 1, 0.0, pltpu.roll(x, n - 1, 0))
    return jnp.where(row == 0, 0.0, pltpu.roll(x, 1, 0))


def _conv3(x, w, b):
    return _shift_rows(x, False) * w[0:1] + x * w[1:2] + _shift_rows(x, True) * w[2:3] + b


def _hyena_conv_kernel(x0_ref, x1_ref, v_ref, w0_ref, w1_ref, wv_ref, b0_ref, b1_ref, bv_ref, bias_ref,
                       f_ref, g_ref, hf_ref, o_ref, gated_ref, gbf_ref, acc_ref, *, n_tiles, tf):
    j = pl.program_id(2)

    @pl.when(j == 0)
    def _():
        gated = _conv3(x1_ref[0], w1_ref[...], b1_ref[...]) * _conv3(v_ref[0], wv_ref[...], bv_ref[...])
        gated_ref[...] = gated
        gbf_ref[...] = gated.astype(BF16)
        acc_ref[...] = jnp.zeros_like(acc_ref)

    u = _dot(f_ref[...], gbf_ref[...])
    ur, ui = u[:tf], u[tf:]
    hr, hi = hf_ref[:tf, :], hf_ref[tf:, :]
    packed = (lax.broadcasted_iota(jnp.int32, ur.shape, 0) == 0) & (j == 0)
    yr = ur * hr - jnp.where(packed, 0.0, ui * hi)
    yi = jnp.where(packed, ui * hi, ur * hi + ui * hr)
    y = jnp.concatenate([yr, yi], axis=0).astype(BF16)
    acc_ref[...] += _dot(g_ref[0], y)

    @pl.when(j == n_tiles - 1)
    def _():
        gated = gated_ref[...]
        y = acc_ref[...] + gated * bias_ref[...]
        o_ref[0] = (_conv3(x0_ref[0], w0_ref[...], b0_ref[...]) * y).astype(BF16)


def hyena_mixer(zh, filt, short_w, short_b, bias):
    bsz, length, _ = zh.shape
    fwd, inv = _dft_matrices(length)
    tf = _dft_tile(length)
    n_tiles = length // tf
    hf = project(fwd, filt, tn=512, tm=min(512, 2 * length))
    cn = 256 if length > 512 else 512
    nct = D_HALF // cn

    def zblk(k):
        return pl.BlockSpec((1, length, cn), lambda b, c, j, k=k: (b, 0, k * nct + c))

    def wblk(k):
        return pl.BlockSpec((3, cn), lambda b, c, j, k=k: (0, k * nct + c))

    def bblk(k):
        return pl.BlockSpec((1, cn), lambda b, c, j, k=k: (0, k * nct + c))

    sb = short_b.reshape(1, -1)
    return pl.pallas_call(
        functools.partial(_hyena_conv_kernel, n_tiles=n_tiles, tf=tf),
        grid=(bsz, nct, n_tiles),
        in_specs=[zblk(0), zblk(1), zblk(2), wblk(0), wblk(1), wblk(2), bblk(0), bblk(1), bblk(2),
                  pl.BlockSpec((1, cn), lambda b, c, j: (0, c)),
                  pl.BlockSpec((2 * tf, length), lambda b, c, j: (j, 0)),
                  pl.BlockSpec((1, length, 2 * tf), lambda b, c, j: (j, 0, 0)),
                  pl.BlockSpec((2 * tf, cn), lambda b, c, j: (j, c))],
        out_specs=pl.BlockSpec((1, length, cn), lambda b, c, j: (b, 0, c)),
        out_shape=jax.ShapeDtypeStruct((bsz, length, D_HALF), BF16),
        scratch_shapes=[pltpu.VMEM((length, cn), F32), pltpu.VMEM((length, cn), BF16), pltpu.VMEM((length, cn), F32)],
        compiler_params=_params(3),
        name="hyena_conv",
    )(zh, zh, zh, short_w, short_w, short_w, sb, sb, sb, bias.reshape(1, -1), fwd, inv, hf)


RW_TL = 128
RW_PRE_TL = 128
RW_NB = 4
RW_PAIRS = RW_HEADS // 2
RW_YBLK = 64
RW_GROUP = 8
LANES = 128


def _block_ones():
    idx = np.arange(LANES) // RW_HD
    return (idx[:, None] == idx[None, :]).astype(np.float32)


def _segment_ones():
    return jnp.asarray(np.concatenate([_block_ones(), _block_ones()], axis=0), BF16)


def _segment_ones_pair():
    return jnp.asarray(np.kron(np.eye(2, dtype=np.float32), _block_ones()), BF16)


def _hi_lo(x):
    hi, lo = _split_bf16(x)
    return jnp.concatenate([hi, lo], axis=1)


def _head_sums(x, seg_ref):
    tiles = [_dot(_hi_lo(x[:, j * LANES:(j + 1) * LANES]), seg_ref[...]) for j in range(x.shape[1] // LANES)]
    return jnp.concatenate(tiles, axis=1)


def _rwkv_pre_kernel(z_ref, zp_ref, zn_ref, mu_ref, wl_ref, w0_ref, a0_ref, kk_ref, ka_ref, seg_ref,
                     r_ref, k_ref, v_ref, g_ref, an_ref, w0o_ref, kd0_ref, b0_ref, w1o_ref, kd1_ref, b1_ref,
                     *, n_tiles):
    i = pl.program_id(1)
    z = z_ref[0]
    row = lax.broadcasted_iota(jnp.int32, z.shape, 0)
    prev = jnp.where(i > 0, zp_ref[0, 7:8, :], 0.0)
    nxt = jnp.where(i < n_tiles - 1, zn_ref[0, 0:1, :], 0.0)
    zm1 = jnp.where(row == 0, prev, pltpu.roll(z, 1, 0))
    zp1 = jnp.where(row == z.shape[0] - 1, nxt, pltpu.roll(z, z.shape[0] - 1, 0))
    x = z + (0.5 * (zm1 + zp1) - z) * mu_ref[...]
    r = x[:, 0:D_HALF]
    k = x[:, D_HALF:2 * D_HALF]
    v = x[:, 2 * D_HALF:3 * D_HALF]
    lo = x[:, 3 * D_HALF:3 * D_HALF + RW_LORA]
    lane = lax.broadcasted_iota(jnp.int32, lo.shape, 1)
    act = jnp.where(lane < RW_W_RANK, jnp.tanh(lo),
                    jnp.where(lane < RW_W_RANK + RW_A_RANK, lo, jax.nn.sigmoid(lo)))
    up = _dot(act.astype(BF16), wl_ref[...])
    g = up[:, 4 * D_HALF:5 * D_HALF]
    kk = k * kk_ref[...]
    kk = kk * lax.rsqrt(_head_sums(kk * kk, seg_ref) + EPS)

    def put(ref, val):
        ref[0] = val

    put(r_ref, r)
    put(k_ref, k)
    put(v_ref, v)
    put(g_ref, g)
    put(an_ref, -kk)
    for d, (wo, kdo, bo) in enumerate(((w0o_ref, kd0_ref, b0_ref), (w1o_ref, kd1_ref, b1_ref))):
        logw = -jax.nn.softplus(-(w0_ref[d:d + 1, :] + up[:, d * D_HALF:(d + 1) * D_HALF])) - 0.5
        a = jax.nn.sigmoid(a0_ref[d:d + 1, :] + up[:, (2 + d) * D_HALF:(3 + d) * D_HALF])
        put(wo, jnp.exp(-jnp.exp(logw)))
        put(kdo, k * (1.0 + (a - 1.0) * ka_ref[...]))
        put(bo, kk * a)


def rwkv_pre(zr, mu, w_lora, w0, a0, k_k, k_a):
    bsz, length, width = zr.shape
    tl = RW_PRE_TL
    n_tiles = length // tl
    vec = lambda n: pl.BlockSpec((n, D_HALF), lambda b, i: (0, 0))
    out_blk = pl.BlockSpec((1, tl, D_HALF), lambda b, i: (b, i, 0))
    out_sds = jax.ShapeDtypeStruct((bsz, length, D_HALF), F32)
    return pl.pallas_call(
        functools.partial(_rwkv_pre_kernel, n_tiles=n_tiles),
        grid=(bsz, n_tiles),
        in_specs=[pl.BlockSpec((1, tl, width), lambda b, i: (b, i, 0)),
                  pl.BlockSpec((1, 8, width), lambda b, i: (b, jnp.maximum(i * (tl // 8) - 1, 0), 0)),
                  pl.BlockSpec((1, 8, width), lambda b, i: (b, jnp.minimum((i + 1) * (tl // 8), length // 8 - 1), 0)),
                  pl.BlockSpec((1, width), lambda b, i: (0, 0)),
                  pl.BlockSpec((RW_LORA, 5 * D_HALF), lambda b, i: (0, 0)),
                  vec(2), vec(2), vec(1), vec(1),
                  pl.BlockSpec((2 * LANES, LANES), lambda b, i: (0, 0))],
        out_specs=[out_blk] * 11,
        out_shape=[out_sds] * 11,
        compiler_params=_params(2),
        name="rwkv_pre",
    )(zr, zr, zr, mu, w_lora, w0, a0, k_k.reshape(1, -1), k_a.reshape(1, -1), _segment_ones())


def _rwkv_scan_kernel(r_ref, w_ref, k_ref, a_ref, b_ref, v_ref, s0_ref, seg_ref, seg2_ref, y_ref, fin_ref,
                      st_ref, lhs_ref, vlhs_ref, ylhs_ref, yacc_ref, *, reverse, n_chunks):
    c = pl.program_id(1)

    @pl.when(c == 0)
    def _():
        st_ref[...] = s0_ref[...]

    lane = lax.broadcasted_iota(jnp.int32, (RW_HD, LANES), 1)
    lane_in_head = lane % RW_HD
    diag = jnp.where(lane_in_head == lax.broadcasted_iota(jnp.int32, (RW_HD, LANES), 0), 1.0, 0.0)
    nt = RW_PAIRS
    chains = range(RW_NB)
    n_blocks = RW_TL // RW_YBLK

    def row(ref, cn, t, p):
        return ref[cn, pl.ds(t, 1), :][:, p * LANES:(p + 1) * LANES]

    def tile(sums, q):
        return sums[q * RW_HD:(q + 1) * RW_HD]

    def two_sums(tiles_ref, cn, lo, hi):
        return _dot(tiles_ref[cn, lo:hi].reshape((hi - lo) * RW_HD, 2 * LANES), seg2_ref[...])

    def step(t, u, vcol):
        for cn in chains:
            for p in range(nt):
                lhs_ref[cn, p] = _hi_lo(st_ref[cn, p] * row(a_ref, cn, t, p))
        sa = [_dot(lhs_ref[cn].reshape(nt * RW_HD, 2 * LANES), seg_ref[...]) for cn in chains]
        for cn in chains:
            for p in range(nt):
                s = (st_ref[cn, p] * row(w_ref, cn, t, p) + tile(sa[cn], p) * row(b_ref, cn, t, p)
                     + vcol[cn][p] * row(k_ref, cn, t, p))
                st_ref[cn, p] = s
                sr = (s * row(r_ref, cn, t, p)).astype(BF16)
                ylhs_ref[cn, p // 2, :, (p % 2) * LANES:(p % 2 + 1) * LANES] = sr
        ys = [two_sums(ylhs_ref, cn, 0, nt // 2) for cn in chains]
        hit = lane_in_head == t % RW_YBLK
        for cn in chains:
            for p in range(nt):
                y = tile(ys[cn], p // 2)[:, (p % 2) * LANES:(p % 2 + 1) * LANES]
                yacc_ref[cn, p] = jnp.where(hit, y, yacc_ref[cn, p])

    def step_pair(t0, t1):
        for cn in chains:
            for p in range(nt):
                vlhs_ref[cn, p] = jnp.concatenate([(diag * row(v_ref, cn, t0, p)).astype(BF16),
                                                   (diag * row(v_ref, cn, t1, p)).astype(BF16)], axis=1)
        vs = [two_sums(vlhs_ref, cn, 0, nt) for cn in chains]
        for u, t in enumerate((t0, t1)):
            step(t, u, [[tile(vs[cn], p)[:, u * LANES:(u + 1) * LANES] for p in range(nt)] for cn in chains])

    def step_group(g, blk):
        i0 = blk * RW_YBLK + g * RW_GROUP
        base = pl.multiple_of((RW_TL - RW_GROUP - i0) if reverse else i0, RW_GROUP)
        order = range(RW_GROUP - 1, -1, -1) if reverse else range(RW_GROUP)
        ts = [base + j for j in order]
        for u in range(0, RW_GROUP, 2):
            step_pair(ts[u], ts[u + 1])
        return blk

    def block(bi, carry):
        yacc_ref[...] = jnp.zeros_like(yacc_ref)
        lax.fori_loop(0, RW_YBLK // RW_GROUP, step_group, bi)
        tb = (n_blocks - 1 - bi) if reverse else bi
        rows = pl.ds(pl.multiple_of(tb * RW_YBLK, RW_YBLK), RW_YBLK)
        for cn in chains:
            for p in range(nt):
                yt = yacc_ref[cn, p].T
                for hh in range(2):
                    h = 2 * p + hh
                    y_ref[cn, rows, h * RW_HD:(h + 1) * RW_HD] = yt[hh * RW_YBLK:(hh + 1) * RW_YBLK]
        return carry

    lax.fori_loop(0, n_blocks, block, 0)

    @pl.when(c == n_chunks - 1)
    def _():
        fin_ref[...] = st_ref[...]


def rwkv_scan(r, w, k, a, b, v, s0, reverse):
    bsz, length, n = r.shape
    n_chunks = length // RW_TL

    def chunk(c):
        return (n_chunks - 1 - c) if reverse else c

    row_blk = pl.BlockSpec((RW_NB, RW_TL, n), lambda bi, c: (bi, chunk(c), 0))
    st_blk = pl.BlockSpec((RW_NB, RW_PAIRS, RW_HD, LANES), lambda bi, c: (bi, 0, 0, 0))
    tiles = lambda m: pltpu.VMEM((RW_NB, m, RW_HD, 2 * LANES), BF16)
    return pl.pallas_call(
        functools.partial(_rwkv_scan_kernel, reverse=reverse, n_chunks=n_chunks),
        grid=(bsz // RW_NB, n_chunks),
        in_specs=[row_blk] * 6 + [st_blk, pl.BlockSpec((2 * LANES, LANES), lambda bi, c: (0, 0)),
                                  pl.BlockSpec((2 * LANES, 2 * LANES), lambda bi, c: (0, 0))],
        out_specs=[row_blk, st_blk],
        out_shape=[jax.ShapeDtypeStruct((bsz, length, n), F32),
                   jax.ShapeDtypeStruct((bsz, RW_PAIRS, RW_HD, LANES), F32)],
        scratch_shapes=[pltpu.VMEM((RW_NB, RW_PAIRS, RW_HD, LANES), F32),
                        tiles(RW_PAIRS), tiles(RW_PAIRS), tiles(RW_PAIRS // 2),
                        pltpu.VMEM((RW_NB, RW_PAIRS, RW_HD, LANES), F32)],
        compiler_params=_params(2),
        name="rwkv_scan_rev" if reverse else "rwkv_scan_fwd",
    )(r, w, k, a, b, v, s0, _segment_ones(), _segment_ones_pair())


def _rwkv_post_kernel(yf_ref, yb_ref, r_ref, k_ref, v_ref, g_ref, rk_ref, lg_ref, lb_ref, seg_ref, o_ref):
    y = yf_ref[0] + yb_ref[0]
    mean = _head_sums(y, seg_ref) * (1.0 / RW_HD)
    yc = y - mean
    var = _head_sums(yc * yc, seg_ref) * (1.0 / RW_HD)
    y = yc * lax.rsqrt(var + GN_EPS) * lg_ref[...] + lb_ref[...]
    bonus = _head_sums(r_ref[0] * k_ref[0] * rk_ref[...], seg_ref) * v_ref[0]
    o_ref[0] = ((y + bonus) * g_ref[0]).astype(BF16)


def rwkv_post(yf, yb, r, k, v, g, r_k, ln_g, ln_b):
    bsz, length, n = r.shape
    tl = RW_TL
    blk = pl.BlockSpec((1, tl, n), lambda b, i: (b, i, 0))
    vec = pl.BlockSpec((1, n), lambda b, i: (0, 0))
    return pl.pallas_call(
        _rwkv_post_kernel,
        grid=(bsz, length // tl),
        in_specs=[blk] * 6 + [vec] * 3 + [pl.BlockSpec((2 * LANES, LANES), lambda b, i: (0, 0))],
        out_specs=blk,
        out_shape=jax.ShapeDtypeStruct((bsz, length, n), BF16),
        compiler_params=_params(2),
        name="rwkv_post",
    )(yf, yb, r, k, v, g, r_k.reshape(1, n), ln_g.reshape(1, n), ln_b.reshape(1, n), _segment_ones())


def rwkv_lora_weights(w2, a2, g2):
    w = jnp.zeros((RW_LORA, 5 * D_HALF), F32)
    for d in range(2):
        w = w.at[0:RW_W_RANK, d * D_HALF:(d + 1) * D_HALF].set(w2[d].astype(F32))
        w = w.at[RW_W_RANK:RW_W_RANK + RW_A_RANK, (2 + d) * D_HALF:(3 + d) * D_HALF].set(a2[d].astype(F32))
    w = w.at[RW_W_RANK + RW_A_RANK:, 4 * D_HALF:].set(g2.astype(F32))
    return w.astype(BF16)


def rwkv_mixer(zr, s0, rwp):
    mu, w0, w2, a0, a2, g2, k_k, k_a, r_k, ln_g, ln_b = rwp
    bsz, length = zr.shape[:2]
    mu_p = jnp.pad(mu.astype(F32), (0, RW_IN_PAD - RW_IN)).reshape(1, RW_IN_PAD)
    r, k, v, g, an, wd0, kd0, b0, wd1, kd1, b1 = rwkv_pre(zr, mu_p, rwkv_lora_weights(w2, a2, g2), w0, a0, k_k, k_a)
    if s0 is None:
        s0 = jnp.zeros((bsz, 2, RW_HEADS, RW_HD, RW_HD), F32)

    def pack_state(s):
        return s.astype(F32).reshape(bsz, RW_PAIRS, 2, RW_HD, RW_HD).transpose(0, 1, 3, 2, 4).reshape(
            bsz, RW_PAIRS, RW_HD, LANES)

    def unpack_state(s):
        return s.reshape(bsz, RW_PAIRS, RW_HD, 2, RW_HD).transpose(0, 1, 3, 2, 4).reshape(bsz, RW_HEADS, RW_HD, RW_HD)

    yf, fin_f = rwkv_scan(r, wd0, kd0, an, b0, v, pack_state(s0[:, 0]), reverse=False)
    yb, fin_b = rwkv_scan(r, wd1, kd1, an, b1, v, pack_state(s0[:, 1]), reverse=True)
    y = rwkv_post(yf, yb, r, k, v, g, r_k, ln_g, ln_b)
    return y, jnp.stack([unpack_state(fin_f), unpack_state(fin_b)], axis=1)


ROUTE_TILE = 512


def _rank_before(vals, n):
    idx = lax.broadcasted_iota(jnp.int32, vals.shape, 0)
    cnt = jnp.zeros(vals.shape, F32)
    for e in range(n):
        row = vals[e:e + 1]
        cnt = cnt + jnp.where((row > vals) | ((row == vals) & (idx > e)), 1.0, 0.0)
    return cnt


def _route_kernel(lg_ref, bias_ref, tril_ref, triu_ref, te_ref, gt_ref, rk_ref, cnt_ref, carry_ref):
    @pl.when(pl.program_id(0) == 0)
    def _():
        carry_ref[...] = jnp.zeros_like(carry_ref)

    t = lg_ref.shape[1]
    per = N_EXPERTS // N_ROUTE_GROUPS
    scores = jax.nn.sigmoid(lg_ref[...])
    sel = scores + bias_ref[...]
    sel3 = sel.reshape(N_ROUTE_GROUPS, per, t)
    m1 = jnp.max(sel3, axis=1, keepdims=True)
    within = lax.broadcasted_iota(jnp.int32, sel3.shape, 1)
    first = jnp.min(jnp.where(sel3 == m1, within, per), axis=1, keepdims=True)
    m2 = jnp.max(jnp.where(within == first, -jnp.inf, sel3), axis=1, keepdims=True)
    group_score = (m1 + m2).reshape(N_ROUTE_GROUPS, t)
    group_ok = _rank_before(group_score, N_ROUTE_GROUPS) < TOPK_ROUTE_GROUPS
    expert_ok = jnp.broadcast_to(group_ok.reshape(N_ROUTE_GROUPS, 1, t), sel3.shape).reshape(N_EXPERTS, t)
    masked = jnp.where(expert_ok, sel, NEG_INF)
    chosen = _rank_before(masked, N_EXPERTS) < TOP_K
    gates = jnp.where(chosen, scores, 0.0)
    gates = gates / jnp.sum(gates, axis=0, keepdims=True) * ROUTED_SCALE
    onehot = jnp.where(chosen, 1.0, 0.0).astype(BF16)
    order = _dot(tril_ref[...], onehot)
    before = carry_ref[:, 0:1] + _dot(onehot, triu_ref[...])
    carry_ref[...] = carry_ref[...] + jnp.sum(jnp.where(chosen, 1.0, 0.0), axis=1, keepdims=True)
    eidx = lax.broadcasted_iota(jnp.int32, chosen.shape, 0).astype(F32)
    for j in range(TOP_K):
        pick = chosen & (order == float(j + 1))
        te_ref[j:j + 1, :] = jnp.sum(jnp.where(pick, eidx, 0.0), axis=0, keepdims=True).astype(jnp.int32)
        gt_ref[j:j + 1, :] = jnp.sum(jnp.where(pick, gates, 0.0), axis=0, keepdims=True)
        rk_ref[j:j + 1, :] = jnp.sum(jnp.where(pick, before, 0.0), axis=0, keepdims=True).astype(jnp.int32)
    cnt_ref[...] = carry_ref[...]


def route(logits_t, b_router):
    e, n = logits_t.shape
    t = ROUTE_TILE
    tril = jnp.asarray(np.tril(np.ones((e, e))), BF16)
    triu = jnp.asarray(np.triu(np.ones((t, t)), 1), BF16)
    out_blk = pl.BlockSpec((TOP_K, t), lambda i: (0, i))
    te, gt, rk, cnt = pl.pallas_call(
        _route_kernel,
        grid=(n // t,),
        in_specs=[pl.BlockSpec((e, t), lambda i: (0, i)),
                  pl.BlockSpec((e, 1), lambda i: (0, 0)),
                  pl.BlockSpec((e, e), lambda i: (0, 0)),
                  pl.BlockSpec((t, t), lambda i: (0, 0))],
        out_specs=[out_blk, out_blk, out_blk, pl.BlockSpec((e, 128), lambda i: (0, 0))],
        out_shape=[jax.ShapeDtypeStruct((TOP_K, n), jnp.int32), jax.ShapeDtypeStruct((TOP_K, n), F32),
                   jax.ShapeDtypeStruct((TOP_K, n), jnp.int32), jax.ShapeDtypeStruct((e, 128), F32)],
        scratch_shapes=[pltpu.VMEM((e, 128), F32)],
        compiler_params=_params(1),
        name="route",
    )(logits_t, b_router.astype(F32).reshape(e, 1), tril, triu)
    return te, gt, rk, cnt[:, 0]


def _expert_kernel(be_ref, nu_ref, x_ref, wg_ref, wu_ref, wd_ref, o_ref, wgb_ref, wub_ref, wdb_ref):
    i = pl.program_id(0)
    used = i < nu_ref[0]
    fresh = (i == 0) | (be_ref[i] != be_ref[jnp.maximum(i - 1, 0)])

    @pl.when(used & fresh)
    def _():
        wgb_ref[...] = wg_ref[0, 0].astype(BF16)
        wub_ref[...] = wu_ref[0, 0].astype(BF16)
        wdb_ref[...] = wd_ref[0, 0].astype(BF16)

    @pl.when(used)
    def _():
        x = x_ref[...]
        gate = _dot(x, wgb_ref[...])
        h = (gate * jax.nn.sigmoid(gate) * _dot(x, wub_ref[...])).astype(BF16)
        o_ref[...] = _dot(h, wdb_ref[...]).astype(o_ref.dtype)

    @pl.when(jnp.logical_not(used))
    def _():
        o_ref[...] = jnp.zeros_like(o_ref)


def expert_ffn(xs, block_expert, n_used, w_gate, w_up, w_down, layer):
    s, d = xs.shape
    de = w_gate.shape[3]
    nb = s // MOE_ROWS
    grid_spec = pltpu.PrefetchScalarGridSpec(
        num_scalar_prefetch=2,
        grid=(nb,),
        in_specs=[pl.BlockSpec((MOE_ROWS, d), lambda i, be, nu: (i, 0)),
                  pl.BlockSpec((1, 1, d, de), lambda i, be, nu: (layer, be[i], 0, 0)),
                  pl.BlockSpec((1, 1, d, de), lambda i, be, nu: (layer, be[i], 0, 0)),
                  pl.BlockSpec((1, 1, de, d), lambda i, be, nu: (layer, be[i], 0, 0))],
        out_specs=pl.BlockSpec((MOE_ROWS, d), lambda i, be, nu: (i, 0)),
        scratch_shapes=[pltpu.VMEM((d, de), BF16), pltpu.VMEM((d, de), BF16), pltpu.VMEM((de, d), BF16)],
    )
    return pl.pallas_call(
        _expert_kernel,
        grid_spec=grid_spec,
        out_shape=jax.ShapeDtypeStruct((s, d), BF16),
        compiler_params=_params(1),
        name="expert_ffn",
    )(block_expert, n_used, xs, w_gate, w_up, w_down)


def _combine_kernel(x_ref, yg_ref, gt_ref, sh_ref, mod_ref, o_ref, *, gate_idx):
    routed = jnp.zeros(x_ref.shape, F32)
    for j in range(TOP_K):
        routed = routed + gt_ref[:, j:j + 1] * yg_ref[j].astype(F32)
    g = mod_ref[0][gate_idx:gate_idx + 1]
    o_ref[...] = x_ref[...] + g * (routed + sh_ref[...].astype(F32))


def moe_combine(x, yg, gates, shared, mods, gate_idx, n_ctx, tm=128):
    n, d = x.shape
    row = functools.partial(_cond_row, n_ctx_tiles=n_ctx // tm, tiles_per_sample=DEC_SEQ // tm)
    return pl.pallas_call(
        functools.partial(_combine_kernel, gate_idx=gate_idx),
        grid=(n // tm,),
        in_specs=[pl.BlockSpec((tm, d), lambda i: (i, 0)),
                  pl.BlockSpec((TOP_K, tm, d), lambda i: (0, i, 0)),
                  pl.BlockSpec((tm, TOP_K), lambda i: (i, 0)),
                  pl.BlockSpec((tm, d), lambda i: (i, 0)),
                  pl.BlockSpec((1, 6, d), lambda i: (row(i), 0, 0))],
        out_specs=pl.BlockSpec((tm, d), lambda i: (i, 0)),
        out_shape=jax.ShapeDtypeStruct((n, d), F32),
        compiler_params=_params(1),
        name="moe_combine",
    )(x, yg, gates, shared, mods)


def moe_ffn(x, h, logits_t, mods, gate_idx, n_ctx, layer, b_router, w_gate, w_up, w_down, ws_gate, ws_up, ws_down):
    n, d = h.shape
    top_e, gates, rank, counts = route(logits_t, b_router)
    counts = counts.astype(jnp.int32)
    padded = (counts + MOE_ROWS - 1) // MOE_ROWS * MOE_ROWS
    pad_end = jnp.cumsum(padded)
    pad_start = pad_end - padded
    n_slots = n * TOP_K + N_EXPERTS * MOE_ROWS
    nb = n_slots // MOE_ROWS
    expert_ids = jnp.arange(N_EXPERTS, dtype=jnp.int32)
    start_of = jnp.sum(jnp.where(top_e[..., None] == expert_ids, pad_start, 0), axis=-1)
    pos = start_of + rank
    tok = jnp.broadcast_to(jnp.arange(n, dtype=jnp.int32)[None, :], pos.shape)
    slot_tok = jnp.zeros((n_slots,), jnp.int32).at[pos.reshape(-1)].set(tok.reshape(-1), unique_indices=True)
    block_start = jnp.arange(nb, dtype=jnp.int32) * MOE_ROWS
    block_expert = jnp.minimum(jnp.sum(pad_end[None, :] <= block_start[:, None], axis=1), N_EXPERTS - 1).astype(jnp.int32)
    n_used = (pad_end[-1:] // MOE_ROWS).astype(jnp.int32)
    xs = h[slot_tok]
    yb = expert_ffn(xs, block_expert, n_used, w_gate, w_up, w_down, layer)
    shared = expert_ffn(h, jnp.zeros((n // MOE_ROWS,), jnp.int32), jnp.full((1,), n // MOE_ROWS, jnp.int32),
                        ws_gate[:, None], ws_up[:, None], ws_down[:, None], layer)
    yg = yb[pos]
    return moe_combine(x, yg, gates.T, shared, mods, gate_idx, n_ctx)


def _layer_ab(h, n_ctx, bsz_ctx, bsz_lat, p):
    z = project(h, p['w_in'], tn=512)
    s5w = s5_weights(*p['s5_disc'])
    outs = []
    extras = {}
    for name, lo, bsz, length in (('ctx', 0, bsz_ctx, SEQ), ('lat', n_ctx, bsz_lat, DEC_SEQ)):
        zg = z[lo:lo + bsz * length].reshape(bsz, length, 4 * D_HALF)
        u = zg[..., :D_HALF]
        q, k, v = (_heads_first(zg[..., (1 + j) * D_HALF:(2 + j) * D_HALF], bsz, length, NA_HEADS) for j in range(3))
        if name == 'ctx':
            y_a, s5_fin = s5_mixer(u, None, p['s5_glu'], s5w)
            y_b, k_n = context_attention(q, k, v, p['q_g'], p['k_g'])
            extras = {'k': k_n, 'v': v, 's5': s5_fin}
        else:
            y_a, _ = s5_mixer(u, p['s5_state'], p['s5_glu'], s5w)
            y_b = neighbourhood_attention(q, k, v, p['ck'], p['cv'], p['rpb'], p['q_g'], p['k_g'])
        outs.append(jnp.concatenate([y_a, y_b], axis=-1).reshape(bsz * length, 2 * D_HALF))
    return jnp.concatenate(outs, axis=0), extras


def _layer_cd(h, n_ctx, bsz_ctx, bsz_lat, p):
    w_in = p['w_in']
    zh_all = project(h, w_in[:, :3 * D_HALF], tn=512)
    w_rw = jnp.pad(w_in[:, 3 * D_HALF:], ((0, 0), (0, RW_IN_PAD - RW_IN)))
    zr_all = project(h, w_rw, tn=512)
    outs = []
    extras = {}
    for name, lo, bsz, length in (('ctx', 0, bsz_ctx, SEQ), ('lat', n_ctx, bsz_lat, DEC_SEQ)):
        zh = zh_all[lo:lo + bsz * length].reshape(bsz, length, 3 * D_HALF)
        zr = zr_all[lo:lo + bsz * length].reshape(bsz, length, RW_IN_PAD)
        filt = hyena_filter(length, *p['hy_filter'])
        y_c = hyena_mixer(zh, filt, p['hy_short_w'], p['hy_short_b'], p['hy_bias'])
        y_d, rw_fin = rwkv_mixer(zr, None if name == 'ctx' else p['rw_state'], p['rwp'])
        if name == 'ctx':
            extras = {'rw': rw_fin}
        outs.append(jnp.concatenate([y_c, y_d], axis=-1).reshape(bsz * length, 2 * D_HALF))
    return jnp.concatenate(outs, axis=0), extras


def kernel(x_prompt, x_sample, cache_na_k, cache_na_v, state_s5, state_rwkv, c, c_ctx, ada_w, ada_b, norm_mix, norm_ffn, ab_w_in, ab_w_out, s5_lam_re, s5_lam_im, s5_log_step, s5_b_re, s5_b_im, s5_c_re, s5_c_im, s5_d, s5_w_glu, s5_b_glu, na_q_norm, na_k_norm, na_rpb, cd_w_in, cd_w_out, hy_short_w, hy_short_b, hy_w1, hy_b1, hy_freq, hy_w2, hy_b2, hy_w3, hy_decay, hy_bias, rw_mu, rw_w0, rw_w2, rw_a0, rw_a2, rw_g2, rw_k_k, rw_k_a, rw_r_k, rw_ln_g, rw_ln_b, moe_router, moe_router_bias, moe_w_gate, moe_w_up, moe_w_down, moe_ws_gate, moe_ws_up, moe_ws_down):
    bsz_ctx, seq, d = x_prompt.shape
    bsz_lat, dec_seq, _ = x_sample.shape
    assert (seq, dec_seq, d) == (SEQ, DEC_SEQ, D_MODEL) and bsz_lat + 1 <= 8
    depth = ada_w.shape[0]
    n_ctx = bsz_ctx * seq
    x = jnp.concatenate([x_prompt.reshape(n_ctx, d), x_sample.reshape(bsz_lat * dec_seq, d)], axis=0).astype(F32)

    cond = jnp.concatenate([c_ctx[None, :], c, jnp.zeros((8 - 1 - bsz_lat, d), c.dtype)], axis=0).astype(F32)
    mods_all = ada_table(cond, ada_w, ada_b).reshape(depth, 8, 6, d)

    new_k, new_v, new_s5, new_rw = [], [], [], []
    for l in range(depth):
        mods = mods_all[l]
        i = l // 2
        h = modulate(x, norm_mix[l], mods, 0, 1, n_ctx)
        if l % 2 == 0:
            p = {'w_in': ab_w_in[i], 'q_g': na_q_norm[i], 'k_g': na_k_norm[i], 'rpb': na_rpb[i],
                 'ck': cache_na_k[:, i], 'cv': cache_na_v[:, i], 's5_state': state_s5[:, i],
                 's5_disc': (s5_lam_re[i], s5_lam_im[i], s5_log_step[i], s5_b_re[i], s5_b_im[i], s5_c_re[i], s5_c_im[i]),
                 's5_glu': (s5_d[i], s5_w_glu[i], s5_b_glu[i])}
            y, ex = _layer_ab(h, n_ctx, bsz_ctx, bsz_lat, p)
            new_k.append(ex['k'])
            new_v.append(ex['v'])
            new_s5.append(ex['s5'])
            w_out = ab_w_out[i]
        else:
            p = {'w_in': cd_w_in[i], 'hy_filter': (hy_w1[i], hy_b1[i], hy_freq[i], hy_w2[i], hy_b2[i], hy_w3[i], hy_decay[i]),
                 'hy_short_w': hy_short_w[i], 'hy_short_b': hy_short_b[i], 'hy_bias': hy_bias[i],
                 'rw_state': state_rwkv[:, i],
                 'rwp': (rw_mu[i], rw_w0[i], rw_w2[i], rw_a0[i], rw_a2[i], rw_g2[i], rw_k_k[i], rw_k_a[i], rw_r_k[i],
                         rw_ln_g[i], rw_ln_b[i])}
            y, ex = _layer_cd(h, n_ctx, bsz_ctx, bsz_lat, p)
            new_rw.append(ex['rw'])
            w_out = cd_w_out[i]
        x = project_residual(y, w_out, x, mods, 2, n_ctx)
        h, logits_t = modulate(x, norm_ffn[l], mods, 3, 4, n_ctx, w_router_t=moe_router[l].T)
        x = moe_ffn(x, h, logits_t, mods, 5, n_ctx, l, moe_router_bias[l], moe_w_gate, moe_w_up, moe_w_down,
                    moe_ws_gate, moe_ws_up, moe_ws_down)

    y_prompt = x[:n_ctx].reshape(bsz_ctx, seq, d)
    y_sample = x[n_ctx:].reshape(bsz_lat, dec_seq, d)
    return (y_prompt, y_sample, jnp.stack(new_k, axis=1), jnp.stack(new_v, axis=1),
            jnp.stack(new_s5, axis=1), jnp.stack(new_rw, axis=1))
```

```python
import functools
import math

import numpy as np
import jax
import jax.numpy as jnp
from jax import lax
from jax.experimental import pallas as pl
from jax.experimental.pallas import tpu as pltpu

F32 = jnp.float32
BF16 = jnp.bfloat16

D_MODEL = 2048
D_HALF = 1024
SEQ = 256
DEC_SEQ = 2048
GRID_W = 64
S5_GROUP = 16
S5_GROUPS = 64
S5_STATE = 64
S5_LANES = S5_GROUPS * S5_STATE
NA_HEADS = 16
NA_HD = 64
NA_KH = 8
NA_KW = 16
HY_BANDS = 16
RW_HEADS = 16
RW_HD = 64
RW_W_RANK = 64
RW_A_RANK = 64
RW_G_RANK = 128
RW_LORA = 256
RW_IN = 3 * D_HALF + RW_LORA
RW_IN_PAD = 3584
N_EXPERTS = 64
TOP_K = 8
N_ROUTE_GROUPS = 8
TOPK_ROUTE_GROUPS = 4
D_EXPERT = 512
ROUTED_SCALE = 2.5
EPS = 1e-6
GN_EPS = 64e-5
NEG_INF = -1e30

TOK_TILE = 256
MOE_ROWS = 256
VMEM_LIMIT = 56 * 1024 * 1024


def _params(n_axes, vmem=VMEM_LIMIT):
    return pltpu.CompilerParams(dimension_semantics=("arbitrary",) * n_axes, vmem_limit_bytes=vmem)


def _dot(a, b):
    return jnp.dot(a, b, preferred_element_type=F32)


def _dot_nt(a, b):
    return lax.dot_general(a, b, (((1,), (1,)), ((), ())), preferred_element_type=F32)


def _split_bf16(x):
    hi = x.astype(BF16)
    lo = (x - hi.astype(F32)).astype(BF16)
    return hi, lo


def _cond_row(i, n_ctx_tiles, tiles_per_sample):
    return jnp.where(i < n_ctx_tiles, 0, 1 + (i - n_ctx_tiles) // tiles_per_sample)


def _ada_kernel(c_ref, w_ref, b_ref, o_ref):
    c = c_ref[...]
    s = (c * jax.nn.sigmoid(c)).astype(BF16)
    o_ref[0] = _dot(s, w_ref[0].astype(BF16)) + b_ref[0]


def ada_table(cond8, ada_w, ada_b):
    depth, d, n = ada_w.shape
    tn = 1024
    return pl.pallas_call(
        _ada_kernel,
        grid=(depth, n // tn),
        in_specs=[pl.BlockSpec((8, d), lambda l, j: (0, 0)),
                  pl.BlockSpec((1, d, tn), lambda l, j: (l, 0, j)),
                  pl.BlockSpec((1, 1, tn), lambda l, j: (l, 0, j))],
        out_specs=pl.BlockSpec((1, 8, tn), lambda l, j: (l, 0, j)),
        out_shape=jax.ShapeDtypeStruct((depth, 8, n), F32),
        compiler_params=_params(2),
        name="ada_table",
    )(cond8, ada_w, ada_b.reshape(depth, 1, n))


def _modulate_kernel(x_ref, g_ref, mod_ref, h_ref, *, shift_idx, scale_idx):
    x = x_ref[...]
    y = x * lax.rsqrt(jnp.mean(x * x, axis=-1, keepdims=True) + EPS)
    m = mod_ref[0]
    h = y * g_ref[...] * (1.0 + m[scale_idx:scale_idx + 1]) + m[shift_idx:shift_idx + 1]
    h_ref[...] = h.astype(BF16)


def _modulate_router_kernel(x_ref, g_ref, mod_ref, wr_ref, h_ref, lg_ref, *, shift_idx, scale_idx):
    x = x_ref[...]
    y = x * lax.rsqrt(jnp.mean(x * x, axis=-1, keepdims=True) + EPS)
    m = mod_ref[0]
    h = y * g_ref[...] * (1.0 + m[scale_idx:scale_idx + 1]) + m[shift_idx:shift_idx + 1]
    h_hi, h_lo = _split_bf16(h)
    h_ref[...] = h_hi
    w_hi, w_lo = _split_bf16(wr_ref[...])
    lg_ref[...] = _dot_nt(w_hi, h_hi) + _dot_nt(w_hi, h_lo) + _dot_nt(w_lo, h_hi)


def modulate(x, gain, mods, shift_idx, scale_idx, n_ctx, w_router_t=None):
    n, d = x.shape
    tm = TOK_TILE
    row = functools.partial(_cond_row, n_ctx_tiles=n_ctx // tm, tiles_per_sample=DEC_SEQ // tm)
    in_specs = [pl.BlockSpec((tm, d), lambda i: (i, 0)),
                pl.BlockSpec((1, d), lambda i: (0, 0)),
                pl.BlockSpec((1, 6, d), lambda i: (row(i), 0, 0))]
    if w_router_t is None:
        return pl.pallas_call(
            functools.partial(_modulate_kernel, shift_idx=shift_idx, scale_idx=scale_idx),
            grid=(n // tm,),
            in_specs=in_specs,
            out_specs=pl.BlockSpec((tm, d), lambda i: (i, 0)),
            out_shape=jax.ShapeDtypeStruct((n, d), BF16),
            compiler_params=_params(1),
            name="modulate",
        )(x, gain.reshape(1, d), mods)
    e = w_router_t.shape[0]
    return pl.pallas_call(
        functools.partial(_modulate_router_kernel, shift_idx=shift_idx, scale_idx=scale_idx),
        grid=(n // tm,),
        in_specs=in_specs + [pl.BlockSpec((e, d), lambda i: (0, 0))],
        out_specs=[pl.BlockSpec((tm, d), lambda i: (i, 0)),
                   pl.BlockSpec((e, tm), lambda i: (0, i))],
        out_shape=[jax.ShapeDtypeStruct((n, d), BF16), jax.ShapeDtypeStruct((e, n), F32)],
        compiler_params=_params(1),
        name="modulate_router",
    )(x, gain.reshape(1, d), mods, w_router_t)


def _proj_kernel(x_ref, w_ref, o_ref, wbf_ref):
    @pl.when(pl.program_id(1) == 0)
    def _():
        wbf_ref[...] = w_ref[...].astype(BF16)

    o_ref[...] = _dot(x_ref[...], wbf_ref[...]).astype(o_ref.dtype)


def project(x, w, tn, tm=512, out_dtype=F32):
    m, k = x.shape
    n = w.shape[1]
    return pl.pallas_call(
        _proj_kernel,
        grid=(n // tn, m // tm),
        in_specs=[pl.BlockSpec((tm, k), lambda j, i: (i, 0)),
                  pl.BlockSpec((k, tn), lambda j, i: (0, j))],
        out_specs=pl.BlockSpec((tm, tn), lambda j, i: (i, j)),
        out_shape=jax.ShapeDtypeStruct((m, n), out_dtype),
        scratch_shapes=[pltpu.VMEM((k, tn), BF16)],
        compiler_params=_params(2),
        name="project",
    )(x, w)


def _proj_residual_kernel(y_ref, w_ref, x_ref, mod_ref, o_ref, wbf_ref, *, gate_idx):
    @pl.when(pl.program_id(1) == 0)
    def _():
        wbf_ref[...] = w_ref[...].astype(BF16)

    g = mod_ref[0][gate_idx:gate_idx + 1]
    o_ref[...] = x_ref[...] + g * _dot(y_ref[...], wbf_ref[...])


def project_residual(y, w, x, mods, gate_idx, n_ctx, tn=512, tm=512):
    m, k = y.shape
    n = w.shape[1]
    row = functools.partial(_cond_row, n_ctx_tiles=n_ctx // tm, tiles_per_sample=DEC_SEQ // tm)
    return pl.pallas_call(
        functools.partial(_proj_residual_kernel, gate_idx=gate_idx),
        grid=(n // tn, m // tm),
        in_specs=[pl.BlockSpec((tm, k), lambda j, i: (i, 0)),
                  pl.BlockSpec((k, tn), lambda j, i: (0, j)),
                  pl.BlockSpec((tm, tn), lambda j, i: (i, j)),
                  pl.BlockSpec((1, 6, tn), lambda j, i: (row(i), 0, j))],
        out_specs=pl.BlockSpec((tm, tn), lambda j, i: (i, j)),
        out_shape=jax.ShapeDtypeStruct((m, n), F32),
        scratch_shapes=[pltpu.VMEM((k, tn), BF16)],
        compiler_params=_params(2),
        name="project_residual",
    )(y, w, x, mods)


S5_CHUNK = 64
S5_ROWS = 8
S5_BLK = 8
S5_SCAN_LANES = 1024


def _s5_kernel(u_ref, wbr_ref, wbi_ref, wcr_ref, wci_ref, lam_ref, x0_ref, y_ref, fin_ref,
               bur_ref, bui_ref, st_ref, *, n_chunks):
    d = pl.program_id(0)
    c = pl.program_id(2)
    tc = S5_CHUNK
    cin = S5_BLK * S5_GROUP
    cst = S5_BLK * S5_STATE

    @pl.when(c == 0)
    def _():
        st_ref[...] = x0_ref[0]

    u = u_ref[...].reshape(tc * S5_ROWS, D_HALF).astype(BF16)
    for k in range(S5_GROUPS // S5_BLK):
        uk = u[:, k * cin:(k + 1) * cin]
        bur_ref[:, k * cst:(k + 1) * cst] = _dot(uk, wbr_ref[0, k])
        bui_ref[:, k * cst:(k + 1) * cst] = _dot(uk, wbi_ref[0, k])

    for j in range(S5_LANES // S5_SCAN_LANES):
        sl = slice(j * S5_SCAN_LANES, (j + 1) * S5_SCAN_LANES)
        lr = jnp.broadcast_to(lam_ref[0, 0:1, sl], (S5_ROWS, S5_SCAN_LANES))
        li = jnp.broadcast_to(lam_ref[0, 1:2, sl], (S5_ROWS, S5_SCAN_LANES))

        def step(i, carry, sl=sl, lr=lr, li=li):
            sr, si = carry
            t = jnp.where(d == 0, i, tc - 1 - i)
            row = pl.multiple_of(t * S5_ROWS, S5_ROWS)
            nr = lr * sr - li * si + bur_ref[pl.ds(row, S5_ROWS), sl]
            ni = lr * si + li * sr + bui_ref[pl.ds(row, S5_ROWS), sl]
            bur_ref[pl.ds(row, S5_ROWS), sl] = nr
            bui_ref[pl.ds(row, S5_ROWS), sl] = ni
            return nr, ni

        sr, si = lax.fori_loop(0, tc, step, (st_ref[0, :, sl], st_ref[1, :, sl]), unroll=4)
        st_ref[0, :, sl] = sr
        st_ref[1, :, sl] = si

    xr = bur_ref[...].astype(BF16)
    xi = bui_ref[...].astype(BF16)
    for k in range(S5_GROUPS // S5_BLK):
        yk = _dot(xr[:, k * cst:(k + 1) * cst], wcr_ref[0, k]) + _dot(xi[:, k * cst:(k + 1) * cst], wci_ref[0, k])
        y_ref[0, :, :, k * cin:(k + 1) * cin] = yk.reshape(tc, S5_ROWS, cin)

    @pl.when(c == n_chunks - 1)
    def _():
        fin_ref[0] = st_ref[...]


def s5_scan(u_t, x0, wbr, wbi, wcr, wci, lam):
    length, bsz, _ = u_t.shape
    n_chunks = length // S5_CHUNK
    nblk = S5_GROUPS // S5_BLK
    cin = S5_BLK * S5_GROUP
    cst = S5_BLK * S5_STATE

    def chunk(d, c):
        return jnp.where(d == 0, c, n_chunks - 1 - c)

    return pl.pallas_call(
        functools.partial(_s5_kernel, n_chunks=n_chunks),
        grid=(2, bsz // S5_ROWS, n_chunks),
        in_specs=[pl.BlockSpec((S5_CHUNK, S5_ROWS, D_HALF), lambda d, b, c: (chunk(d, c), b, 0)),
                  pl.BlockSpec((1, nblk, cin, cst), lambda d, b, c: (d, 0, 0, 0)),
                  pl.BlockSpec((1, nblk, cin, cst), lambda d, b, c: (d, 0, 0, 0)),
                  pl.BlockSpec((1, nblk, cst, cin), lambda d, b, c: (d, 0, 0, 0)),
                  pl.BlockSpec((1, nblk, cst, cin), lambda d, b, c: (d, 0, 0, 0)),
                  pl.BlockSpec((1, 2, S5_LANES), lambda d, b, c: (d, 0, 0)),
                  pl.BlockSpec((1, 2, S5_ROWS, S5_LANES), lambda d, b, c: (d, 0, b, 0))],
        out_specs=[pl.BlockSpec((1, S5_CHUNK, S5_ROWS, D_HALF), lambda d, b, c: (d, chunk(d, c), b, 0)),
                   pl.BlockSpec((1, 2, S5_ROWS, S5_LANES), lambda d, b, c: (d, 0, b, 0))],
        out_shape=[jax.ShapeDtypeStruct((2, length, bsz, D_HALF), F32),
                   jax.ShapeDtypeStruct((2, 2, bsz, S5_LANES), F32)],
        scratch_shapes=[pltpu.VMEM((S5_CHUNK * S5_ROWS, S5_LANES), F32),
                        pltpu.VMEM((S5_CHUNK * S5_ROWS, S5_LANES), F32),
                        pltpu.VMEM((2, S5_ROWS, S5_LANES), F32)],
        compiler_params=_params(3),
        name="s5_scan",
    )(u_t, wbr, wbi, wcr, wci, lam, x0)


def s5_weights(lam_re, lam_im, log_step, b_re, b_im, c_re, c_im):
    lam = lax.complex(lam_re.astype(F32), lam_im.astype(F32))
    lam_bar = jnp.exp(lam * jnp.exp(log_step.astype(F32))[..., None])
    b_bar = ((lam_bar - 1.0) / lam)[..., None] * lax.complex(b_re.astype(F32), b_im.astype(F32))
    nblk = S5_GROUPS // S5_BLK
    eye = jnp.eye(S5_BLK, dtype=F32)

    def embed_in(b):
        b = b.reshape(2, nblk, S5_BLK, S5_STATE, S5_GROUP)
        w = jnp.einsum('dkgpc,gh->dkgchp', b, eye)
        return w.reshape(2, nblk, S5_BLK * S5_GROUP, S5_BLK * S5_STATE).astype(BF16)

    def embed_out(cm):
        cm = cm.reshape(2, nblk, S5_BLK, S5_GROUP, S5_STATE)
        w = jnp.einsum('dkgcp,gh->dkgphc', cm, eye)
        return w.reshape(2, nblk, S5_BLK * S5_STATE, S5_BLK * S5_GROUP).astype(BF16)

    lam_rows = jnp.stack([lam_bar.real.reshape(2, S5_LANES), lam_bar.imag.reshape(2, S5_LANES)], axis=1)
    return (embed_in(b_bar.real), embed_in(b_bar.imag),
            embed_out(c_re.astype(F32)), embed_out(-c_im.astype(F32)), lam_rows)


def _glu_kernel(yf_ref, yb_ref, u_ref, d_ref, w_ref, b_ref, o_ref, wbf_ref):
    @pl.when(pl.program_id(0) == 0)
    def _():
        wbf_ref[...] = w_ref[...].astype(BF16)

    y = jax.nn.gelu(u_ref[...] * d_ref[...] + yf_ref[0] + yb_ref[0])
    o_ref[...] = (y * jax.nn.sigmoid(_dot(y.astype(BF16), wbf_ref[...]) + b_ref[...])).astype(BF16)


def s5_glu(y_dirs, u, d_skip, w_glu, b_glu, tm=512):
    m, n = u.shape
    return pl.pallas_call(
        _glu_kernel,
        grid=(m // tm,),
        in_specs=[pl.BlockSpec((1, tm, n), lambda i: (0, i, 0)),
                  pl.BlockSpec((1, tm, n), lambda i: (1, i, 0)),
                  pl.BlockSpec((tm, n), lambda i: (i, 0)),
                  pl.BlockSpec((1, n), lambda i: (0, 0)),
                  pl.BlockSpec((n, n), lambda i: (0, 0)),
                  pl.BlockSpec((1, n), lambda i: (0, 0))],
        out_specs=pl.BlockSpec((tm, n), lambda i: (i, 0)),
        out_shape=jax.ShapeDtypeStruct((m, n), BF16),
        scratch_shapes=[pltpu.VMEM((n, n), BF16)],
        compiler_params=_params(1),
        name="s5_glu",
    )(y_dirs, y_dirs, u, d_skip.reshape(1, n), w_glu, b_glu.reshape(1, n))


def s5_mixer(u, x0, s5p, weights):
    bsz, length, _ = u.shape
    bp = -(-bsz // S5_ROWS) * S5_ROWS
    wbr, wbi, wcr, wci, lam = weights
    d_skip, w_glu, b_glu = s5p
    u_t = jnp.swapaxes(u, 0, 1)
    if x0 is None:
        x0_t = jnp.zeros((2, 2, bp, S5_LANES), F32)
    else:
        x0_t = x0.astype(F32).reshape(bsz, 2, 2, S5_LANES).transpose(1, 2, 0, 3)
    if bp != bsz:
        u_t = jnp.pad(u_t, ((0, 0), (0, bp - bsz), (0, 0)))
        x0_t = jnp.pad(x0_t, ((0, 0), (0, 0), (0, bp - bsz), (0, 0)))
    y_dirs, fin = s5_scan(u_t, x0_t, wbr, wbi, wcr, wci, lam)
    y = s5_glu(y_dirs.reshape(2, length * bp, D_HALF), u_t.reshape(length * bp, D_HALF), d_skip, w_glu, b_glu)
    y = jnp.swapaxes(y.reshape(length, bp, D_HALF)[:, :bsz], 0, 1)
    fin = fin[:, :, :bsz].transpose(2, 0, 1, 3).reshape(bsz, 2, 2, S5_GROUPS, S5_STATE)
    return y, fin


def _head_rms(x, g):
    return x * lax.rsqrt(jnp.mean(x * x, axis=-1, keepdims=True) + EPS) * g


def _ctx_attn_kernel(q_ref, k_ref, v_ref, qg_ref, kg_ref, o_ref, kn_ref):
    scale = NA_HD ** -0.5
    for h in range(NA_HEADS):
        q = _head_rms(q_ref[0, h], qg_ref[...])
        k = _head_rms(k_ref[0, h], kg_ref[...])
        kn_ref[0, h] = k
        s = _dot_nt(q.astype(BF16), k.astype(BF16)) * scale
        p = jnp.exp(s - jnp.max(s, axis=-1, keepdims=True))
        l = jnp.sum(p, axis=-1, keepdims=True)
        o = _dot(p.astype(BF16), v_ref[0, h].astype(BF16)) / l
        o_ref[0, :, h * NA_HD:(h + 1) * NA_HD] = o.astype(BF16)


def context_attention(q, k, v, q_g, k_g):
    bsz, nh, t, hd = q.shape
    blk = pl.BlockSpec((1, nh, t, hd), lambda b: (b, 0, 0, 0))
    vec = pl.BlockSpec((1, hd), lambda b: (0, 0))
    return pl.pallas_call(
        _ctx_attn_kernel,
        grid=(bsz,),
        in_specs=[blk, blk, blk, vec, vec],
        out_specs=[pl.BlockSpec((1, t, nh * hd), lambda b: (b, 0, 0)), blk],
        out_shape=[jax.ShapeDtypeStruct((bsz, t, nh * hd), BF16), jax.ShapeDtypeStruct((bsz, nh, t, hd), F32)],
        compiler_params=_params(1),
        name="context_attention",
    )(q, k, v, q_g.reshape(1, hd), k_g.reshape(1, hd))


NA_PAIR = 2


def _na_kernel(q_ref, k_ref, v_ref, ck_ref, cv_ref, bias_ref, qg_ref, kg_ref, o_ref, kn_ref, vn_ref, *, rows):
    scale = NA_HD ** -0.5
    win = NA_KH * GRID_W
    for hh in range(NA_PAIR):
        kn_ref[...] = _head_rms(k_ref[0, hh], kg_ref[...]).astype(BF16)
        vn_ref[...] = v_ref[0, hh].astype(BF16)
        ck = ck_ref[0, hh].astype(BF16)
        cv = cv_ref[0, hh].astype(BF16)
        for r in range(rows):
            rs = min(max(r - NA_KH // 2, 0), rows - NA_KH)
            cls = rs - r + NA_KH - 1
            q = _head_rms(q_ref[0, hh, r * GRID_W:(r + 1) * GRID_W, :], qg_ref[...]).astype(BF16)
            kw = kn_ref[rs * GRID_W:rs * GRID_W + win, :]
            vw = vn_ref[rs * GRID_W:rs * GRID_W + win, :]
            s_w = _dot_nt(q, kw) * scale + bias_ref[hh, cls]
            s_c = _dot_nt(q, ck) * scale
            m = jnp.maximum(jnp.max(s_w, axis=-1, keepdims=True), jnp.max(s_c, axis=-1, keepdims=True))
            p_w = jnp.exp(s_w - m)
            p_c = jnp.exp(s_c - m)
            l = jnp.sum(p_w, axis=-1, keepdims=True) + jnp.sum(p_c, axis=-1, keepdims=True)
            o = (_dot(p_w.astype(BF16), vw) + _dot(p_c.astype(BF16), cv)) / l
            o_ref[0, r * GRID_W:(r + 1) * GRID_W, hh * NA_HD:(hh + 1) * NA_HD] = o.astype(BF16)


def na_bias_table(rpb, rows):
    qc = np.arange(GRID_W)
    kc = np.arange(GRID_W)
    c0 = np.clip(qc - NA_KW // 2, 0, GRID_W - NA_KW)
    ok = (kc[None, :] >= c0[:, None]) & (kc[None, :] < c0[:, None] + NA_KW)
    dc = np.clip(kc[None, :] - qc[:, None] + NA_KW - 1, 0, 2 * NA_KW - 2)
    dr = np.arange(NA_KH)[:, None] + np.arange(NA_KH)[None, :]
    b = rpb.astype(F32)[:, dr][:, :, :, dc]
    b = jnp.where(ok[None, None, None], b, NEG_INF)
    return b.transpose(0, 1, 3, 2, 4).reshape(rpb.shape[0], NA_KH, GRID_W, NA_KH * GRID_W)


def neighbourhood_attention(q, k, v, ck, cv, rpb, q_g, k_g):
    bsz, nh, length, hd = q.shape
    past = ck.shape[2]
    rows = length // GRID_W
    bias = na_bias_table(rpb, rows)
    blk = pl.BlockSpec((1, NA_PAIR, length, hd), lambda b, h: (b, h, 0, 0))
    cblk = pl.BlockSpec((1, NA_PAIR, past, hd), lambda b, h: (b, h, 0, 0))
    vec = pl.BlockSpec((1, hd), lambda b, h: (0, 0))
    return pl.pallas_call(
        functools.partial(_na_kernel, rows=rows),
        grid=(bsz, nh // NA_PAIR),
        in_specs=[blk, blk, blk, cblk, cblk,
                  pl.BlockSpec((NA_PAIR, NA_KH, GRID_W, NA_KH * GRID_W), lambda b, h: (h, 0, 0, 0)),
                  vec, vec],
        out_specs=pl.BlockSpec((1, length, NA_PAIR * hd), lambda b, h: (b, 0, h)),
        out_shape=jax.ShapeDtypeStruct((bsz, length, nh * hd), BF16),
        scratch_shapes=[pltpu.VMEM((length, hd), BF16), pltpu.VMEM((length, hd), BF16)],
        compiler_params=_params(2),
        name="neighbourhood_attention",
    )(q, k, v, ck, cv, bias, q_g.reshape(1, hd), k_g.reshape(1, hd))


def _heads_first(t, bsz, length, nh):
    return t.reshape(bsz, length, nh, -1).transpose(0, 2, 1, 3)


HY_EMB_PAD = 64


def _dot3(a, b):
    a_hi, a_lo = _split_bf16(a)
    b_hi, b_lo = _split_bf16(b)
    return _dot(a_hi, b_hi) + _dot(a_hi, b_lo) + _dot(a_lo, b_hi)


def _hyena_filter_kernel(z_ref, w1_ref, b1_ref, f_ref, w2_ref, b2_ref, w3_ref, dec_ref, o_ref):
    length = z_ref.shape[0]
    h = jnp.sin(f_ref[0:1, :] * (_dot3(z_ref[...], w1_ref[...]) + b1_ref[...]))
    h = jnp.sin(f_ref[1:2, :] * (_dot3(h, w2_ref[...]) + b2_ref[...]))
    filt = _dot3(h, w3_ref[...])
    t = lax.broadcasted_iota(jnp.int32, filt.shape, 0).astype(F32)
    offset = jnp.abs(t - float(length // 2)) / float(length)
    filt = filt * jnp.exp(-offset * jnp.abs(dec_ref[...]))
    o_ref[...] = filt / (jnp.sum(jnp.abs(filt), axis=0, keepdims=True) + EPS)


def _hyena_embedding(length):
    t = np.arange(length, dtype=np.float32) / np.float32(length)
    bands = np.linspace(1e-4, HY_BANDS - 1, HY_BANDS, dtype=np.float32)
    ang = (2 * math.pi * t[:, None] * bands[None, :]).astype(np.float32)
    z = np.concatenate([t[:, None], np.cos(ang), np.sin(ang)], axis=-1).astype(np.float32)
    return np.pad(z, ((0, 0), (0, HY_EMB_PAD - z.shape[1])))


def hyena_filter(length, w1, b1, freq, w2, b2, w3, decay):
    z = jnp.asarray(_hyena_embedding(length))
    w1p = jnp.pad(w1.astype(F32), ((0, HY_EMB_PAD - w1.shape[0]), (0, 0)))
    n = w3.shape[1]
    return pl.pallas_call(
        _hyena_filter_kernel,
        out_shape=jax.ShapeDtypeStruct((length, n), F32),
        compiler_params=_params(0),
        name="hyena_filter",
    )(z, w1p, b1.reshape(1, -1), freq, w2, b2.reshape(1, -1), w3, decay.reshape(1, n))


def _dft_tile(length):
    return min(length, 512)


def _dft_matrices(length):
    n = 2 * length
    tf = _dft_tile(length)
    f = np.arange(length)
    s = np.arange(length)
    ang = 2 * np.pi * ((f[:, None] * s[None, :]) % n) / n
    fc = np.cos(ang)
    fs = -np.sin(ang)
    fs[0] = np.where(s % 2 == 0, 1.0, -1.0)
    nt = np.arange(length) + length // 2
    ang_i = 2 * np.pi * ((nt[:, None] * f[None, :]) % n) / n
    wf = np.where(f == 0, 1.0, 2.0)[None, :]
    gc = wf * np.cos(ang_i) / n
    gs = -wf * np.sin(ang_i) / n
    gs[:, 0] = np.where(nt % 2 == 0, 1.0, -1.0) / n
    n_tiles = length // tf
    fwd = np.concatenate([fc.reshape(n_tiles, tf, length), fs.reshape(n_tiles, tf, length)], axis=1)
    inv = np.concatenate([gc.reshape(length, n_tiles, tf), gs.reshape(length, n_tiles, tf)], axis=2)
    return (jnp.asarray(fwd.reshape(n_tiles * 2 * tf, length), BF16),
            jnp.asarray(inv.transpose(1, 0, 2), BF16))


def _shift_rows(x, up):
    n = x.shape[0]
    row = lax.broadcasted_iota(jnp.int32, x.shape, 0)
    if up:
        return jnp.where(row == n - 1, 0.0, pltpu.roll(x, n - 1, 0))
    return jnp.where(row == 0, 0.0, pltpu.roll(x, 1, 0))


def _conv3(x, w, b):
    return _shift_rows(x, False) * w[0:1] + x * w[1:2] + _shift_rows(x, True) * w[2:3] + b


def _hyena_conv_kernel(x0_ref, x1_ref, v_ref, w0_ref, w1_ref, wv_ref, b0_ref, b1_ref, bv_ref, bias_ref,
                       f_ref, g_ref, hf_ref, o_ref, gated_ref, gbf_ref, acc_ref, *, n_tiles, tf):
    j = pl.program_id(2)

    @pl.when(j == 0)
    def _():
        gated = _conv3(x1_ref[0], w1_ref[...], b1_ref[...]) * _conv3(v_ref[0], wv_ref[...], bv_ref[...])
        gated_ref[...] = gated
        gbf_ref[...] = gated.astype(BF16)
        acc_ref[...] = jnp.zeros_like(acc_ref)

    u = _dot(f_ref[...], gbf_ref[...])
    ur, ui = u[:tf], u[tf:]
    hr, hi = hf_ref[:tf, :], hf_ref[tf:, :]
    packed = (lax.broadcasted_iota(jnp.int32, ur.shape, 0) == 0) & (j == 0)
    yr = ur * hr - jnp.where(packed, 0.0, ui * hi)
    yi = jnp.where(packed, ui * hi, ur * hi + ui * hr)
    y = jnp.concatenate([yr, yi], axis=0).astype(BF16)
    acc_ref[...] += _dot(g_ref[0], y)

    @pl.when(j == n_tiles - 1)
    def _():
        gated = gated_ref[...]
        y = acc_ref[...] + gated * bias_ref[...]
        o_ref[0] = (_conv3(x0_ref[0], w0_ref[...], b0_ref[...]) * y).astype(BF16)


def hyena_mixer(zh, filt, short_w, short_b, bias):
    bsz, length, _ = zh.shape
    fwd, inv = _dft_matrices(length)
    tf = _dft_tile(length)
    n_tiles = length // tf
    hf = project(fwd, filt, tn=512, tm=min(512, 2 * length))
    cn = 256 if length > 512 else 512
    nct = D_HALF // cn

    def zblk(k):
        return pl.BlockSpec((1, length, cn), lambda b, c, j, k=k: (b, 0, k * nct + c))

    def wblk(k):
        return pl.BlockSpec((3, cn), lambda b, c, j, k=k: (0, k * nct + c))

    def bblk(k):
        return pl.BlockSpec((1, cn), lambda b, c, j, k=k: (0, k * nct + c))

    sb = short_b.reshape(1, -1)
    return pl.pallas_call(
        functools.partial(_hyena_conv_kernel, n_tiles=n_tiles, tf=tf),
        grid=(bsz, nct, n_tiles),
        in_specs=[zblk(0), zblk(1), zblk(2), wblk(0), wblk(1), wblk(2), bblk(0), bblk(1), bblk(2),
                  pl.BlockSpec((1, cn), lambda b, c, j: (0, c)),
                  pl.BlockSpec((2 * tf, length), lambda b, c, j: (j, 0)),
                  pl.BlockSpec((1, length, 2 * tf), lambda b, c, j: (j, 0, 0)),
                  pl.BlockSpec((2 * tf, cn), lambda b, c, j: (j, c))],
        out_specs=pl.BlockSpec((1, length, cn), lambda b, c, j: (b, 0, c)),
        out_shape=jax.ShapeDtypeStruct((bsz, length, D_HALF), BF16),
        scratch_shapes=[pltpu.VMEM((length, cn), F32), pltpu.VMEM((length, cn), BF16), pltpu.VMEM((length, cn), F32)],
        compiler_params=_params(3),
        name="hyena_conv",
    )(zh, zh, zh, short_w, short_w, short_w, sb, sb, sb, bias.reshape(1, -1), fwd, inv, hf)


RW_TL = 128
RW_PRE_TL = 128
RW_NB = 4
RW_PAIRS = RW_HEADS // 2
RW_YBLK = 64
RW_GROUP = 8
LANES = 128


def _block_ones():
    idx = np.arange(LANES) // RW_HD
    return (idx[:, None] == idx[None, :]).astype(np.float32)


def _segment_ones():
    return jnp.asarray(np.concatenate([_block_ones(), _block_ones()], axis=0), BF16)


def _segment_ones_pair():
    return jnp.asarray(np.kron(np.eye(2, dtype=np.float32), _block_ones()), BF16)


def _hi_lo(x):
    hi, lo = _split_bf16(x)
    return jnp.concatenate([hi, lo], axis=1)


def _head_sums(x, seg_ref):
    tiles = [_dot(_hi_lo(x[:, j * LANES:(j + 1) * LANES]), seg_ref[...]) for j in range(x.shape[1] // LANES)]
    return jnp.concatenate(tiles, axis=1)


def _rwkv_pre_kernel(z_ref, zp_ref, zn_ref, mu_ref, wl_ref, w0_ref, a0_ref, kk_ref, ka_ref, seg_ref,
                     r_ref, k_ref, v_ref, g_ref, an_ref, w0o_ref, kd0_ref, b0_ref, w1o_ref, kd1_ref, b1_ref,
                     *, n_tiles):
    i = pl.program_id(1)
    z = z_ref[0]
    row = lax.broadcasted_iota(jnp.int32, z.shape, 0)
    prev = jnp.where(i > 0, zp_ref[0, 7:8, :], 0.0)
    nxt = jnp.where(i < n_tiles - 1, zn_ref[0, 0:1, :], 0.0)
    zm1 = jnp.where(row == 0, prev, pltpu.roll(z, 1, 0))
    zp1 = jnp.where(row == z.shape[0] - 1, nxt, pltpu.roll(z, z.shape[0] - 1, 0))
    x = z + (0.5 * (zm1 + zp1) - z) * mu_ref[...]
    r = x[:, 0:D_HALF]
    k = x[:, D_HALF:2 * D_HALF]
    v = x[:, 2 * D_HALF:3 * D_HALF]
    lo = x[:, 3 * D_HALF:3 * D_HALF + RW_LORA]
    lane = lax.broadcasted_iota(jnp.int32, lo.shape, 1)
    act = jnp.where(lane < RW_W_RANK, jnp.tanh(lo),
                    jnp.where(lane < RW_W_RANK + RW_A_RANK, lo, jax.nn.sigmoid(lo)))
    up = _dot(act.astype(BF16), wl_ref[...])
    g = up[:, 4 * D_HALF:5 * D_HALF]
    kk = k * kk_ref[...]
    kk = kk * lax.rsqrt(_head_sums(kk * kk, seg_ref) + EPS)

    def put(ref, val):
        ref[0] = val

    put(r_ref, r)
    put(k_ref, k)
    put(v_ref, v)
    put(g_ref, g)
    put(an_ref, -kk)
    for d, (wo, kdo, bo) in enumerate(((w0o_ref, kd0_ref, b0_ref), (w1o_ref, kd1_ref, b1_ref))):
        logw = -jax.nn.softplus(-(w0_ref[d:d + 1, :] + up[:, d * D_HALF:(d + 1) * D_HALF])) - 0.5
        a = jax.nn.sigmoid(a0_ref[d:d + 1, :] + up[:, (2 + d) * D_HALF:(3 + d) * D_HALF])
        put(wo, jnp.exp(-jnp.exp(logw)))
        put(kdo, k * (1.0 + (a - 1.0) * ka_ref[...]))
        put(bo, kk * a)


def rwkv_pre(zr, mu, w_lora, w0, a0, k_k, k_a):
    bsz, length, width = zr.shape
    tl = RW_PRE_TL
    n_tiles = length // tl
    vec = lambda n: pl.BlockSpec((n, D_HALF), lambda b, i: (0, 0))
    out_blk = pl.BlockSpec((1, tl, D_HALF), lambda b, i: (b, i, 0))
    out_sds = jax.ShapeDtypeStruct((bsz, length, D_HALF), F32)
    return pl.pallas_call(
        functools.partial(_rwkv_pre_kernel, n_tiles=n_tiles),
        grid=(bsz, n_tiles),
        in_specs=[pl.BlockSpec((1, tl, width), lambda b, i: (b, i, 0)),
                  pl.BlockSpec((1, 8, width), lambda b, i: (b, jnp.maximum(i * (tl // 8) - 1, 0), 0)),
                  pl.BlockSpec((1, 8, width), lambda b, i: (b, jnp.minimum((i + 1) * (tl // 8), length // 8 - 1), 0)),
                  pl.BlockSpec((1, width), lambda b, i: (0, 0)),
                  pl.BlockSpec((RW_LORA, 5 * D_HALF), lambda b, i: (0, 0)),
                  vec(2), vec(2), vec(1), vec(1),
                  pl.BlockSpec((2 * LANES, LANES), lambda b, i: (0, 0))],
        out_specs=[out_blk] * 11,
        out_shape=[out_sds] * 11,
        compiler_params=_params(2),
        name="rwkv_pre",
    )(zr, zr, zr, mu, w_lora, w0, a0, k_k.reshape(1, -1), k_a.reshape(1, -1), _segment_ones())


def _rwkv_scan_kernel(r_ref, w_ref, k_ref, a_ref, b_ref, v_ref, s0_ref, seg_ref, seg2_ref, y_ref, fin_ref,
                      st_ref, lhs_ref, vlhs_ref, ylhs_ref, yacc_ref, *, reverse, n_chunks):
    c = pl.program_id(1)

    @pl.when(c == 0)
    def _():
        st_ref[...] = s0_ref[...]

    lane = lax.broadcasted_iota(jnp.int32, (RW_HD, LANES), 1)
    lane_in_head = lane % RW_HD
    diag = jnp.where(lane_in_head == lax.broadcasted_iota(jnp.int32, (RW_HD, LANES), 0), 1.0, 0.0)
    nt = RW_PAIRS
    chains = range(RW_NB)
    n_blocks = RW_TL // RW_YBLK

    def row(ref, cn, t, p):
        return ref[cn, pl.ds(t, 1), :][:, p * LANES:(p + 1) * LANES]

    def tile(sums, q):
        return sums[q * RW_HD:(q + 1) * RW_HD]

    def two_sums(tiles_ref, cn, lo, hi):
        return _dot(tiles_ref[cn, lo:hi].reshape((hi - lo) * RW_HD, 2 * LANES), seg2_ref[...])

    def step(t, u, vcol):
        for cn in chains:
            for p in range(nt):
                sa_in = (st_ref[cn, p] * row(a_ref, cn, t, p)).astype(BF16)
                lhs_ref[cn, p // 2, :, (p % 2) * LANES:(p % 2 + 1) * LANES] = sa_in
        sa = [two_sums(lhs_ref, cn, 0, nt // 2) for cn in chains]
        for cn in chains:
            for p in range(nt):
                sa_p = tile(sa[cn], p // 2)[:, (p % 2) * LANES:(p % 2 + 1) * LANES]
                s = (st_ref[cn, p] * row(w_ref, cn, t, p) + sa_p * row(b_ref, cn, t, p)
                     + vcol[cn][p] * row(k_ref, cn, t, p))
                st_ref[cn, p] = s
                sr = (s * row(r_ref, cn, t, p)).astype(BF16)
                ylhs_ref[cn, p // 2, :, (p % 2) * LANES:(p % 2 + 1) * LANES] = sr
        ys = [two_sums(ylhs_ref, cn, 0, nt // 2) for cn in chains]
        hit = lane_in_head == t % RW_YBLK
        for cn in chains:
            for p in range(nt):
                y = tile(ys[cn], p // 2)[:, (p % 2) * LANES:(p % 2 + 1) * LANES]
                yacc_ref[cn, p] = jnp.where(hit, y, yacc_ref[cn, p])

    def step_pair(t0, t1):
        for cn in chains:
            for p in range(nt):
                vlhs_ref[cn, p] = jnp.concatenate([(diag * row(v_ref, cn, t0, p)).astype(BF16),
                                                   (diag * row(v_ref, cn, t1, p)).astype(BF16)], axis=1)
        vs = [two_sums(vlhs_ref, cn, 0, nt) for cn in chains]
        for u, t in enumerate((t0, t1)):
            step(t, u, [[tile(vs[cn], p)[:, u * LANES:(u + 1) * LANES] for p in range(nt)] for cn in chains])

    def step_group(g, blk):
        i0 = blk * RW_YBLK + g * RW_GROUP
        base = pl.multiple_of((RW_TL - RW_GROUP - i0) if reverse else i0, RW_GROUP)
        order = range(RW_GROUP - 1, -1, -1) if reverse else range(RW_GROUP)
        ts = [base + j for j in order]
        for u in range(0, RW_GROUP, 2):
            step_pair(ts[u], ts[u + 1])
        return blk

    def block(bi, carry):
        yacc_ref[...] = jnp.zeros_like(yacc_ref)
        lax.fori_loop(0, RW_YBLK // RW_GROUP, step_group, bi)
        tb = (n_blocks - 1 - bi) if reverse else bi
        rows = pl.ds(pl.multiple_of(tb * RW_YBLK, RW_YBLK), RW_YBLK)
        for cn in chains:
            for p in range(nt):
                yt = yacc_ref[cn, p].T
                for hh in range(2):
                    h = 2 * p + hh
                    y_ref[cn, rows, h * RW_HD:(h + 1) * RW_HD] = yt[hh * RW_YBLK:(hh + 1) * RW_YBLK]
        return carry

    lax.fori_loop(0, n_blocks, block, 0)

    @pl.when(c == n_chunks - 1)
    def _():
        fin_ref[...] = st_ref[...]


def rwkv_scan(r, w, k, a, b, v, s0, reverse):
    bsz, length, n = r.shape
    n_chunks = length // RW_TL

    def chunk(c):
        return (n_chunks - 1 - c) if reverse else c

    row_blk = pl.BlockSpec((RW_NB, RW_TL, n), lambda bi, c: (bi, chunk(c), 0))
    st_blk = pl.BlockSpec((RW_NB, RW_PAIRS, RW_HD, LANES), lambda bi, c: (bi, 0, 0, 0))
    tiles = lambda m: pltpu.VMEM((RW_NB, m, RW_HD, 2 * LANES), BF16)
    return pl.pallas_call(
        functools.partial(_rwkv_scan_kernel, reverse=reverse, n_chunks=n_chunks),
        grid=(bsz // RW_NB, n_chunks),
        in_specs=[row_blk] * 6 + [st_blk, pl.BlockSpec((2 * LANES, LANES), lambda bi, c: (0, 0)),
                                  pl.BlockSpec((2 * LANES, 2 * LANES), lambda bi, c: (0, 0))],
        out_specs=[row_blk, st_blk],
        out_shape=[jax.ShapeDtypeStruct((bsz, length, n), F32),
                   jax.ShapeDtypeStruct((bsz, RW_PAIRS, RW_HD, LANES), F32)],
        scratch_shapes=[pltpu.VMEM((RW_NB, RW_PAIRS, RW_HD, LANES), F32),
                        tiles(RW_PAIRS // 2), tiles(RW_PAIRS), tiles(RW_PAIRS // 2),
                        pltpu.VMEM((RW_NB, RW_PAIRS, RW_HD, LANES), F32)],
        compiler_params=_params(2),
        name="rwkv_scan_rev" if reverse else "rwkv_scan_fwd",
    )(r, w, k, a, b, v, s0, _segment_ones(), _segment_ones_pair())


def _rwkv_post_kernel(yf_ref, yb_ref, r_ref, k_ref, v_ref, g_ref, rk_ref, lg_ref, lb_ref, seg_ref, o_ref):
    y = yf_ref[0] + yb_ref[0]
    mean = _head_sums(y, seg_ref) * (1.0 / RW_HD)
    yc = y - mean
    var = _head_sums(yc * yc, seg_ref) * (1.0 / RW_HD)
    y = yc * lax.rsqrt(var + GN_EPS) * lg_ref[...] + lb_ref[...]
    bonus = _head_sums(r_ref[0] * k_ref[0] * rk_ref[...], seg_ref) * v_ref[0]
    o_ref[0] = ((y + bonus) * g_ref[0]).astype(BF16)


def rwkv_post(yf, yb, r, k, v, g, r_k, ln_g, ln_b):
    bsz, length, n = r.shape
    tl = RW_TL
    blk = pl.BlockSpec((1, tl, n), lambda b, i: (b, i, 0))
    vec = pl.BlockSpec((1, n), lambda b, i: (0, 0))
    return pl.pallas_call(
        _rwkv_post_kernel,
        grid=(bsz, length // tl),
        in_specs=[blk] * 6 + [vec] * 3 + [pl.BlockSpec((2 * LANES, LANES), lambda b, i: (0, 0))],
        out_specs=blk,
        out_shape=jax.ShapeDtypeStruct((bsz, length, n), BF16),
        compiler_params=_params(2),
        name="rwkv_post",
    )(yf, yb, r, k, v, g, r_k.reshape(1, n), ln_g.reshape(1, n), ln_b.reshape(1, n), _segment_ones())


def rwkv_lora_weights(w2, a2, g2):
    w = jnp.zeros((RW_LORA, 5 * D_HALF), F32)
    for d in range(2):
        w = w.at[0:RW_W_RANK, d * D_HALF:(d + 1) * D_HALF].set(w2[d].astype(F32))
        w = w.at[RW_W_RANK:RW_W_RANK + RW_A_RANK, (2 + d) * D_HALF:(3 + d) * D_HALF].set(a2[d].astype(F32))
    w = w.at[RW_W_RANK + RW_A_RANK:, 4 * D_HALF:].set(g2.astype(F32))
    return w.astype(BF16)


def rwkv_mixer(zr, s0, rwp):
    mu, w0, w2, a0, a2, g2, k_k, k_a, r_k, ln_g, ln_b = rwp
    bsz, length = zr.shape[:2]
    mu_p = jnp.pad(mu.astype(F32), (0, RW_IN_PAD - RW_IN)).reshape(1, RW_IN_PAD)
    r, k, v, g, an, wd0, kd0, b0, wd1, kd1, b1 = rwkv_pre(zr, mu_p, rwkv_lora_weights(w2, a2, g2), w0, a0, k_k, k_a)
    if s0 is None:
        s0 = jnp.zeros((bsz, 2, RW_HEADS, RW_HD, RW_HD), F32)

    def pack_state(s):
        return s.astype(F32).reshape(bsz, RW_PAIRS, 2, RW_HD, RW_HD).transpose(0, 1, 3, 2, 4).reshape(
            bsz, RW_PAIRS, RW_HD, LANES)

    def unpack_state(s):
        return s.reshape(bsz, RW_PAIRS, RW_HD, 2, RW_HD).transpose(0, 1, 3, 2, 4).reshape(bsz, RW_HEADS, RW_HD, RW_HD)

    yf, fin_f = rwkv_scan(r, wd0, kd0, an, b0, v, pack_state(s0[:, 0]), reverse=False)
    yb, fin_b = rwkv_scan(r, wd1, kd1, an, b1, v, pack_state(s0[:, 1]), reverse=True)
    y = rwkv_post(yf, yb, r, k, v, g, r_k, ln_g, ln_b)
    return y, jnp.stack([unpack_state(fin_f), unpack_state(fin_b)], axis=1)


ROUTE_TILE = 512


def _rank_before(vals, n):
    idx = lax.broadcasted_iota(jnp.int32, vals.shape, 0)
    cnt = jnp.zeros(vals.shape, F32)
    for e in range(n):
        row = vals[e:e + 1]
        cnt = cnt + jnp.where((row > vals) | ((row == vals) & (idx > e)), 1.0, 0.0)
    return cnt


def _route_kernel(lg_ref, bias_ref, tril_ref, triu_ref, te_ref, gt_ref, rk_ref, cnt_ref, carry_ref):
    @pl.when(pl.program_id(0) == 0)
    def _():
        carry_ref[...] = jnp.zeros_like(carry_ref)

    t = lg_ref.shape[1]
    per = N_EXPERTS // N_ROUTE_GROUPS
    scores = jax.nn.sigmoid(lg_ref[...])
    sel = scores + bias_ref[...]
    sel3 = sel.reshape(N_ROUTE_GROUPS, per, t)
    m1 = jnp.max(sel3, axis=1, keepdims=True)
    within = lax.broadcasted_iota(jnp.int32, sel3.shape, 1)
    first = jnp.min(jnp.where(sel3 == m1, within, per), axis=1, keepdims=True)
    m2 = jnp.max(jnp.where(within == first, -jnp.inf, sel3), axis=1, keepdims=True)
    group_score = (m1 + m2).reshape(N_ROUTE_GROUPS, t)
    group_ok = _rank_before(group_score, N_ROUTE_GROUPS) < TOPK_ROUTE_GROUPS
    expert_ok = jnp.broadcast_to(group_ok.reshape(N_ROUTE_GROUPS, 1, t), sel3.shape).reshape(N_EXPERTS, t)
    masked = jnp.where(expert_ok, sel, NEG_INF)
    chosen = _rank_before(masked, N_EXPERTS) < TOP_K
    gates = jnp.where(chosen, scores, 0.0)
    gates = gates / jnp.sum(gates, axis=0, keepdims=True) * ROUTED_SCALE
    onehot = jnp.where(chosen, 1.0, 0.0).astype(BF16)
    order = _dot(tril_ref[...], onehot)
    before = carry_ref[:, 0:1] + _dot(onehot, triu_ref[...])
    carry_ref[...] = carry_ref[...] + jnp.sum(jnp.where(chosen, 1.0, 0.0), axis=1, keepdims=True)
    eidx = lax.broadcasted_iota(jnp.int32, chosen.shape, 0).astype(F32)
    for j in range(TOP_K):
        pick = chosen & (order == float(j + 1))
        te_ref[j:j + 1, :] = jnp.sum(jnp.where(pick, eidx, 0.0), axis=0, keepdims=True).astype(jnp.int32)
        gt_ref[j:j + 1, :] = jnp.sum(jnp.where(pick, gates, 0.0), axis=0, keepdims=True)
        rk_ref[j:j + 1, :] = jnp.sum(jnp.where(pick, before, 0.0), axis=0, keepdims=True).astype(jnp.int32)
    cnt_ref[...] = carry_ref[...]


def route(logits_t, b_router):
    e, n = logits_t.shape
    t = ROUTE_TILE
    tril = jnp.asarray(np.tril(np.ones((e, e))), BF16)
    triu = jnp.asarray(np.triu(np.ones((t, t)), 1), BF16)
    out_blk = pl.BlockSpec((TOP_K, t), lambda i: (0, i))
    te, gt, rk, cnt = pl.pallas_call(
        _route_kernel,
        grid=(n // t,),
        in_specs=[pl.BlockSpec((e, t), lambda i: (0, i)),
                  pl.BlockSpec((e, 1), lambda i: (0, 0)),
                  pl.BlockSpec((e, e), lambda i: (0, 0)),
                  pl.BlockSpec((t, t), lambda i: (0, 0))],
        out_specs=[out_blk, out_blk, out_blk, pl.BlockSpec((e, 128), lambda i: (0, 0))],
        out_shape=[jax.ShapeDtypeStruct((TOP_K, n), jnp.int32), jax.ShapeDtypeStruct((TOP_K, n), F32),
                   jax.ShapeDtypeStruct((TOP_K, n), jnp.int32), jax.ShapeDtypeStruct((e, 128), F32)],
        scratch_shapes=[pltpu.VMEM((e, 128), F32)],
        compiler_params=_params(1),
        name="route",
    )(logits_t, b_router.astype(F32).reshape(e, 1), tril, triu)
    return te, gt, rk, cnt[:, 0]


def _expert_kernel(be_ref, nu_ref, x_ref, wg_ref, wu_ref, wd_ref, o_ref, wgb_ref, wub_ref, wdb_ref):
    i = pl.program_id(0)
    used = i < nu_ref[0]
    fresh = (i == 0) | (be_ref[i] != be_ref[jnp.maximum(i - 1, 0)])

    @pl.when(used & fresh)
    def _():
        wgb_ref[...] = wg_ref[0, 0].astype(BF16)
        wub_ref[...] = wu_ref[0, 0].astype(BF16)
        wdb_ref[...] = wd_ref[0, 0].astype(BF16)

    @pl.when(used)
    def _():
        x = x_ref[...]
        gate = _dot(x, wgb_ref[...])
        h = (gate * jax.nn.sigmoid(gate) * _dot(x, wub_ref[...])).astype(BF16)
        o_ref[...] = _dot(h, wdb_ref[...]).astype(o_ref.dtype)

    @pl.when(jnp.logical_not(used))
    def _():
        o_ref[...] = jnp.zeros_like(o_ref)


def expert_ffn(xs, block_expert, n_used, w_gate, w_up, w_down, layer):
    s, d = xs.shape
    de = w_gate.shape[3]
    nb = s // MOE_ROWS
    grid_spec = pltpu.PrefetchScalarGridSpec(
        num_scalar_prefetch=2,
        grid=(nb,),
        in_specs=[pl.BlockSpec((MOE_ROWS, d), lambda i, be, nu: (i, 0)),
                  pl.BlockSpec((1, 1, d, de), lambda i, be, nu: (layer, be[i], 0, 0)),
                  pl.BlockSpec((1, 1, d, de), lambda i, be, nu: (layer, be[i], 0, 0)),
                  pl.BlockSpec((1, 1, de, d), lambda i, be, nu: (layer, be[i], 0, 0))],
        out_specs=pl.BlockSpec((MOE_ROWS, d), lambda i, be, nu: (i, 0)),
        scratch_shapes=[pltpu.VMEM((d, de), BF16), pltpu.VMEM((d, de), BF16), pltpu.VMEM((de, d), BF16)],
    )
    return pl.pallas_call(
        _expert_kernel,
        grid_spec=grid_spec,
        out_shape=jax.ShapeDtypeStruct((s, d), BF16),
        compiler_params=_params(1),
        name="expert_ffn",
    )(block_expert, n_used, xs, w_gate, w_up, w_down)


def _combine_kernel(x_ref, yg_ref, gt_ref, sh_ref, mod_ref, o_ref, *, gate_idx):
    routed = jnp.zeros(x_ref.shape, F32)
    for j in range(TOP_K):
        routed = routed + gt_ref[:, j:j + 1] * yg_ref[j].astype(F32)
    g = mod_ref[0][gate_idx:gate_idx + 1]
    o_ref[...] = x_ref[...] + g * (routed + sh_ref[...].astype(F32))


def moe_combine(x, yg, gates, shared, mods, gate_idx, n_ctx, tm=128):
    n, d = x.shape
    row = functools.partial(_cond_row, n_ctx_tiles=n_ctx // tm, tiles_per_sample=DEC_SEQ // tm)
    return pl.pallas_call(
        functools.partial(_combine_kernel, gate_idx=gate_idx),
        grid=(n // tm,),
        in_specs=[pl.BlockSpec((tm, d), lambda i: (i, 0)),
                  pl.BlockSpec((TOP_K, tm, d), lambda i: (0, i, 0)),
                  pl.BlockSpec((tm, TOP_K), lambda i: (i, 0)),
                  pl.BlockSpec((tm, d), lambda i: (i, 0)),
                  pl.BlockSpec((1, 6, d), lambda i: (row(i), 0, 0))],
        out_specs=pl.BlockSpec((tm, d), lambda i: (i, 0)),
        out_shape=jax.ShapeDtypeStruct((n, d), F32),
        compiler_params=_params(1),
        name="moe_combine",
    )(x, yg, gates, shared, mods)


def moe_ffn(x, h, logits_t, mods, gate_idx, n_ctx, layer, b_router, w_gate, w_up, w_down, ws_gate, ws_up, ws_down):
    n, d = h.shape
    top_e, gates, rank, counts = route(logits_t, b_router)
    counts = counts.astype(jnp.int32)
    padded = (counts + MOE_ROWS - 1) // MOE_ROWS * MOE_ROWS
    pad_end = jnp.cumsum(padded)
    pad_start = pad_end - padded
    n_slots = n * TOP_K + N_EXPERTS * MOE_ROWS
    nb = n_slots // MOE_ROWS
    expert_ids = jnp.arange(N_EXPERTS, dtype=jnp.int32)
    start_of = jnp.sum(jnp.where(top_e[..., None] == expert_ids, pad_start, 0), axis=-1)
    pos = start_of + rank
    tok = jnp.broadcast_to(jnp.arange(n, dtype=jnp.int32)[None, :], pos.shape)
    slot_tok = (jnp.arange(n_slots, dtype=jnp.int32) % n).at[pos.reshape(-1)].set(tok.reshape(-1), unique_indices=True)
    block_start = jnp.arange(nb, dtype=jnp.int32) * MOE_ROWS
    block_expert = jnp.minimum(jnp.sum(pad_end[None, :] <= block_start[:, None], axis=1), N_EXPERTS - 1).astype(jnp.int32)
    n_used = (pad_end[-1:] // MOE_ROWS).astype(jnp.int32)
    xs = h[slot_tok]
    yb = expert_ffn(xs, block_expert, n_used, w_gate, w_up, w_down, layer)
    shared = expert_ffn(h, jnp.zeros((n // MOE_ROWS,), jnp.int32), jnp.full((1,), n // MOE_ROWS, jnp.int32),
                        ws_gate[:, None], ws_up[:, None], ws_down[:, None], layer)
    yg = yb[pos]
    return moe_combine(x, yg, gates.T, shared, mods, gate_idx, n_ctx)


def _layer_ab(h, n_ctx, bsz_ctx, bsz_lat, p):
    z = project(h, p['w_in'], tn=512)
    s5w = s5_weights(*p['s5_disc'])
    outs = []
    extras = {}
    for name, lo, bsz, length in (('ctx', 0, bsz_ctx, SEQ), ('lat', n_ctx, bsz_lat, DEC_SEQ)):
        zg = z[lo:lo + bsz * length].reshape(bsz, length, 4 * D_HALF)
        u = zg[..., :D_HALF]
        q, k, v = (_heads_first(zg[..., (1 + j) * D_HALF:(2 + j) * D_HALF], bsz, length, NA_HEADS) for j in range(3))
        if name == 'ctx':
            y_a, s5_fin = s5_mixer(u, None, p['s5_glu'], s5w)
            y_b, k_n = context_attention(q, k, v, p['q_g'], p['k_g'])
            extras = {'k': k_n, 'v': v, 's5': s5_fin}
        else:
            y_a, _ = s5_mixer(u, p['s5_state'], p['s5_glu'], s5w)
            y_b = neighbourhood_attention(q, k, v, p['ck'], p['cv'], p['rpb'], p['q_g'], p['k_g'])
        outs.append(jnp.concatenate([y_a, y_b], axis=-1).reshape(bsz * length, 2 * D_HALF))
    return jnp.concatenate(outs, axis=0), extras


def _layer_cd(h, n_ctx, bsz_ctx, bsz_lat, p):
    w_in = p['w_in']
    zh_all = project(h, w_in[:, :3 * D_HALF], tn=512)
    w_rw = jnp.pad(w_in[:, 3 * D_HALF:], ((0, 0), (0, RW_IN_PAD - RW_IN)))
    zr_all = project(h, w_rw, tn=512)
    outs = []
    extras = {}
    for name, lo, bsz, length in (('ctx', 0, bsz_ctx, SEQ), ('lat', n_ctx, bsz_lat, DEC_SEQ)):
        zh = zh_all[lo:lo + bsz * length].reshape(bsz, length, 3 * D_HALF)
        zr = zr_all[lo:lo + bsz * length].reshape(bsz, length, RW_IN_PAD)
        filt = hyena_filter(length, *p['hy_filter'])
        y_c = hyena_mixer(zh, filt, p['hy_short_w'], p['hy_short_b'], p['hy_bias'])
        y_d, rw_fin = rwkv_mixer(zr, None if name == 'ctx' else p['rw_state'], p['rwp'])
        if name == 'ctx':
            extras = {'rw': rw_fin}
        outs.append(jnp.concatenate([y_c, y_d], axis=-1).reshape(bsz * length, 2 * D_HALF))
    return jnp.concatenate(outs, axis=0), extras


def kernel(x_prompt, x_sample, cache_na_k, cache_na_v, state_s5, state_rwkv, c, c_ctx, ada_w, ada_b, norm_mix, norm_ffn, ab_w_in, ab_w_out, s5_lam_re, s5_lam_im, s5_log_step, s5_b_re, s5_b_im, s5_c_re, s5_c_im, s5_d, s5_w_glu, s5_b_glu, na_q_norm, na_k_norm, na_rpb, cd_w_in, cd_w_out, hy_short_w, hy_short_b, hy_w1, hy_b1, hy_freq, hy_w2, hy_b2, hy_w3, hy_decay, hy_bias, rw_mu, rw_w0, rw_w2, rw_a0, rw_a2, rw_g2, rw_k_k, rw_k_a, rw_r_k, rw_ln_g, rw_ln_b, moe_router, moe_router_bias, moe_w_gate, moe_w_up, moe_w_down, moe_ws_gate, moe_ws_up, moe_ws_down):
    bsz_ctx, seq, d = x_prompt.shape
    bsz_lat, dec_seq, _ = x_sample.shape
    assert (seq, dec_seq, d) == (SEQ, DEC_SEQ, D_MODEL) and bsz_lat + 1 <= 8
    depth = ada_w.shape[0]
    n_ctx = bsz_ctx * seq
    x = jnp.concatenate([x_prompt.reshape(n_ctx, d), x_sample.reshape(bsz_lat * dec_seq, d)], axis=0).astype(F32)

    cond = jnp.concatenate([c_ctx[None, :], c, jnp.zeros((8 - 1 - bsz_lat, d), c.dtype)], axis=0).astype(F32)
    mods_all = ada_table(cond, ada_w, ada_b).reshape(depth, 8, 6, d)

    new_k, new_v, new_s5, new_rw = [], [], [], []
    for l in range(depth):
        mods = mods_all[l]
        i = l // 2
        h = modulate(x, norm_mix[l], mods, 0, 1, n_ctx)
        if l % 2 == 0:
            p = {'w_in': ab_w_in[i], 'q_g': na_q_norm[i], 'k_g': na_k_norm[i], 'rpb': na_rpb[i],
                 'ck': cache_na_k[:, i], 'cv': cache_na_v[:, i], 's5_state': state_s5[:, i],
                 's5_disc': (s5_lam_re[i], s5_lam_im[i], s5_log_step[i], s5_b_re[i], s5_b_im[i], s5_c_re[i], s5_c_im[i]),
                 's5_glu': (s5_d[i], s5_w_glu[i], s5_b_glu[i])}
            y, ex = _layer_ab(h, n_ctx, bsz_ctx, bsz_lat, p)
            new_k.append(ex['k'])
            new_v.append(ex['v'])
            new_s5.append(ex['s5'])
            w_out = ab_w_out[i]
        else:
            p = {'w_in': cd_w_in[i], 'hy_filter': (hy_w1[i], hy_b1[i], hy_freq[i], hy_w2[i], hy_b2[i], hy_w3[i], hy_decay[i]),
                 'hy_short_w': hy_short_w[i], 'hy_short_b': hy_short_b[i], 'hy_bias': hy_bias[i],
                 'rw_state': state_rwkv[:, i],
                 'rwp': (rw_mu[i], rw_w0[i], rw_w2[i], rw_a0[i], rw_a2[i], rw_g2[i], rw_k_k[i], rw_k_a[i], rw_r_k[i],
                         rw_ln_g[i], rw_ln_b[i])}
            y, ex = _layer_cd(h, n_ctx, bsz_ctx, bsz_lat, p)
            new_rw.append(ex['rw'])
            w_out = cd_w_out[i]
        x = project_residual(y, w_out, x, mods, 2, n_ctx)
        h, logits_t = modulate(x, norm_ffn[l], mods, 3, 4, n_ctx, w_router_t=moe_router[l].T)
        x = moe_ffn(x, h, logits_t, mods, 5, n_ctx, l, moe_router_bias[l], moe_w_gate, moe_w_up, moe_w_down,
                    moe_ws_gate, moe_ws_up, moe_ws_down)

    y_prompt = x[:n_ctx].reshape(bsz_ctx, seq, d)
    y_sample = x[n_ctx:].reshape(bsz_lat, dec_seq, d)
    return (y_prompt, y_sample, jnp.stack(new_k, axis=1), jnp.stack(new_v, axis=1),
            jnp.stack(new_s5, axis=1), jnp.stack(new_rw, axis=1))
```

```python
import functools
import math

import numpy as np
import jax
import jax.numpy as jnp
from jax import lax
from jax.experimental import pallas as pl
from jax.experimental.pallas import tpu as pltpu

F32 = jnp.float32
BF16 = jnp.bfloat16

D_MODEL = 2048
D_HALF = 1024
SEQ = 256
DEC_SEQ = 2048
GRID_W = 64
S5_GROUP = 16
S5_GROUPS = 64
S5_STATE = 64
S5_LANES = S5_GROUPS * S5_STATE
NA_HEADS = 16
NA_HD = 64
NA_KH = 8
NA_KW = 16
HY_BANDS = 16
RW_HEADS = 16
RW_HD = 64
RW_W_RANK = 64
RW_A_RANK = 64
RW_G_RANK = 128
RW_LORA = 256
RW_IN = 3 * D_HALF + RW_LORA
RW_IN_PAD = 3584
N_EXPERTS = 64
TOP_K = 8
N_ROUTE_GROUPS = 8
TOPK_ROUTE_GROUPS = 4
D_EXPERT = 512
ROUTED_SCALE = 2.5
EPS = 1e-6
GN_EPS = 64e-5
NEG_INF = -1e30

TOK_TILE = 256
MOE_ROWS = 256
VMEM_LIMIT = 56 * 1024 * 1024


def _params(n_axes, vmem=VMEM_LIMIT):
    return pltpu.CompilerParams(dimension_semantics=("arbitrary",) * n_axes, vmem_limit_bytes=vmem)


def _dot(a, b):
    return jnp.dot(a, b, preferred_element_type=F32)


def _dot_nt(a, b):
    return lax.dot_general(a, b, (((1,), (1,)), ((), ())), preferred_element_type=F32)


def _split_bf16(x):
    hi = x.astype(BF16)
    lo = (x - hi.astype(F32)).astype(BF16)
    return hi, lo


def _cond_row(i, n_ctx_tiles, tiles_per_sample):
    return jnp.where(i < n_ctx_tiles, 0, 1 + (i - n_ctx_tiles) // tiles_per_sample)


def _ada_kernel(c_ref, w_ref, b_ref, o_ref):
    c = c_ref[...]
    s = (c * jax.nn.sigmoid(c)).astype(BF16)
    o_ref[0] = _dot(s, w_ref[0].astype(BF16)) + b_ref[0]


def ada_table(cond8, ada_w, ada_b):
    depth, d, n = ada_w.shape
    tn = 1024
    return pl.pallas_call(
        _ada_kernel,
        grid=(depth, n // tn),
        in_specs=[pl.BlockSpec((8, d), lambda l, j: (0, 0)),
                  pl.BlockSpec((1, d, tn), lambda l, j: (l, 0, j)),
                  pl.BlockSpec((1, 1, tn), lambda l, j: (l, 0, j))],
        out_specs=pl.BlockSpec((1, 8, tn), lambda l, j: (l, 0, j)),
        out_shape=jax.ShapeDtypeStruct((depth, 8, n), F32),
        compiler_params=_params(2),
        name="ada_table",
    )(cond8, ada_w, ada_b.reshape(depth, 1, n))


def _modulate_kernel(x_ref, g_ref, mod_ref, h_ref, *, shift_idx, scale_idx):
    x = x_ref[...]
    y = x * lax.rsqrt(jnp.mean(x * x, axis=-1, keepdims=True) + EPS)
    m = mod_ref[0]
    h = y * g_ref[...] * (1.0 + m[scale_idx:scale_idx + 1]) + m[shift_idx:shift_idx + 1]
    h_ref[...] = h.astype(BF16)


def _modulate_router_kernel(x_ref, g_ref, mod_ref, wr_ref, h_ref, lg_ref, *, shift_idx, scale_idx):
    x = x_ref[...]
    y = x * lax.rsqrt(jnp.mean(x * x, axis=-1, keepdims=True) + EPS)
    m = mod_ref[0]
    h = y * g_ref[...] * (1.0 + m[scale_idx:scale_idx + 1]) + m[shift_idx:shift_idx + 1]
    h_hi, h_lo = _split_bf16(h)
    h_ref[...] = h_hi
    w_hi, w_lo = _split_bf16(wr_ref[...])
    lg_ref[...] = _dot_nt(w_hi, h_hi) + _dot_nt(w_hi, h_lo) + _dot_nt(w_lo, h_hi)


def modulate(x, gain, mods, shift_idx, scale_idx, n_ctx, w_router_t=None):
    n, d = x.shape
    tm = TOK_TILE
    row = functools.partial(_cond_row, n_ctx_tiles=n_ctx // tm, tiles_per_sample=DEC_SEQ // tm)
    in_specs = [pl.BlockSpec((tm, d), lambda i: (i, 0)),
                pl.BlockSpec((1, d), lambda i: (0, 0)),
                pl.BlockSpec((1, 6, d), lambda i: (row(i), 0, 0))]
    if w_router_t is None:
        return pl.pallas_call(
            functools.partial(_modulate_kernel, shift_idx=shift_idx, scale_idx=scale_idx),
            grid=(n // tm,),
            in_specs=in_specs,
            out_specs=pl.BlockSpec((tm, d), lambda i: (i, 0)),
            out_shape=jax.ShapeDtypeStruct((n, d), BF16),
            compiler_params=_params(1),
            name="modulate",
        )(x, gain.reshape(1, d), mods)
    e = w_router_t.shape[0]
    return pl.pallas_call(
        functools.partial(_modulate_router_kernel, shift_idx=shift_idx, scale_idx=scale_idx),
        grid=(n // tm,),
        in_specs=in_specs + [pl.BlockSpec((e, d), lambda i: (0, 0))],
        out_specs=[pl.BlockSpec((tm, d), lambda i: (i, 0)),
                   pl.BlockSpec((e, tm), lambda i: (0, i))],
        out_shape=[jax.ShapeDtypeStruct((n, d), BF16), jax.ShapeDtypeStruct((e, n), F32)],
        compiler_params=_params(1),
        name="modulate_router",
    )(x, gain.reshape(1, d), mods, w_router_t)


def _proj_kernel(x_ref, w_ref, o_ref, wbf_ref):
    @pl.when(pl.program_id(1) == 0)
    def _():
        wbf_ref[...] = w_ref[...].astype(BF16)

    o_ref[...] = _dot(x_ref[...], wbf_ref[...]).astype(o_ref.dtype)


def project(x, w, tn, tm=512, out_dtype=F32):
    m, k = x.shape
    n = w.shape[1]
    return pl.pallas_call(
        _proj_kernel,
        grid=(n // tn, m // tm),
        in_specs=[pl.BlockSpec((tm, k), lambda j, i: (i, 0)),
                  pl.BlockSpec((k, tn), lambda j, i: (0, j))],
        out_specs=pl.BlockSpec((tm, tn), lambda j, i: (i, j)),
        out_shape=jax.ShapeDtypeStruct((m, n), out_dtype),
        scratch_shapes=[pltpu.VMEM((k, tn), BF16)],
        compiler_params=_params(2),
        name="project",
    )(x, w)


def _proj_residual_kernel(y_ref, w_ref, x_ref, mod_ref, o_ref, wbf_ref, *, gate_idx):
    @pl.when(pl.program_id(1) == 0)
    def _():
        wbf_ref[...] = w_ref[...].astype(BF16)

    g = mod_ref[0][gate_idx:gate_idx + 1]
    o_ref[...] = x_ref[...] + g * _dot(y_ref[...], wbf_ref[...])


def project_residual(y, w, x, mods, gate_idx, n_ctx, tn=512, tm=512):
    m, k = y.shape
    n = w.shape[1]
    row = functools.partial(_cond_row, n_ctx_tiles=n_ctx // tm, tiles_per_sample=DEC_SEQ // tm)
    return pl.pallas_call(
        functools.partial(_proj_residual_kernel, gate_idx=gate_idx),
        grid=(n // tn, m // tm),
        in_specs=[pl.BlockSpec((tm, k), lambda j, i: (i, 0)),
                  pl.BlockSpec((k, tn), lambda j, i: (0, j)),
                  pl.BlockSpec((tm, tn), lambda j, i: (i, j)),
                  pl.BlockSpec((1, 6, tn), lambda j, i: (row(i), 0, j))],
        out_specs=pl.BlockSpec((tm, tn), lambda j, i: (i, j)),
        out_shape=jax.ShapeDtypeStruct((m, n), F32),
        scratch_shapes=[pltpu.VMEM((k, tn), BF16)],
        compiler_params=_params(2),
        name="project_residual",
    )(y, w, x, mods)


S5_CHUNK = 64
S5_ROWS = 8
S5_BLK = 8
S5_SCAN_LANES = 1024


def _s5_kernel(u_ref, wbr_ref, wbi_ref, wcr_ref, wci_ref, lam_ref, x0_ref, y_ref, fin_ref,
               bur_ref, bui_ref, st_ref, *, n_chunks):
    d = pl.program_id(0)
    c = pl.program_id(2)
    tc = S5_CHUNK
    cin = S5_BLK * S5_GROUP
    cst = S5_BLK * S5_STATE

    @pl.when(c == 0)
    def _():
        st_ref[...] = x0_ref[0]

    u = u_ref[...].reshape(tc * S5_ROWS, D_HALF).astype(BF16)
    for k in range(S5_GROUPS // S5_BLK):
        uk = u[:, k * cin:(k + 1) * cin]
        bur_ref[:, k * cst:(k + 1) * cst] = _dot(uk, wbr_ref[0, k])
        bui_ref[:, k * cst:(k + 1) * cst] = _dot(uk, wbi_ref[0, k])

    for j in range(S5_LANES // S5_SCAN_LANES):
        sl = slice(j * S5_SCAN_LANES, (j + 1) * S5_SCAN_LANES)
        lr = jnp.broadcast_to(lam_ref[0, 0:1, sl], (S5_ROWS, S5_SCAN_LANES))
        li = jnp.broadcast_to(lam_ref[0, 1:2, sl], (S5_ROWS, S5_SCAN_LANES))

        def step(i, carry, sl=sl, lr=lr, li=li):
            sr, si = carry
            t = jnp.where(d == 0, i, tc - 1 - i)
            row = pl.multiple_of(t * S5_ROWS, S5_ROWS)
            nr = lr * sr - li * si + bur_ref[pl.ds(row, S5_ROWS), sl]
            ni = lr * si + li * sr + bui_ref[pl.ds(row, S5_ROWS), sl]
            bur_ref[pl.ds(row, S5_ROWS), sl] = nr
            bui_ref[pl.ds(row, S5_ROWS), sl] = ni
            return nr, ni

        sr, si = lax.fori_loop(0, tc, step, (st_ref[0, :, sl], st_ref[1, :, sl]), unroll=4)
        st_ref[0, :, sl] = sr
        st_ref[1, :, sl] = si

    xr = bur_ref[...].astype(BF16)
    xi = bui_ref[...].astype(BF16)
    for k in range(S5_GROUPS // S5_BLK):
        yk = _dot(xr[:, k * cst:(k + 1) * cst], wcr_ref[0, k]) + _dot(xi[:, k * cst:(k + 1) * cst], wci_ref[0, k])
        y_ref[0, :, :, k * cin:(k + 1) * cin] = yk.reshape(tc, S5_ROWS, cin)

    @pl.when(c == n_chunks - 1)
    def _():
        fin_ref[0] = st_ref[...]


def s5_scan(u_t, x0, wbr, wbi, wcr, wci, lam):
    length, bsz, _ = u_t.shape
    n_chunks = length // S5_CHUNK
    nblk = S5_GROUPS // S5_BLK
    cin = S5_BLK * S5_GROUP
    cst = S5_BLK * S5_STATE

    def chunk(d, c):
        return jnp.where(d == 0, c, n_chunks - 1 - c)

    return pl.pallas_call(
        functools.partial(_s5_kernel, n_chunks=n_chunks),
        grid=(2, bsz // S5_ROWS, n_chunks),
        in_specs=[pl.BlockSpec((S5_CHUNK, S5_ROWS, D_HALF), lambda d, b, c: (chunk(d, c), b, 0)),
                  pl.BlockSpec((1, nblk, cin, cst), lambda d, b, c: (d, 0, 0, 0)),
                  pl.BlockSpec((1, nblk, cin, cst), lambda d, b, c: (d, 0, 0, 0)),
                  pl.BlockSpec((1, nblk, cst, cin), lambda d, b, c: (d, 0, 0, 0)),
                  pl.BlockSpec((1, nblk, cst, cin), lambda d, b, c: (d, 0, 0, 0)),
                  pl.BlockSpec((1, 2, S5_LANES), lambda d, b, c: (d, 0, 0)),
                  pl.BlockSpec((1, 2, S5_ROWS, S5_LANES), lambda d, b, c: (d, 0, b, 0))],
        out_specs=[pl.BlockSpec((1, S5_CHUNK, S5_ROWS, D_HALF), lambda d, b, c: (d, chunk(d, c), b, 0)),
                   pl.BlockSpec((1, 2, S5_ROWS, S5_LANES), lambda d, b, c: (d, 0, b, 0))],
        out_shape=[jax.ShapeDtypeStruct((2, length, bsz, D_HALF), F32),
                   jax.ShapeDtypeStruct((2, 2, bsz, S5_LANES), F32)],
        scratch_shapes=[pltpu.VMEM((S5_CHUNK * S5_ROWS, S5_LANES), F32),
                        pltpu.VMEM((S5_CHUNK * S5_ROWS, S5_LANES), F32),
                        pltpu.VMEM((2, S5_ROWS, S5_LANES), F32)],
        compiler_params=_params(3),
        name="s5_scan",
    )(u_t, wbr, wbi, wcr, wci, lam, x0)


def s5_weights(lam_re, lam_im, log_step, b_re, b_im, c_re, c_im):
    lam = lax.complex(lam_re.astype(F32), lam_im.astype(F32))
    lam_bar = jnp.exp(lam * jnp.exp(log_step.astype(F32))[..., None])
    b_bar = ((lam_bar - 1.0) / lam)[..., None] * lax.complex(b_re.astype(F32), b_im.astype(F32))
    nblk = S5_GROUPS // S5_BLK
    eye = jnp.eye(S5_BLK, dtype=F32)

    def embed_in(b):
        b = b.reshape(2, nblk, S5_BLK, S5_STATE, S5_GROUP)
        w = jnp.einsum('dkgpc,gh->dkgchp', b, eye)
        return w.reshape(2, nblk, S5_BLK * S5_GROUP, S5_BLK * S5_STATE).astype(BF16)

    def embed_out(cm):
        cm = cm.reshape(2, nblk, S5_BLK, S5_GROUP, S5_STATE)
        w = jnp.einsum('dkgcp,gh->dkgphc', cm, eye)
        return w.reshape(2, nblk, S5_BLK * S5_STATE, S5_BLK * S5_GROUP).astype(BF16)

    lam_rows = jnp.stack([lam_bar.real.reshape(2, S5_LANES), lam_bar.imag.reshape(2, S5_LANES)], axis=1)
    return (embed_in(b_bar.real), embed_in(b_bar.imag),
            embed_out(c_re.astype(F32)), embed_out(-c_im.astype(F32)), lam_rows)


def _glu_kernel(yf_ref, yb_ref, u_ref, d_ref, w_ref, b_ref, o_ref, wbf_ref):
    @pl.when(pl.program_id(0) == 0)
    def _():
        wbf_ref[...] = w_ref[...].astype(BF16)

    y = jax.nn.gelu(u_ref[...] * d_ref[...] + yf_ref[0] + yb_ref[0])
    o_ref[...] = (y * jax.nn.sigmoid(_dot(y.astype(BF16), wbf_ref[...]) + b_ref[...])).astype(BF16)


def s5_glu(y_dirs, u, d_skip, w_glu, b_glu, tm=512):
    m, n = u.shape
    return pl.pallas_call(
        _glu_kernel,
        grid=(m // tm,),
        in_specs=[pl.BlockSpec((1, tm, n), lambda i: (0, i, 0)),
                  pl.BlockSpec((1, tm, n), lambda i: (1, i, 0)),
                  pl.BlockSpec((tm, n), lambda i: (i, 0)),
                  pl.BlockSpec((1, n), lambda i: (0, 0)),
                  pl.BlockSpec((n, n), lambda i: (0, 0)),
                  pl.BlockSpec((1, n), lambda i: (0, 0))],
        out_specs=pl.BlockSpec((tm, n), lambda i: (i, 0)),
        out_shape=jax.ShapeDtypeStruct((m, n), BF16),
        scratch_shapes=[pltpu.VMEM((n, n), BF16)],
        compiler_params=_params(1),
        name="s5_glu",
    )(y_dirs, y_dirs, u, d_skip.reshape(1, n), w_glu, b_glu.reshape(1, n))


def s5_mixer(u, x0, s5p, weights):
    bsz, length, _ = u.shape
    bp = -(-bsz // S5_ROWS) * S5_ROWS
    wbr, wbi, wcr, wci, lam = weights
    d_skip, w_glu, b_glu = s5p
    u_t = jnp.swapaxes(u, 0, 1)
    if x0 is None:
        x0_t = jnp.zeros((2, 2, bp, S5_LANES), F32)
    else:
        x0_t = x0.astype(F32).reshape(bsz, 2, 2, S5_LANES).transpose(1, 2, 0, 3)
    if bp != bsz:
        u_t = jnp.pad(u_t, ((0, 0), (0, bp - bsz), (0, 0)))
        x0_t = jnp.pad(x0_t, ((0, 0), (0, 0), (0, bp - bsz), (0, 0)))
    y_dirs, fin = s5_scan(u_t, x0_t, wbr, wbi, wcr, wci, lam)
    y = s5_glu(y_dirs.reshape(2, length * bp, D_HALF), u_t.reshape(length * bp, D_HALF), d_skip, w_glu, b_glu)
    y = jnp.swapaxes(y.reshape(length, bp, D_HALF)[:, :bsz], 0, 1)
    fin = fin[:, :, :bsz].transpose(2, 0, 1, 3).reshape(bsz, 2, 2, S5_GROUPS, S5_STATE)
    return y, fin


def _head_rms(x, g):
    return x * lax.rsqrt(jnp.mean(x * x, axis=-1, keepdims=True) + EPS) * g


def _ctx_attn_kernel(q_ref, k_ref, v_ref, qg_ref, kg_ref, o_ref, kn_ref):
    scale = NA_HD ** -0.5
    for h in range(NA_HEADS):
        q = _head_rms(q_ref[0, h], qg_ref[...])
        k = _head_rms(k_ref[0, h], kg_ref[...])
        kn_ref[0, h] = k
        s = _dot_nt(q.astype(BF16), k.astype(BF16)) * scale
        p = jnp.exp(s - jnp.max(s, axis=-1, keepdims=True))
        l = jnp.sum(p, axis=-1, keepdims=True)
        o = _dot(p.astype(BF16), v_ref[0, h].astype(BF16)) / l
        o_ref[0, :, h * NA_HD:(h + 1) * NA_HD] = o.astype(BF16)


def context_attention(q, k, v, q_g, k_g):
    bsz, nh, t, hd = q.shape
    blk = pl.BlockSpec((1, nh, t, hd), lambda b: (b, 0, 0, 0))
    vec = pl.BlockSpec((1, hd), lambda b: (0, 0))
    return pl.pallas_call(
        _ctx_attn_kernel,
        grid=(bsz,),
        in_specs=[blk, blk, blk, vec, vec],
        out_specs=[pl.BlockSpec((1, t, nh * hd), lambda b: (b, 0, 0)), blk],
        out_shape=[jax.ShapeDtypeStruct((bsz, t, nh * hd), BF16), jax.ShapeDtypeStruct((bsz, nh, t, hd), F32)],
        compiler_params=_params(1),
        name="context_attention",
    )(q, k, v, q_g.reshape(1, hd), k_g.reshape(1, hd))


NA_PAIR = 2


def _na_kernel(q_ref, k_ref, v_ref, ck_ref, cv_ref, bias_ref, qg_ref, kg_ref, o_ref, kn_ref, vn_ref, *, rows):
    scale = NA_HD ** -0.5
    win = NA_KH * GRID_W
    for hh in range(NA_PAIR):
        kn_ref[...] = _head_rms(k_ref[0, hh], kg_ref[...]).astype(BF16)
        vn_ref[...] = v_ref[0, hh].astype(BF16)
        ck = ck_ref[0, hh].astype(BF16)
        cv = cv_ref[0, hh].astype(BF16)
        for r in range(rows):
            rs = min(max(r - NA_KH // 2, 0), rows - NA_KH)
            cls = rs - r + NA_KH - 1
            q = _head_rms(q_ref[0, hh, r * GRID_W:(r + 1) * GRID_W, :], qg_ref[...]).astype(BF16)
            kw = kn_ref[rs * GRID_W:rs * GRID_W + win, :]
            vw = vn_ref[rs * GRID_W:rs * GRID_W + win, :]
            s_w = _dot_nt(q, kw) * scale + bias_ref[hh, cls]
            s_c = _dot_nt(q, ck) * scale
            m = jnp.maximum(jnp.max(s_w, axis=-1, keepdims=True), jnp.max(s_c, axis=-1, keepdims=True))
            p_w = jnp.exp(s_w - m)
            p_c = jnp.exp(s_c - m)
            l = jnp.sum(p_w, axis=-1, keepdims=True) + jnp.sum(p_c, axis=-1, keepdims=True)
            o = (_dot(p_w.astype(BF16), vw) + _dot(p_c.astype(BF16), cv)) / l
            o_ref[0, r * GRID_W:(r + 1) * GRID_W, hh * NA_HD:(hh + 1) * NA_HD] = o.astype(BF16)


def na_bias_table(rpb, rows):
    qc = np.arange(GRID_W)
    kc = np.arange(GRID_W)
    c0 = np.clip(qc - NA_KW // 2, 0, GRID_W - NA_KW)
    ok = (kc[None, :] >= c0[:, None]) & (kc[None, :] < c0[:, None] + NA_KW)
    dc = np.clip(kc[None, :] - qc[:, None] + NA_KW - 1, 0, 2 * NA_KW - 2)
    dr = np.arange(NA_KH)[:, None] + np.arange(NA_KH)[None, :]
    b = rpb.astype(F32)[:, dr][:, :, :, dc]
    b = jnp.where(ok[None, None, None], b, NEG_INF)
    return b.transpose(0, 1, 3, 2, 4).reshape(rpb.shape[0], NA_KH, GRID_W, NA_KH * GRID_W)


def neighbourhood_attention(q, k, v, ck, cv, rpb, q_g, k_g):
    bsz, nh, length, hd = q.shape
    past = ck.shape[2]
    rows = length // GRID_W
    bias = na_bias_table(rpb, rows)
    blk = pl.BlockSpec((1, NA_PAIR, length, hd), lambda b, h: (b, h, 0, 0))
    cblk = pl.BlockSpec((1, NA_PAIR, past, hd), lambda b, h: (b, h, 0, 0))
    vec = pl.BlockSpec((1, hd), lambda b, h: (0, 0))
    return pl.pallas_call(
        functools.partial(_na_kernel, rows=rows),
        grid=(bsz, nh // NA_PAIR),
        in_specs=[blk, blk, blk, cblk, cblk,
                  pl.BlockSpec((NA_PAIR, NA_KH, GRID_W, NA_KH * GRID_W), lambda b, h: (h, 0, 0, 0)),
                  vec, vec],
        out_specs=pl.BlockSpec((1, length, NA_PAIR * hd), lambda b, h: (b, 0, h)),
        out_shape=jax.ShapeDtypeStruct((bsz, length, nh * hd), BF16),
        scratch_shapes=[pltpu.VMEM((length, hd), BF16), pltpu.VMEM((length, hd), BF16)],
        compiler_params=_params(2),
        name="neighbourhood_attention",
    )(q, k, v, ck, cv, bias, q_g.reshape(1, hd), k_g.reshape(1, hd))


def _heads_first(t, bsz, length, nh):
    return t.reshape(bsz, length, nh, -1).transpose(0, 2, 1, 3)


HY_EMB_PAD = 64


def _dot3(a, b):
    a_hi, a_lo = _split_bf16(a)
    b_hi, b_lo = _split_bf16(b)
    return _dot(a_hi, b_hi) + _dot(a_hi, b_lo) + _dot(a_lo, b_hi)


def _hyena_filter_kernel(z_ref, w1_ref, b1_ref, f_ref, w2_ref, b2_ref, w3_ref, dec_ref, o_ref):
    length = z_ref.shape[0]
    h = jnp.sin(f_ref[0:1, :] * (_dot3(z_ref[...], w1_ref[...]) + b1_ref[...]))
    h = jnp.sin(f_ref[1:2, :] * (_dot3(h, w2_ref[...]) + b2_ref[...]))
    filt = _dot3(h, w3_ref[...])
    t = lax.broadcasted_iota(jnp.int32, filt.shape, 0).astype(F32)
    offset = jnp.abs(t - float(length // 2)) / float(length)
    filt = filt * jnp.exp(-offset * jnp.abs(dec_ref[...]))
    o_ref[...] = filt / (jnp.sum(jnp.abs(filt), axis=0, keepdims=True) + EPS)


def _hyena_embedding(length):
    t = np.arange(length, dtype=np.float32) / np.float32(length)
    bands = np.linspace(1e-4, HY_BANDS - 1, HY_BANDS, dtype=np.float32)
    ang = (2 * math.pi * t[:, None] * bands[None, :]).astype(np.float32)
    z = np.concatenate([t[:, None], np.cos(ang), np.sin(ang)], axis=-1).astype(np.float32)
    return np.pad(z, ((0, 0), (0, HY_EMB_PAD - z.shape[1])))


def hyena_filter(length, w1, b1, freq, w2, b2, w3, decay):
    z = jnp.asarray(_hyena_embedding(length))
    w1p = jnp.pad(w1.astype(F32), ((0, HY_EMB_PAD - w1.shape[0]), (0, 0)))
    n = w3.shape[1]
    return pl.pallas_call(
        _hyena_filter_kernel,
        out_shape=jax.ShapeDtypeStruct((length, n), F32),
        compiler_params=_params(0),
        name="hyena_filter",
    )(z, w1p, b1.reshape(1, -1), freq, w2, b2.reshape(1, -1), w3, decay.reshape(1, n))


def _dft_tile(length):
    return min(length, 512)


def _dft_matrices(length):
    n = 2 * length
    tf = _dft_tile(length)
    f = np.arange(length)
    s = np.arange(length)
    ang = 2 * np.pi * ((f[:, None] * s[None, :]) % n) / n
    fc = np.cos(ang)
    fs = -np.sin(ang)
    fs[0] = np.where(s % 2 == 0, 1.0, -1.0)
    nt = np.arange(length) + length // 2
    ang_i = 2 * np.pi * ((nt[:, None] * f[None, :]) % n) / n
    wf = np.where(f == 0, 1.0, 2.0)[None, :]
    gc = wf * np.cos(ang_i) / n
    gs = -wf * np.sin(ang_i) / n
    gs[:, 0] = np.where(nt % 2 == 0, 1.0, -1.0) / n
    n_tiles = length // tf
    fwd = np.concatenate([fc.reshape(n_tiles, tf, length), fs.reshape(n_tiles, tf, length)], axis=1)
    inv = np.concatenate([gc.reshape(length, n_tiles, tf), gs.reshape(length, n_tiles, tf)], axis=2)
    return (jnp.asarray(fwd.reshape(n_tiles * 2 * tf, length), BF16),
            jnp.asarray(inv.transpose(1, 0, 2), BF16))


def _shift_rows(x, up):
    n = x.shape[0]
    row = lax.broadcasted_iota(jnp.int32, x.shape, 0)
    if up:
        return jnp.where(row == n - 1, 0.0, pltpu.roll(x, n - 1, 0))
    return jnp.where(row == 0, 0.0, pltpu.roll(x, 1, 0))


def _conv3(x, w, b):
    return _shift_rows(x, False) * w[0:1] + x * w[1:2] + _shift_rows(x, True) * w[2:3] + b


def _hyena_conv_kernel(x0_ref, x1_ref, v_ref, w0_ref, w1_ref, wv_ref, b0_ref, b1_ref, bv_ref, bias_ref,
                       f_ref, g_ref, hf_ref, o_ref, gated_ref, gbf_ref, acc_ref, *, n_tiles, tf):
    j = pl.program_id(2)

    @pl.when(j == 0)
    def _():
        gated = _conv3(x1_ref[0], w1_ref[...], b1_ref[...]) * _conv3(v_ref[0], wv_ref[...], bv_ref[...])
        gated_ref[...] = gated
        gbf_ref[...] = gated.astype(BF16)
        acc_ref[...] = jnp.zeros_like(acc_ref)

    u = _dot(f_ref[...], gbf_ref[...])
    ur, ui = u[:tf], u[tf:]
    hr, hi = hf_ref[:tf, :], hf_ref[tf:, :]
    packed = (lax.broadcasted_iota(jnp.int32, ur.shape, 0) == 0) & (j == 0)
    yr = ur * hr - jnp.where(packed, 0.0, ui * hi)
    yi = jnp.where(packed, ui * hi, ur * hi + ui * hr)
    y = jnp.concatenate([yr, yi], axis=0).astype(BF16)
    acc_ref[...] += _dot(g_ref[0], y)

    @pl.when(j == n_tiles - 1)
    def _():
        gated = gated_ref[...]
        y = acc_ref[...] + gated * bias_ref[...]
        o_ref[0] = (_conv3(x0_ref[0], w0_ref[...], b0_ref[...]) * y).astype(BF16)


def hyena_mixer(zh, filt, short_w, short_b, bias):
    bsz, length, _ = zh.shape
    fwd, inv = _dft_matrices(length)
    tf = _dft_tile(length)
    n_tiles = length // tf
    hf = project(fwd, filt, tn=512, tm=min(512, 2 * length))
    cn = 256 if length > 512 else 512
    nct = D_HALF // cn

    def zblk(k):
        return pl.BlockSpec((1, length, cn), lambda b, c, j, k=k: (b, 0, k * nct + c))

    def wblk(k):
        return pl.BlockSpec((3, cn), lambda b, c, j, k=k: (0, k * nct + c))

    def bblk(k):
        return pl.BlockSpec((1, cn), lambda b, c, j, k=k: (0, k * nct + c))

    sb = short_b.reshape(1, -1)
    return pl.pallas_call(
        functools.partial(_hyena_conv_kernel, n_tiles=n_tiles, tf=tf),
        grid=(bsz, nct, n_tiles),
        in_specs=[zblk(0), zblk(1), zblk(2), wblk(0), wblk(1), wblk(2), bblk(0), bblk(1), bblk(2),
                  pl.BlockSpec((1, cn), lambda b, c, j: (0, c)),
                  pl.BlockSpec((2 * tf, length), lambda b, c, j: (j, 0)),
                  pl.BlockSpec((1, length, 2 * tf), lambda b, c, j: (j, 0, 0)),
                  pl.BlockSpec((2 * tf, cn), lambda b, c, j: (j, c))],
        out_specs=pl.BlockSpec((1, length, cn), lambda b, c, j: (b, 0, c)),
        out_shape=jax.ShapeDtypeStruct((bsz, length, D_HALF), BF16),
        scratch_shapes=[pltpu.VMEM((length, cn), F32), pltpu.VMEM((length, cn), BF16), pltpu.VMEM((length, cn), F32)],
        compiler_params=_params(3),
        name="hyena_conv",
    )(zh, zh, zh, short_w, short_w, short_w, sb, sb, sb, bias.reshape(1, -1), fwd, inv, hf)


RW_TL = 128
RW_PRE_TL = 128
RW_NB = 4
RW_PAIRS = RW_HEADS // 2
RW_YBLK = 64
RW_GROUP = 8
LANES = 128


def _block_ones():
    idx = np.arange(LANES) // RW_HD
    return (idx[:, None] == idx[None, :]).astype(np.float32)


def _segment_ones():
    return jnp.asarray(np.concatenate([_block_ones(), _block_ones()], axis=0), BF16)


def _segment_ones_pair():
    return jnp.asarray(np.kron(np.eye(2, dtype=np.float32), _block_ones()), BF16)


def _hi_lo(x):
    hi, lo = _split_bf16(x)
    return jnp.concatenate([hi, lo], axis=1)


def _head_sums(x, seg_ref):
    tiles = [_dot(_hi_lo(x[:, j * LANES:(j + 1) * LANES]), seg_ref[...]) for j in range(x.shape[1] // LANES)]
    return jnp.concatenate(tiles, axis=1)


def _rwkv_pre_kernel(z_ref, zp_ref, zn_ref, mu_ref, wl_ref, w0_ref, a0_ref, kk_ref, ka_ref, seg_ref,
                     r_ref, k_ref, v_ref, g_ref, an_ref, w0o_ref, kd0_ref, b0_ref, w1o_ref, kd1_ref, b1_ref,
                     *, n_tiles):
    i = pl.program_id(1)
    z = z_ref[0]
    row = lax.broadcasted_iota(jnp.int32, z.shape, 0)
    prev = jnp.where(i > 0, zp_ref[0, 7:8, :], 0.0)
    nxt = jnp.where(i < n_tiles - 1, zn_ref[0, 0:1, :], 0.0)
    zm1 = jnp.where(row == 0, prev, pltpu.roll(z, 1, 0))
    zp1 = jnp.where(row == z.shape[0] - 1, nxt, pltpu.roll(z, z.shape[0] - 1, 0))
    x = z + (0.5 * (zm1 + zp1) - z) * mu_ref[...]
    r = x[:, 0:D_HALF]
    k = x[:, D_HALF:2 * D_HALF]
    v = x[:, 2 * D_HALF:3 * D_HALF]
    lo = x[:, 3 * D_HALF:3 * D_HALF + RW_LORA]
    lane = lax.broadcasted_iota(jnp.int32, lo.shape, 1)
    act = jnp.where(lane < RW_W_RANK, jnp.tanh(lo),
                    jnp.where(lane < RW_W_RANK + RW_A_RANK, lo, jax.nn.sigmoid(lo)))
    up = _dot(act.astype(BF16), wl_ref[...])
    g = up[:, 4 * D_HALF:5 * D_HALF]
    kk = k * kk_ref[...]
    kk = kk * lax.rsqrt(_head_sums(kk * kk, seg_ref) + EPS)

    def put(ref, val):
        ref[0] = val

    put(r_ref, r)
    put(k_ref, k)
    put(v_ref, v)
    put(g_ref, g)
    put(an_ref, -kk)
    for d, (wo, kdo, bo) in enumerate(((w0o_ref, kd0_ref, b0_ref), (w1o_ref, kd1_ref, b1_ref))):
        logw = -jax.nn.softplus(-(w0_ref[d:d + 1, :] + up[:, d * D_HALF:(d + 1) * D_HALF])) - 0.5
        a = jax.nn.sigmoid(a0_ref[d:d + 1, :] + up[:, (2 + d) * D_HALF:(3 + d) * D_HALF])
        put(wo, jnp.exp(-jnp.exp(logw)))
        put(kdo, k * (1.0 + (a - 1.0) * ka_ref[...]))
        put(bo, kk * a)


def rwkv_pre(zr, mu, w_lora, w0, a0, k_k, k_a):
    bsz, length, width = zr.shape
    tl = RW_PRE_TL
    n_tiles = length // tl
    vec = lambda n: pl.BlockSpec((n, D_HALF), lambda b, i: (0, 0))
    out_blk = pl.BlockSpec((1, tl, D_HALF), lambda b, i: (b, i, 0))
    out_sds = jax.ShapeDtypeStruct((bsz, length, D_HALF), F32)
    return pl.pallas_call(
        functools.partial(_rwkv_pre_kernel, n_tiles=n_tiles),
        grid=(bsz, n_tiles),
        in_specs=[pl.BlockSpec((1, tl, width), lambda b, i: (b, i, 0)),
                  pl.BlockSpec((1, 8, width), lambda b, i: (b, jnp.maximum(i * (tl // 8) - 1, 0), 0)),
                  pl.BlockSpec((1, 8, width), lambda b, i: (b, jnp.minimum((i + 1) * (tl // 8), length // 8 - 1), 0)),
                  pl.BlockSpec((1, width), lambda b, i: (0, 0)),
                  pl.BlockSpec((RW_LORA, 5 * D_HALF), lambda b, i: (0, 0)),
                  vec(2), vec(2), vec(1), vec(1),
                  pl.BlockSpec((2 * LANES, LANES), lambda b, i: (0, 0))],
        out_specs=[out_blk] * 11,
        out_shape=[out_sds] * 11,
        compiler_params=_params(2),
        name="rwkv_pre",
    )(zr, zr, zr, mu, w_lora, w0, a0, k_k.reshape(1, -1), k_a.reshape(1, -1), _segment_ones())


def _rwkv_scan_kernel(r_ref, w_ref, k_ref, a_ref, b_ref, v_ref, s0_ref, seg_ref, seg2_ref, y_ref, fin_ref,
                      st_ref, lhs_ref, vlhs_ref, ylhs_ref, yacc_ref, *, reverse, n_chunks):
    c = pl.program_id(1)

    @pl.when(c == 0)
    def _():
        st_ref[...] = s0_ref[...]

    lane = lax.broadcasted_iota(jnp.int32, (RW_HD, LANES), 1)
    lane_in_head = lane % RW_HD
    diag = jnp.where(lane_in_head == lax.broadcasted_iota(jnp.int32, (RW_HD, LANES), 0), 1.0, 0.0)
    nt = RW_PAIRS
    chains = range(RW_NB)
    n_blocks = RW_TL // RW_YBLK

    def row(ref, cn, t, p):
        return ref[cn, pl.ds(t, 1), :][:, p * LANES:(p + 1) * LANES]

    def tile(sums, q):
        return sums[q * RW_HD:(q + 1) * RW_HD]

    def two_sums(tiles_ref, cn, lo, hi):
        return _dot(tiles_ref[cn, lo:hi].reshape((hi - lo) * RW_HD, 2 * LANES), seg2_ref[...])

    def step(t, u, vcol):
        for cn in chains:
            for p in range(nt):
                lhs_ref[cn, p] = _hi_lo(st_ref[cn, p] * row(a_ref, cn, t, p))
        sa = [_dot(lhs_ref[cn].reshape(nt * RW_HD, 2 * LANES), seg_ref[...]) for cn in chains]
        for cn in chains:
            for p in range(nt):
                s = (st_ref[cn, p] * row(w_ref, cn, t, p) + tile(sa[cn], p) * row(b_ref, cn, t, p)
                     + vcol[cn][p] * row(k_ref, cn, t, p))
                st_ref[cn, p] = s
                sr = (s * row(r_ref, cn, t, p)).astype(BF16)
                ylhs_ref[cn, p // 2, :, (p % 2) * LANES:(p % 2 + 1) * LANES] = sr
        ys = [two_sums(ylhs_ref, cn, 0, nt // 2) for cn in chains]
        hit = lane_in_head == t % RW_YBLK
        for cn in chains:
            for p in range(nt):
                y = tile(ys[cn], p // 2)[:, (p % 2) * LANES:(p % 2 + 1) * LANES]
                yacc_ref[cn, p] = jnp.where(hit, y, yacc_ref[cn, p])

    def step_pair(t0, t1):
        for cn in chains:
            for p in range(nt):
                vlhs_ref[cn, p] = jnp.concatenate([(diag * row(v_ref, cn, t0, p)).astype(BF16),
                                                   (diag * row(v_ref, cn, t1, p)).astype(BF16)], axis=1)
        vs = [two_sums(vlhs_ref, cn, 0, nt) for cn in chains]
        for u, t in enumerate((t0, t1)):
            step(t, u, [[tile(vs[cn], p)[:, u * LANES:(u + 1) * LANES] for p in range(nt)] for cn in chains])

    def step_group(g, blk):
        i0 = blk * RW_YBLK + g * RW_GROUP
        base = pl.multiple_of((RW_TL - RW_GROUP - i0) if reverse else i0, RW_GROUP)
        order = range(RW_GROUP - 1, -1, -1) if reverse else range(RW_GROUP)
        ts = [base + j for j in order]
        for u in range(0, RW_GROUP, 2):
            step_pair(ts[u], ts[u + 1])
        return blk

    def block(bi, carry):
        yacc_ref[...] = jnp.zeros_like(yacc_ref)
        lax.fori_loop(0, RW_YBLK // RW_GROUP, step_group, bi)
        tb = (n_blocks - 1 - bi) if reverse else bi
        rows = pl.ds(pl.multiple_of(tb * RW_YBLK, RW_YBLK), RW_YBLK)
        for cn in chains:
            for p in range(nt):
                yt = yacc_ref[cn, p].T
                for hh in range(2):
                    h = 2 * p + hh
                    y_ref[cn, rows, h * RW_HD:(h + 1) * RW_HD] = yt[hh * RW_YBLK:(hh + 1) * RW_YBLK]
        return carry

    lax.fori_loop(0, n_blocks, block, 0)

    @pl.when(c == n_chunks - 1)
    def _():
        fin_ref[...] = st_ref[...]


def rwkv_scan(r, w, k, a, b, v, s0, reverse):
    bsz, length, n = r.shape
    n_chunks = length // RW_TL

    def chunk(c):
        return (n_chunks - 1 - c) if reverse else c

    row_blk = pl.BlockSpec((RW_NB, RW_TL, n), lambda bi, c: (bi, chunk(c), 0))
    st_blk = pl.BlockSpec((RW_NB, RW_PAIRS, RW_HD, LANES), lambda bi, c: (bi, 0, 0, 0))
    tiles = lambda m: pltpu.VMEM((RW_NB, m, RW_HD, 2 * LANES), BF16)
    return pl.pallas_call(
        functools.partial(_rwkv_scan_kernel, reverse=reverse, n_chunks=n_chunks),
        grid=(bsz // RW_NB, n_chunks),
        in_specs=[row_blk] * 6 + [st_blk, pl.BlockSpec((2 * LANES, LANES), lambda bi, c: (0, 0)),
                                  pl.BlockSpec((2 * LANES, 2 * LANES), lambda bi, c: (0, 0))],
        out_specs=[row_blk, st_blk],
        out_shape=[jax.ShapeDtypeStruct((bsz, length, n), F32),
                   jax.ShapeDtypeStruct((bsz, RW_PAIRS, RW_HD, LANES), F32)],
        scratch_shapes=[pltpu.VMEM((RW_NB, RW_PAIRS, RW_HD, LANES), F32),
                        tiles(RW_PAIRS), tiles(RW_PAIRS), tiles(RW_PAIRS // 2),
                        pltpu.VMEM((RW_NB, RW_PAIRS, RW_HD, LANES), F32)],
        compiler_params=_params(2),
        name="rwkv_scan_rev" if reverse else "rwkv_scan_fwd",
    )(r, w, k, a, b, v, s0, _segment_ones(), _segment_ones_pair())


def _rwkv_post_kernel(yf_ref, yb_ref, r_ref, k_ref, v_ref, g_ref, rk_ref, lg_ref, lb_ref, seg_ref, o_ref):
    y = yf_ref[0] + yb_ref[0]
    mean = _head_sums(y, seg_ref) * (1.0 / RW_HD)
    yc = y - mean
    var = _head_sums(yc * yc, seg_ref) * (1.0 / RW_HD)
    y = yc * lax.rsqrt(var + GN_EPS) * lg_ref[...] + lb_ref[...]
    bonus = _head_sums(r_ref[0] * k_ref[0] * rk_ref[...], seg_ref) * v_ref[0]
    o_ref[0] = ((y + bonus) * g_ref[0]).astype(BF16)


def rwkv_post(yf, yb, r, k, v, g, r_k, ln_g, ln_b):
    bsz, length, n = r.shape
    tl = RW_TL
    blk = pl.BlockSpec((1, tl, n), lambda b, i: (b, i, 0))
    vec = pl.BlockSpec((1, n), lambda b, i: (0, 0))
    return pl.pallas_call(
        _rwkv_post_kernel,
        grid=(bsz, length // tl),
        in_specs=[blk] * 6 + [vec] * 3 + [pl.BlockSpec((2 * LANES, LANES), lambda b, i: (0, 0))],
        out_specs=blk,
        out_shape=jax.ShapeDtypeStruct((bsz, length, n), BF16),
        compiler_params=_params(2),
        name="rwkv_post",
    )(yf, yb, r, k, v, g, r_k.reshape(1, n), ln_g.reshape(1, n), ln_b.reshape(1, n), _segment_ones())


def rwkv_lora_weights(w2, a2, g2):
    w = jnp.zeros((RW_LORA, 5 * D_HALF), F32)
    for d in range(2):
        w = w.at[0:RW_W_RANK, d * D_HALF:(d + 1) * D_HALF].set(w2[d].astype(F32))
        w = w.at[RW_W_RANK:RW_W_RANK + RW_A_RANK, (2 + d) * D_HALF:(3 + d) * D_HALF].set(a2[d].astype(F32))
    w = w.at[RW_W_RANK + RW_A_RANK:, 4 * D_HALF:].set(g2.astype(F32))
    return w.astype(BF16)


def rwkv_mixer(zr, s0, rwp):
    mu, w0, w2, a0, a2, g2, k_k, k_a, r_k, ln_g, ln_b = rwp
    bsz, length = zr.shape[:2]
    mu_p = jnp.pad(mu.astype(F32), (0, RW_IN_PAD - RW_IN)).reshape(1, RW_IN_PAD)
    r, k, v, g, an, wd0, kd0, b0, wd1, kd1, b1 = rwkv_pre(zr, mu_p, rwkv_lora_weights(w2, a2, g2), w0, a0, k_k, k_a)
    if s0 is None:
        s0 = jnp.zeros((bsz, 2, RW_HEADS, RW_HD, RW_HD), F32)

    def pack_state(s):
        return s.astype(F32).reshape(bsz, RW_PAIRS, 2, RW_HD, RW_HD).transpose(0, 1, 3, 2, 4).reshape(
            bsz, RW_PAIRS, RW_HD, LANES)

    def unpack_state(s):
        return s.reshape(bsz, RW_PAIRS, RW_HD, 2, RW_HD).transpose(0, 1, 3, 2, 4).reshape(bsz, RW_HEADS, RW_HD, RW_HD)

    yf, fin_f = rwkv_scan(r, wd0, kd0, an, b0, v, pack_state(s0[:, 0]), reverse=False)
    yb, fin_b = rwkv_scan(r, wd1, kd1, an, b1, v, pack_state(s0[:, 1]), reverse=True)
    y = rwkv_post(yf, yb, r, k, v, g, r_k, ln_g, ln_b)
    return y, jnp.stack([unpack_state(fin_f), unpack_state(fin_b)], axis=1)


ROUTE_TILE = 512


def _rank_before(vals, n):
    idx = lax.broadcasted_iota(jnp.int32, vals.shape, 0)
    cnt = jnp.zeros(vals.shape, F32)
    for e in range(n):
        row = vals[e:e + 1]
        cnt = cnt + jnp.where((row > vals) | ((row == vals) & (idx > e)), 1.0, 0.0)
    return cnt


def _route_kernel(lg_ref, bias_ref, tril_ref, triu_ref, te_ref, gt_ref, rk_ref, cnt_ref, carry_ref):
    @pl.when(pl.program_id(0) == 0)
    def _():
        carry_ref[...] = jnp.zeros_like(carry_ref)

    t = lg_ref.shape[1]
    per = N_EXPERTS // N_ROUTE_GROUPS
    scores = jax.nn.sigmoid(lg_ref[...])
    sel = scores + bias_ref[...]
    sel3 = sel.reshape(N_ROUTE_GROUPS, per, t)
    m1 = jnp.max(sel3, axis=1, keepdims=True)
    within = lax.broadcasted_iota(jnp.int32, sel3.shape, 1)
    first = jnp.min(jnp.where(sel3 == m1, within, per), axis=1, keepdims=True)
    m2 = jnp.max(jnp.where(within == first, -jnp.inf, sel3), axis=1, keepdims=True)
    group_score = (m1 + m2).reshape(N_ROUTE_GROUPS, t)
    group_ok = _rank_before(group_score, N_ROUTE_GROUPS) < TOPK_ROUTE_GROUPS
    expert_ok = jnp.broadcast_to(group_ok.reshape(N_ROUTE_GROUPS, 1, t), sel3.shape).reshape(N_EXPERTS, t)
    masked = jnp.where(expert_ok, sel, NEG_INF)
    chosen = _rank_before(masked, N_EXPERTS) < TOP_K
    gates = jnp.where(chosen, scores, 0.0)
    gates = gates / jnp.sum(gates, axis=0, keepdims=True) * ROUTED_SCALE
    onehot = jnp.where(chosen, 1.0, 0.0).astype(BF16)
    order = _dot(tril_ref[...], onehot)
    before = carry_ref[:, 0:1] + _dot(onehot, triu_ref[...])
    carry_ref[...] = carry_ref[...] + jnp.sum(jnp.where(chosen, 1.0, 0.0), axis=1, keepdims=True)
    eidx = lax.broadcasted_iota(jnp.int32, chosen.shape, 0).astype(F32)
    for j in range(TOP_K):
        pick = chosen & (order == float(j + 1))
        te_ref[j:j + 1, :] = jnp.sum(jnp.where(pick, eidx, 0.0), axis=0, keepdims=True).astype(jnp.int32)
        gt_ref[j:j + 1, :] = jnp.sum(jnp.where(pick, gates, 0.0), axis=0, keepdims=True)
        rk_ref[j:j + 1, :] = jnp.sum(jnp.where(pick, before, 0.0), axis=0, keepdims=True).astype(jnp.int32)
    cnt_ref[...] = carry_ref[...]


def route(logits_t, b_router):
    e, n = logits_t.shape
    t = ROUTE_TILE
    tril = jnp.asarray(np.tril(np.ones((e, e))), BF16)
    triu = jnp.asarray(np.triu(np.ones((t, t)), 1), BF16)
    out_blk = pl.BlockSpec((TOP_K, t), lambda i: (0, i))
    te, gt, rk, cnt = pl.pallas_call(
        _route_kernel,
        grid=(n // t,),
        in_specs=[pl.BlockSpec((e, t), lambda i: (0, i)),
                  pl.BlockSpec((e, 1), lambda i: (0, 0)),
                  pl.BlockSpec((e, e), lambda i: (0, 0)),
                  pl.BlockSpec((t, t), lambda i: (0, 0))],
        out_specs=[out_blk, out_blk, out_blk, pl.BlockSpec((e, 128), lambda i: (0, 0))],
        out_shape=[jax.ShapeDtypeStruct((TOP_K, n), jnp.int32), jax.ShapeDtypeStruct((TOP_K, n), F32),
                   jax.ShapeDtypeStruct((TOP_K, n), jnp.int32), jax.ShapeDtypeStruct((e, 128), F32)],
        scratch_shapes=[pltpu.VMEM((e, 128), F32)],
        compiler_params=_params(1),
        name="route",
    )(logits_t, b_router.astype(F32).reshape(e, 1), tril, triu)
    return te, gt, rk, cnt[:, 0]


def _expert_kernel(be_ref, nu_ref, x_ref, wg_ref, wu_ref, wd_ref, o_ref, wgb_ref, wub_ref, wdb_ref):
    i = pl.program_id(0)
    used = i < nu_ref[0]
    fresh = (i == 0) | (be_ref[i] != be_ref[jnp.maximum(i - 1, 0)])

    @pl.when(used & fresh)
    def _():
        wgb_ref[...] = wg_ref[0, 0].astype(BF16)
        wub_ref[...] = wu_ref[0, 0].astype(BF16)
        wdb_ref[...] = wd_ref[0, 0].astype(BF16)

    @pl.when(used)
    def _():
        x = x_ref[...]
        gate = _dot(x, wgb_ref[...])
        h = (gate * jax.nn.sigmoid(gate) * _dot(x, wub_ref[...])).astype(BF16)
        o_ref[...] = _dot(h, wdb_ref[...]).astype(o_ref.dtype)

    @pl.when(jnp.logical_not(used))
    def _():
        o_ref[...] = jnp.zeros_like(o_ref)


def expert_ffn(xs, block_expert, n_used, w_gate, w_up, w_down, layer):
    s, d = xs.shape
    de = w_gate.shape[3]
    nb = s // MOE_ROWS
    grid_spec = pltpu.PrefetchScalarGridSpec(
        num_scalar_prefetch=2,
        grid=(nb,),
        in_specs=[pl.BlockSpec((MOE_ROWS, d), lambda i, be, nu: (i, 0)),
                  pl.BlockSpec((1, 1, d, de), lambda i, be, nu: (layer, be[i], 0, 0)),
                  pl.BlockSpec((1, 1, d, de), lambda i, be, nu: (layer, be[i], 0, 0)),
                  pl.BlockSpec((1, 1, de, d), lambda i, be, nu: (layer, be[i], 0, 0))],
        out_specs=pl.BlockSpec((MOE_ROWS, d), lambda i, be, nu: (i, 0)),
        scratch_shapes=[pltpu.VMEM((d, de), BF16), pltpu.VMEM((d, de), BF16), pltpu.VMEM((de, d), BF16)],
    )
    return pl.pallas_call(
        _expert_kernel,
        grid_spec=grid_spec,
        out_shape=jax.ShapeDtypeStruct((s, d), BF16),
        compiler_params=_params(1),
        name="expert_ffn",
    )(block_expert, n_used, xs, w_gate, w_up, w_down)


def _combine_kernel(x_ref, yg_ref, gt_ref, sh_ref, mod_ref, o_ref, *, gate_idx):
    routed = jnp.zeros(x_ref.shape, F32)
    for j in range(TOP_K):
        routed = routed + gt_ref[:, j:j + 1] * yg_ref[j].astype(F32)
    g = mod_ref[0][gate_idx:gate_idx + 1]
    o_ref[...] = x_ref[...] + g * (routed + sh_ref[...].astype(F32))


def moe_combine(x, yg, gates, shared, mods, gate_idx, n_ctx, tm=128):
    n, d = x.shape
    row = functools.partial(_cond_row, n_ctx_tiles=n_ctx // tm, tiles_per_sample=DEC_SEQ // tm)
    return pl.pallas_call(
        functools.partial(_combine_kernel, gate_idx=gate_idx),
        grid=(n // tm,),
        in_specs=[pl.BlockSpec((tm, d), lambda i: (i, 0)),
                  pl.BlockSpec((TOP_K, tm, d), lambda i: (0, i, 0)),
                  pl.BlockSpec((tm, TOP_K), lambda i: (i, 0)),
                  pl.BlockSpec((tm, d), lambda i: (i, 0)),
                  pl.BlockSpec((1, 6, d), lambda i: (row(i), 0, 0))],
        out_specs=pl.BlockSpec((tm, d), lambda i: (i, 0)),
        out_shape=jax.ShapeDtypeStruct((n, d), F32),
        compiler_params=_params(1),
        name="moe_combine",
    )(x, yg, gates, shared, mods)


def moe_ffn(x, h, logits_t, mods, gate_idx, n_ctx, layer, b_router, w_gate, w_up, w_down, ws_gate, ws_up, ws_down):
    n, d = h.shape
    top_e, gates, rank, counts = route(logits_t, b_router)
    counts = counts.astype(jnp.int32)
    padded = (counts + MOE_ROWS - 1) // MOE_ROWS * MOE_ROWS
    pad_end = jnp.cumsum(padded)
    pad_start = pad_end - padded
    n_slots = n * TOP_K + N_EXPERTS * MOE_ROWS
    nb = n_slots // MOE_ROWS
    expert_ids = jnp.arange(N_EXPERTS, dtype=jnp.int32)
    start_of = jnp.sum(jnp.where(top_e[..., None] == expert_ids, pad_start, 0), axis=-1)
    pos = start_of + rank
    tok = jnp.broadcast_to(jnp.arange(n, dtype=jnp.int32)[None, :], pos.shape)
    slot_tok = (jnp.arange(n_slots, dtype=jnp.int32) % n).at[pos.reshape(-1)].set(tok.reshape(-1), unique_indices=True)
    block_start = jnp.arange(nb, dtype=jnp.int32) * MOE_ROWS
    block_expert = jnp.minimum(jnp.sum(pad_end[None, :] <= block_start[:, None], axis=1), N_EXPERTS - 1).astype(jnp.int32)
    n_used = (pad_end[-1:] // MOE_ROWS).astype(jnp.int32)
    xs = h[slot_tok]
    yb = expert_ffn(xs, block_expert, n_used, w_gate, w_up, w_down, layer)
    shared = expert_ffn(h, jnp.zeros((n // MOE_ROWS,), jnp.int32), jnp.full((1,), n // MOE_ROWS, jnp.int32),
                        ws_gate[:, None], ws_up[:, None], ws_down[:, None], layer)
    yg = yb[pos]
    return moe_combine(x, yg, gates.T, shared, mods, gate_idx, n_ctx)


def _layer_ab(h, n_ctx, bsz_ctx, bsz_lat, p):
    z = project(h, p['w_in'], tn=1024)
    s5w = s5_weights(*p['s5_disc'])
    outs = []
    extras = {}
    for name, lo, bsz, length in (('ctx', 0, bsz_ctx, SEQ), ('lat', n_ctx, bsz_lat, DEC_SEQ)):
        zg = z[lo:lo + bsz * length].reshape(bsz, length, 4 * D_HALF)
        u = zg[..., :D_HALF]
        q, k, v = (_heads_first(zg[..., (1 + j) * D_HALF:(2 + j) * D_HALF], bsz, length, NA_HEADS) for j in range(3))
        if name == 'ctx':
            y_a, s5_fin = s5_mixer(u, None, p['s5_glu'], s5w)
            y_b, k_n = context_attention(q, k, v, p['q_g'], p['k_g'])
            extras = {'k': k_n, 'v': v, 's5': s5_fin}
        else:
            y_a, _ = s5_mixer(u, p['s5_state'], p['s5_glu'], s5w)
            y_b = neighbourhood_attention(q, k, v, p['ck'], p['cv'], p['rpb'], p['q_g'], p['k_g'])
        outs.append(jnp.concatenate([y_a, y_b], axis=-1).reshape(bsz * length, 2 * D_HALF))
    return jnp.concatenate(outs, axis=0), extras


def _layer_cd(h, n_ctx, bsz_ctx, bsz_lat, p):
    w_in = p['w_in']
    zh_all = project(h, w_in[:, :3 * D_HALF], tn=1024)
    w_rw = jnp.pad(w_in[:, 3 * D_HALF:], ((0, 0), (0, RW_IN_PAD - RW_IN)))
    zr_all = project(h, w_rw, tn=512)
    outs = []
    extras = {}
    for name, lo, bsz, length in (('ctx', 0, bsz_ctx, SEQ), ('lat', n_ctx, bsz_lat, DEC_SEQ)):
        zh = zh_all[lo:lo + bsz * length].reshape(bsz, length, 3 * D_HALF)
        zr = zr_all[lo:lo + bsz * length].reshape(bsz, length, RW_IN_PAD)
        filt = hyena_filter(length, *p['hy_filter'])
        y_c = hyena_mixer(zh, filt, p['hy_short_w'], p['hy_short_b'], p['hy_bias'])
        y_d, rw_fin = rwkv_mixer(zr, None if name == 'ctx' else p['rw_state'], p['rwp'])
        if name == 'ctx':
            extras = {'rw': rw_fin}
        outs.append(jnp.concatenate([y_c, y_d], axis=-1).reshape(bsz * length, 2 * D_HALF))
    return jnp.concatenate(outs, axis=0), extras


def kernel(x_prompt, x_sample, cache_na_k, cache_na_v, state_s5, state_rwkv, c, c_ctx, ada_w, ada_b, norm_mix, norm_ffn, ab_w_in, ab_w_out, s5_lam_re, s5_lam_im, s5_log_step, s5_b_re, s5_b_im, s5_c_re, s5_c_im, s5_d, s5_w_glu, s5_b_glu, na_q_norm, na_k_norm, na_rpb, cd_w_in, cd_w_out, hy_short_w, hy_short_b, hy_w1, hy_b1, hy_freq, hy_w2, hy_b2, hy_w3, hy_decay, hy_bias, rw_mu, rw_w0, rw_w2, rw_a0, rw_a2, rw_g2, rw_k_k, rw_k_a, rw_r_k, rw_ln_g, rw_ln_b, moe_router, moe_router_bias, moe_w_gate, moe_w_up, moe_w_down, moe_ws_gate, moe_ws_up, moe_ws_down):
    bsz_ctx, seq, d = x_prompt.shape
    bsz_lat, dec_seq, _ = x_sample.shape
    assert (seq, dec_seq, d) == (SEQ, DEC_SEQ, D_MODEL) and bsz_lat + 1 <= 8
    depth = ada_w.shape[0]
    n_ctx = bsz_ctx * seq
    x = jnp.concatenate([x_prompt.reshape(n_ctx, d), x_sample.reshape(bsz_lat * dec_seq, d)], axis=0).astype(F32)

    cond = jnp.concatenate([c_ctx[None, :], c, jnp.zeros((8 - 1 - bsz_lat, d), c.dtype)], axis=0).astype(F32)
    mods_all = ada_table(cond, ada_w, ada_b).reshape(depth, 8, 6, d)

    new_k, new_v, new_s5, new_rw = [], [], [], []
    for l in range(depth):
        mods = mods_all[l]
        i = l // 2
        h = modulate(x, norm_mix[l], mods, 0, 1, n_ctx)
        if l % 2 == 0:
            p = {'w_in': ab_w_in[i], 'q_g': na_q_norm[i], 'k_g': na_k_norm[i], 'rpb': na_rpb[i],
                 'ck': cache_na_k[:, i], 'cv': cache_na_v[:, i], 's5_state': state_s5[:, i],
                 's5_disc': (s5_lam_re[i], s5_lam_im[i], s5_log_step[i], s5_b_re[i], s5_b_im[i], s5_c_re[i], s5_c_im[i]),
                 's5_glu': (s5_d[i], s5_w_glu[i], s5_b_glu[i])}
            y, ex = _layer_ab(h, n_ctx, bsz_ctx, bsz_lat, p)
            new_k.append(ex['k'])
            new_v.append(ex['v'])
            new_s5.append(ex['s5'])
            w_out = ab_w_out[i]
        else:
            p = {'w_in': cd_w_in[i], 'hy_filter': (hy_w1[i], hy_b1[i], hy_freq[i], hy_w2[i], hy_b2[i], hy_w3[i], hy_decay[i]),
                 'hy_short_w': hy_short_w[i], 'hy_short_b': hy_short_b[i], 'hy_bias': hy_bias[i],
                 'rw_state': state_rwkv[:, i],
                 'rwp': (rw_mu[i], rw_w0[i], rw_w2[i], rw_a0[i], rw_a2[i], rw_g2[i], rw_k_k[i], rw_k_a[i], rw_r_k[i],
                         rw_ln_g[i], rw_ln_b[i])}
            y, ex = _layer_cd(h, n_ctx, bsz_ctx, bsz_lat, p)
            new_rw.append(ex['rw'])
            w_out = cd_w_out[i]
        x = project_residual(y, w_out, x, mods, 2, n_ctx)
        h, logits_t = modulate(x, norm_ffn[l], mods, 3, 4, n_ctx, w_router_t=moe_router[l].T)
        x = moe_ffn(x, h, logits_t, mods, 5, n_ctx, l, moe_router_bias[l], moe_w_gate, moe_w_up, moe_w_down,
                    moe_ws_gate, moe_ws_up, moe_ws_down)

    y_prompt = x[:n_ctx].reshape(bsz_ctx, seq, d)
    y_sample = x[n_ctx:].reshape(bsz_lat, dec_seq, d)
    return (y_prompt, y_sample, jnp.stack(new_k, axis=1), jnp.stack(new_v, axis=1),
            jnp.stack(new_s5, axis=1), jnp.stack(new_rw, axis=1))
```

```python
import functools
import math

import numpy as np
import jax
import jax.numpy as jnp
from jax import lax
from jax.experimental import pallas as pl
from jax.experimental.pallas import tpu as pltpu

F32 = jnp.float32
BF16 = jnp.bfloat16

D_MODEL = 2048
D_HALF = 1024
SEQ = 256
DEC_SEQ = 2048
GRID_W = 64
S5_GROUP = 16
S5_GROUPS = 64
S5_STATE = 64
S5_LANES = S5_GROUPS * S5_STATE
NA_HEADS = 16
NA_HD = 64
NA_KH = 8
NA_KW = 16
HY_BANDS = 16
RW_HEADS = 16
RW_HD = 64
RW_W_RANK = 64
RW_A_RANK = 64
RW_G_RANK = 128
RW_LORA = 256
RW_IN = 3 * D_HALF + RW_LORA
RW_IN_PAD = 3584
N_EXPERTS = 64
TOP_K = 8
N_ROUTE_GROUPS = 8
TOPK_ROUTE_GROUPS = 4
D_EXPERT = 512
ROUTED_SCALE = 2.5
EPS = 1e-6
GN_EPS = 64e-5
NEG_INF = -1e30

TOK_TILE = 256
MOE_ROWS = 256
VMEM_LIMIT = 56 * 1024 * 1024


def _params(n_axes, vmem=VMEM_LIMIT):
    return pltpu.CompilerParams(dimension_semantics=("arbitrary",) * n_axes, vmem_limit_bytes=vmem)


def _dot(a, b):
    return jnp.dot(a, b, preferred_element_type=F32)


def _dot_nt(a, b):
    return lax.dot_general(a, b, (((1,), (1,)), ((), ())), preferred_element_type=F32)


def _split_bf16(x):
    hi = x.astype(BF16)
    lo = (x - hi.astype(F32)).astype(BF16)
    return hi, lo


def _cond_row(i, n_ctx_tiles, tiles_per_sample):
    return jnp.where(i < n_ctx_tiles, 0, 1 + (i - n_ctx_tiles) // tiles_per_sample)


def _ada_kernel(c_ref, w_ref, b_ref, o_ref):
    c = c_ref[...]
    s = (c * jax.nn.sigmoid(c)).astype(BF16)
    o_ref[0] = _dot(s, w_ref[0].astype(BF16)) + b_ref[0]


def ada_table(cond8, ada_w, ada_b):
    depth, d, n = ada_w.shape
    tn = 1024
    return pl.pallas_call(
        _ada_kernel,
        grid=(depth, n // tn),
        in_specs=[pl.BlockSpec((8, d), lambda l, j: (0, 0)),
                  pl.BlockSpec((1, d, tn), lambda l, j: (l, 0, j)),
                  pl.BlockSpec((1, 1, tn), lambda l, j: (l, 0, j))],
        out_specs=pl.BlockSpec((1, 8, tn), lambda l, j: (l, 0, j)),
        out_shape=jax.ShapeDtypeStruct((depth, 8, n), F32),
        compiler_params=_params(2),
        name="ada_table",
    )(cond8, ada_w, ada_b.reshape(depth, 1, n))


def _modulate_kernel(x_ref, g_ref, mod_ref, h_ref, *, shift_idx, scale_idx):
    x = x_ref[...]
    y = x * lax.rsqrt(jnp.mean(x * x, axis=-1, keepdims=True) + EPS)
    m = mod_ref[0]
    h = y * g_ref[...] * (1.0 + m[scale_idx:scale_idx + 1]) + m[shift_idx:shift_idx + 1]
    h_ref[...] = h.astype(BF16)


def _modulate_router_kernel(x_ref, g_ref, mod_ref, wr_ref, h_ref, lg_ref, *, shift_idx, scale_idx):
    x = x_ref[...]
    y = x * lax.rsqrt(jnp.mean(x * x, axis=-1, keepdims=True) + EPS)
    m = mod_ref[0]
    h = y * g_ref[...] * (1.0 + m[scale_idx:scale_idx + 1]) + m[shift_idx:shift_idx + 1]
    h_hi, h_lo = _split_bf16(h)
    h_ref[...] = h_hi
    w_hi, w_lo = _split_bf16(wr_ref[...])
    lg_ref[...] = _dot_nt(w_hi, h_hi) + _dot_nt(w_hi, h_lo) + _dot_nt(w_lo, h_hi)


def modulate(x, gain, mods, shift_idx, scale_idx, n_ctx, w_router_t=None):
    n, d = x.shape
    tm = TOK_TILE
    row = functools.partial(_cond_row, n_ctx_tiles=n_ctx // tm, tiles_per_sample=DEC_SEQ // tm)
    in_specs = [pl.BlockSpec((tm, d), lambda i: (i, 0)),
                pl.BlockSpec((1, d), lambda i: (0, 0)),
                pl.BlockSpec((1, 6, d), lambda i: (row(i), 0, 0))]
    if w_router_t is None:
        return pl.pallas_call(
            functools.partial(_modulate_kernel, shift_idx=shift_idx, scale_idx=scale_idx),
            grid=(n // tm,),
            in_specs=in_specs,
            out_specs=pl.BlockSpec((tm, d), lambda i: (i, 0)),
            out_shape=jax.ShapeDtypeStruct((n, d), BF16),
            compiler_params=_params(1),
            name="modulate",
        )(x, gain.reshape(1, d), mods)
    e = w_router_t.shape[0]
    return pl.pallas_call(
        functools.partial(_modulate_router_kernel, shift_idx=shift_idx, scale_idx=scale_idx),
        grid=(n // tm,),
        in_specs=in_specs + [pl.BlockSpec((e, d), lambda i: (0, 0))],
        out_specs=[pl.BlockSpec((tm, d), lambda i: (i, 0)),
                   pl.BlockSpec((e, tm), lambda i: (0, i))],
        out_shape=[jax.ShapeDtypeStruct((n, d), BF16), jax.ShapeDtypeStruct((e, n), F32)],
        compiler_params=_params(1),
        name="modulate_router",
    )(x, gain.reshape(1, d), mods, w_router_t)


def _proj_kernel(x_ref, w_ref, o_ref, wbf_ref):
    @pl.when(pl.program_id(1) == 0)
    def _():
        wbf_ref[...] = w_ref[...].astype(BF16)

    o_ref[...] = _dot(x_ref[...], wbf_ref[...]).astype(o_ref.dtype)


def project(x, w, tn, tm=512, out_dtype=F32):
    m, k = x.shape
    n = w.shape[1]
    return pl.pallas_call(
        _proj_kernel,
        grid=(n // tn, m // tm),
        in_specs=[pl.BlockSpec((tm, k), lambda j, i: (i, 0)),
                  pl.BlockSpec((k, tn), lambda j, i: (0, j))],
        out_specs=pl.BlockSpec((tm, tn), lambda j, i: (i, j)),
        out_shape=jax.ShapeDtypeStruct((m, n), out_dtype),
        scratch_shapes=[pltpu.VMEM((k, tn), BF16)],
        compiler_params=_params(2),
        name="project",
    )(x, w)


def _proj_residual_kernel(y_ref, w_ref, x_ref, mod_ref, o_ref, wbf_ref, *, gate_idx):
    @pl.when(pl.program_id(1) == 0)
    def _():
        wbf_ref[...] = w_ref[...].astype(BF16)

    g = mod_ref[0][gate_idx:gate_idx + 1]
    o_ref[...] = x_ref[...] + g * _dot(y_ref[...], wbf_ref[...])


def project_residual(y, w, x, mods, gate_idx, n_ctx, tn=1024, tm=512):
    m, k = y.shape
    n = w.shape[1]
    row = functools.partial(_cond_row, n_ctx_tiles=n_ctx // tm, tiles_per_sample=DEC_SEQ // tm)
    return pl.pallas_call(
        functools.partial(_proj_residual_kernel, gate_idx=gate_idx),
        grid=(n // tn, m // tm),
        in_specs=[pl.BlockSpec((tm, k), lambda j, i: (i, 0)),
                  pl.BlockSpec((k, tn), lambda j, i: (0, j)),
                  pl.BlockSpec((tm, tn), lambda j, i: (i, j)),
                  pl.BlockSpec((1, 6, tn), lambda j, i: (row(i), 0, j))],
        out_specs=pl.BlockSpec((tm, tn), lambda j, i: (i, j)),
        out_shape=jax.ShapeDtypeStruct((m, n), F32),
        scratch_shapes=[pltpu.VMEM((k, tn), BF16)],
        compiler_params=_params(2),
        name="project_residual",
    )(y, w, x, mods)


S5_CHUNK = 64
S5_ROWS = 8
S5_BLK = 8
S5_SCAN_LANES = 1024


def _s5_kernel(u_ref, wbr_ref, wbi_ref, wcr_ref, wci_ref, lam_ref, x0_ref, y_ref, fin_ref,
               bur_ref, bui_ref, st_ref, *, n_chunks):
    d = pl.program_id(0)
    c = pl.program_id(2)
    tc = S5_CHUNK
    cin = S5_BLK * S5_GROUP
    cst = S5_BLK * S5_STATE

    @pl.when(c == 0)
    def _():
        st_ref[...] = x0_ref[0]

    u = u_ref[...].reshape(tc * S5_ROWS, D_HALF).astype(BF16)
    for k in range(S5_GROUPS // S5_BLK):
        uk = u[:, k * cin:(k + 1) * cin]
        bur_ref[:, k * cst:(k + 1) * cst] = _dot(uk, wbr_ref[0, k])
        bui_ref[:, k * cst:(k + 1) * cst] = _dot(uk, wbi_ref[0, k])

    for j in range(S5_LANES // S5_SCAN_LANES):
        sl = slice(j * S5_SCAN_LANES, (j + 1) * S5_SCAN_LANES)
        lr = jnp.broadcast_to(lam_ref[0, 0:1, sl], (S5_ROWS, S5_SCAN_LANES))
        li = jnp.broadcast_to(lam_ref[0, 1:2, sl], (S5_ROWS, S5_SCAN_LANES))

        def step(i, carry, sl=sl, lr=lr, li=li):
            sr, si = carry
            t = jnp.where(d == 0, i, tc - 1 - i)
            row = pl.multiple_of(t * S5_ROWS, S5_ROWS)
            nr = lr * sr - li * si + bur_ref[pl.ds(row, S5_ROWS), sl]
            ni = lr * si + li * sr + bui_ref[pl.ds(row, S5_ROWS), sl]
            bur_ref[pl.ds(row, S5_ROWS), sl] = nr
            bui_ref[pl.ds(row, S5_ROWS), sl] = ni
            return nr, ni

        sr, si = lax.fori_loop(0, tc, step, (st_ref[0, :, sl], st_ref[1, :, sl]), unroll=4)
        st_ref[0, :, sl] = sr
        st_ref[1, :, sl] = si

    xr = bur_ref[...].astype(BF16)
    xi = bui_ref[...].astype(BF16)
    for k in range(S5_GROUPS // S5_BLK):
        yk = _dot(xr[:, k * cst:(k + 1) * cst], wcr_ref[0, k]) + _dot(xi[:, k * cst:(k + 1) * cst], wci_ref[0, k])
        y_ref[0, :, :, k * cin:(k + 1) * cin] = yk.reshape(tc, S5_ROWS, cin)

    @pl.when(c == n_chunks - 1)
    def _():
        fin_ref[0] = st_ref[...]


def s5_scan(u_t, x0, wbr, wbi, wcr, wci, lam):
    length, bsz, _ = u_t.shape
    n_chunks = length // S5_CHUNK
    nblk = S5_GROUPS // S5_BLK
    cin = S5_BLK * S5_GROUP
    cst = S5_BLK * S5_STATE

    def chunk(d, c):
        return jnp.where(d == 0, c, n_chunks - 1 - c)

    return pl.pallas_call(
        functools.partial(_s5_kernel, n_chunks=n_chunks),
        grid=(2, bsz // S5_ROWS, n_chunks),
        in_specs=[pl.BlockSpec((S5_CHUNK, S5_ROWS, D_HALF), lambda d, b, c: (chunk(d, c), b, 0)),
                  pl.BlockSpec((1, nblk, cin, cst), lambda d, b, c: (d, 0, 0, 0)),
                  pl.BlockSpec((1, nblk, cin, cst), lambda d, b, c: (d, 0, 0, 0)),
                  pl.BlockSpec((1, nblk, cst, cin), lambda d, b, c: (d, 0, 0, 0)),
                  pl.BlockSpec((1, nblk, cst, cin), lambda d, b, c: (d, 0, 0, 0)),
                  pl.BlockSpec((1, 2, S5_LANES), lambda d, b, c: (d, 0, 0)),
                  pl.BlockSpec((1, 2, S5_ROWS, S5_LANES), lambda d, b, c: (d, 0, b, 0))],
        out_specs=[pl.BlockSpec((1, S5_CHUNK, S5_ROWS, D_HALF), lambda d, b, c: (d, chunk(d, c), b, 0)),
                   pl.BlockSpec((1, 2, S5_ROWS, S5_LANES), lambda d, b, c: (d, 0, b, 0))],
        out_shape=[jax.ShapeDtypeStruct((2, length, bsz, D_HALF), F32),
                   jax.ShapeDtypeStruct((2, 2, bsz, S5_LANES), F32)],
        scratch_shapes=[pltpu.VMEM((S5_CHUNK * S5_ROWS, S5_LANES), F32),
                        pltpu.VMEM((S5_CHUNK * S5_ROWS, S5_LANES), F32),
                        pltpu.VMEM((2, S5_ROWS, S5_LANES), F32)],
        compiler_params=_params(3),
        name="s5_scan",
    )(u_t, wbr, wbi, wcr, wci, lam, x0)


def s5_weights(lam_re, lam_im, log_step, b_re, b_im, c_re, c_im):
    lam = lax.complex(lam_re.astype(F32), lam_im.astype(F32))
    lam_bar = jnp.exp(lam * jnp.exp(log_step.astype(F32))[..., None])
    b_bar = ((lam_bar - 1.0) / lam)[..., None] * lax.complex(b_re.astype(F32), b_im.astype(F32))
    nblk = S5_GROUPS // S5_BLK
    eye = jnp.eye(S5_BLK, dtype=F32)

    def embed_in(b):
        b = b.reshape(2, nblk, S5_BLK, S5_STATE, S5_GROUP)
        w = jnp.einsum('dkgpc,gh->dkgchp', b, eye)
        return w.reshape(2, nblk, S5_BLK * S5_GROUP, S5_BLK * S5_STATE).astype(BF16)

    def embed_out(cm):
        cm = cm.reshape(2, nblk, S5_BLK, S5_GROUP, S5_STATE)
        w = jnp.einsum('dkgcp,gh->dkgphc', cm, eye)
        return w.reshape(2, nblk, S5_BLK * S5_STATE, S5_BLK * S5_GROUP).astype(BF16)

    lam_rows = jnp.stack([lam_bar.real.reshape(2, S5_LANES), lam_bar.imag.reshape(2, S5_LANES)], axis=1)
    return (embed_in(b_bar.real), embed_in(b_bar.imag),
            embed_out(c_re.astype(F32)), embed_out(-c_im.astype(F32)), lam_rows)


def _glu_kernel(yf_ref, yb_ref, u_ref, d_ref, w_ref, b_ref, o_ref, wbf_ref):
    @pl.when(pl.program_id(0) == 0)
    def _():
        wbf_ref[...] = w_ref[...].astype(BF16)

    y = jax.nn.gelu(u_ref[...] * d_ref[...] + yf_ref[0] + yb_ref[0])
    o_ref[...] = (y * jax.nn.sigmoid(_dot(y.astype(BF16), wbf_ref[...]) + b_ref[...])).astype(BF16)


def s5_glu(y_dirs, u, d_skip, w_glu, b_glu, tm=512):
    m, n = u.shape
    return pl.pallas_call(
        _glu_kernel,
        grid=(m // tm,),
        in_specs=[pl.BlockSpec((1, tm, n), lambda i: (0, i, 0)),
                  pl.BlockSpec((1, tm, n), lambda i: (1, i, 0)),
                  pl.BlockSpec((tm, n), lambda i: (i, 0)),
                  pl.BlockSpec((1, n), lambda i: (0, 0)),
                  pl.BlockSpec((n, n), lambda i: (0, 0)),
                  pl.BlockSpec((1, n), lambda i: (0, 0))],
        out_specs=pl.BlockSpec((tm, n), lambda i: (i, 0)),
        out_shape=jax.ShapeDtypeStruct((m, n), BF16),
        scratch_shapes=[pltpu.VMEM((n, n), BF16)],
        compiler_params=_params(1),
        name="s5_glu",
    )(y_dirs, y_dirs, u, d_skip.reshape(1, n), w_glu, b_glu.reshape(1, n))


def s5_mixer(u, x0, s5p, weights):
    bsz, length, _ = u.shape
    bp = -(-bsz // S5_ROWS) * S5_ROWS
    wbr, wbi, wcr, wci, lam = weights
    d_skip, w_glu, b_glu = s5p
    u_t = jnp.swapaxes(u, 0, 1)
    if x0 is None:
        x0_t = jnp.zeros((2, 2, bp, S5_LANES), F32)
    else:
        x0_t = x0.astype(F32).reshape(bsz, 2, 2, S5_LANES).transpose(1, 2, 0, 3)
    if bp != bsz:
        u_t = jnp.pad(u_t, ((0, 0), (0, bp - bsz), (0, 0)))
        x0_t = jnp.pad(x0_t, ((0, 0), (0, 0), (0, bp - bsz), (0, 0)))
    y_dirs, fin = s5_scan(u_t, x0_t, wbr, wbi, wcr, wci, lam)
    y = s5_glu(y_dirs.reshape(2, length * bp, D_HALF), u_t.reshape(length * bp, D_HALF), d_skip, w_glu, b_glu)
    y = jnp.swapaxes(y.reshape(length, bp, D_HALF)[:, :bsz], 0, 1)
    fin = fin[:, :, :bsz].transpose(2, 0, 1, 3).reshape(bsz, 2, 2, S5_GROUPS, S5_STATE)
    return y, fin


def _head_rms(x, g):
    return x * lax.rsqrt(jnp.mean(x * x, axis=-1, keepdims=True) + EPS) * g


def _ctx_attn_kernel(q_ref, k_ref, v_ref, qg_ref, kg_ref, o_ref, kn_ref):
    scale = NA_HD ** -0.5
    for h in range(NA_HEADS):
        q = _head_rms(q_ref[0, h], qg_ref[...])
        k = _head_rms(k_ref[0, h], kg_ref[...])
        kn_ref[0, h] = k
        s = _dot_nt(q.astype(BF16), k.astype(BF16)) * scale
        p = jnp.exp(s - jnp.max(s, axis=-1, keepdims=True))
        l = jnp.sum(p, axis=-1, keepdims=True)
        o = _dot(p.astype(BF16), v_ref[0, h].astype(BF16)) / l
        o_ref[0, :, h * NA_HD:(h + 1) * NA_HD] = o.astype(BF16)


def context_attention(q, k, v, q_g, k_g):
    bsz, nh, t, hd = q.shape
    blk = pl.BlockSpec((1, nh, t, hd), lambda b: (b, 0, 0, 0))
    vec = pl.BlockSpec((1, hd), lambda b: (0, 0))
    return pl.pallas_call(
        _ctx_attn_kernel,
        grid=(bsz,),
        in_specs=[blk, blk, blk, vec, vec],
        out_specs=[pl.BlockSpec((1, t, nh * hd), lambda b: (b, 0, 0)), blk],
        out_shape=[jax.ShapeDtypeStruct((bsz, t, nh * hd), BF16), jax.ShapeDtypeStruct((bsz, nh, t, hd), F32)],
        compiler_params=_params(1),
        name="context_attention",
    )(q, k, v, q_g.reshape(1, hd), k_g.reshape(1, hd))


NA_PAIR = 2


def _na_kernel(q_ref, k_ref, v_ref, ck_ref, cv_ref, bias_ref, qg_ref, kg_ref, o_ref, kn_ref, vn_ref, *, rows):
    scale = NA_HD ** -0.5
    win = NA_KH * GRID_W
    for hh in range(NA_PAIR):
        kn_ref[...] = _head_rms(k_ref[0, hh], kg_ref[...]).astype(BF16)
        vn_ref[...] = v_ref[0, hh].astype(BF16)
        ck = ck_ref[0, hh].astype(BF16)
        cv = cv_ref[0, hh].astype(BF16)
        for r in range(rows):
            rs = min(max(r - NA_KH // 2, 0), rows - NA_KH)
            cls = rs - r + NA_KH - 1
            q = _head_rms(q_ref[0, hh, r * GRID_W:(r + 1) * GRID_W, :], qg_ref[...]).astype(BF16)
            kw = kn_ref[rs * GRID_W:rs * GRID_W + win, :]
            vw = vn_ref[rs * GRID_W:rs * GRID_W + win, :]
            s_w = _dot_nt(q, kw) * scale + bias_ref[hh, cls]
            s_c = _dot_nt(q, ck) * scale
            m = jnp.maximum(jnp.max(s_w, axis=-1, keepdims=True), jnp.max(s_c, axis=-1, keepdims=True))
            p_w = jnp.exp(s_w - m)
            p_c = jnp.exp(s_c - m)
            l = jnp.sum(p_w, axis=-1, keepdims=True) + jnp.sum(p_c, axis=-1, keepdims=True)
            o = (_dot(p_w.astype(BF16), vw) + _dot(p_c.astype(BF16), cv)) / l
            o_ref[0, r * GRID_W:(r + 1) * GRID_W, hh * NA_HD:(hh + 1) * NA_HD] = o.astype(BF16)


def na_bias_table(rpb, rows):
    qc = np.arange(GRID_W)
    kc = np.arange(GRID_W)
    c0 = np.clip(qc - NA_KW // 2, 0, GRID_W - NA_KW)
    ok = (kc[None, :] >= c0[:, None]) & (kc[None, :] < c0[:, None] + NA_KW)
    dc = np.clip(kc[None, :] - qc[:, None] + NA_KW - 1, 0, 2 * NA_KW - 2)
    dr = np.arange(NA_KH)[:, None] + np.arange(NA_KH)[None, :]
    b = rpb.astype(F32)[:, dr][:, :, :, dc]
    b = jnp.where(ok[None, None, None], b, NEG_INF)
    return b.transpose(0, 1, 3, 2, 4).reshape(rpb.shape[0], NA_KH, GRID_W, NA_KH * GRID_W)


def neighbourhood_attention(q, k, v, ck, cv, rpb, q_g, k_g):
    bsz, nh, length, hd = q.shape
    past = ck.shape[2]
    rows = length // GRID_W
    bias = na_bias_table(rpb, rows)
    blk = pl.BlockSpec((1, NA_PAIR, length, hd), lambda b, h: (b, h, 0, 0))
    cblk = pl.BlockSpec((1, NA_PAIR, past, hd), lambda b, h: (b, h, 0, 0))
    vec = pl.BlockSpec((1, hd), lambda b, h: (0, 0))
    return pl.pallas_call(
        functools.partial(_na_kernel, rows=rows),
        grid=(bsz, nh // NA_PAIR),
        in_specs=[blk, blk, blk, cblk, cblk,
                  pl.BlockSpec((NA_PAIR, NA_KH, GRID_W, NA_KH * GRID_W), lambda b, h: (h, 0, 0, 0)),
                  vec, vec],
        out_specs=pl.BlockSpec((1, length, NA_PAIR * hd), lambda b, h: (b, 0, h)),
        out_shape=jax.ShapeDtypeStruct((bsz, length, nh * hd), BF16),
        scratch_shapes=[pltpu.VMEM((length, hd), BF16), pltpu.VMEM((length, hd), BF16)],
        compiler_params=_params(2),
        name="neighbourhood_attention",
    )(q, k, v, ck, cv, bias, q_g.reshape(1, hd), k_g.reshape(1, hd))


def _heads_first(t, bsz, length, nh):
    return t.reshape(bsz, length, nh, -1).transpose(0, 2, 1, 3)


HY_EMB_PAD = 64


def _dot3(a, b):
    a_hi, a_lo = _split_bf16(a)
    b_hi, b_lo = _split_bf16(b)
    return _dot(a_hi, b_hi) + _dot(a_hi, b_lo) + _dot(a_lo, b_hi)


def _hyena_filter_kernel(z_ref, w1_ref, b1_ref, f_ref, w2_ref, b2_ref, w3_ref, dec_ref, o_ref):
    length = z_ref.shape[0]
    h = jnp.sin(f_ref[0:1, :] * (_dot3(z_ref[...], w1_ref[...]) + b1_ref[...]))
    h = jnp.sin(f_ref[1:2, :] * (_dot3(h, w2_ref[...]) + b2_ref[...]))
    filt = _dot3(h, w3_ref[...])
    t = lax.broadcasted_iota(jnp.int32, filt.shape, 0).astype(F32)
    offset = jnp.abs(t - float(length // 2)) / float(length)
    filt = filt * jnp.exp(-offset * jnp.abs(dec_ref[...]))
    o_ref[...] = filt / (jnp.sum(jnp.abs(filt), axis=0, keepdims=True) + EPS)


def _hyena_embedding(length):
    t = np.arange(length, dtype=np.float32) / np.float32(length)
    bands = np.linspace(1e-4, HY_BANDS - 1, HY_BANDS, dtype=np.float32)
    ang = (2 * math.pi * t[:, None] * bands[None, :]).astype(np.float32)
    z = np.concatenate([t[:, None], np.cos(ang), np.sin(ang)], axis=-1).astype(np.float32)
    return np.pad(z, ((0, 0), (0, HY_EMB_PAD - z.shape[1])))


def hyena_filter(length, w1, b1, freq, w2, b2, w3, decay):
    z = jnp.asarray(_hyena_embedding(length))
    w1p = jnp.pad(w1.astype(F32), ((0, HY_EMB_PAD - w1.shape[0]), (0, 0)))
    n = w3.shape[1]
    return pl.pallas_call(
        _hyena_filter_kernel,
        out_shape=jax.ShapeDtypeStruct((length, n), F32),
        compiler_params=_params(0),
        name="hyena_filter",
    )(z, w1p, b1.reshape(1, -1), freq, w2, b2.reshape(1, -1), w3, decay.reshape(1, n))


def _dft_tile(length):
    return min(length, 512)


def _dft_matrices(length):
    n = 2 * length
    tf = _dft_tile(length)
    f = np.arange(length)
    s = np.arange(length)
    ang = 2 * np.pi * ((f[:, None] * s[None, :]) % n) / n
    fc = np.cos(ang)
    fs = -np.sin(ang)
    fs[0] = np.where(s % 2 == 0, 1.0, -1.0)
    nt = np.arange(length) + length // 2
    ang_i = 2 * np.pi * ((nt[:, None] * f[None, :]) % n) / n
    wf = np.where(f == 0, 1.0, 2.0)[None, :]
    gc = wf * np.cos(ang_i) / n
    gs = -wf * np.sin(ang_i) / n
    gs[:, 0] = np.where(nt % 2 == 0, 1.0, -1.0) / n
    n_tiles = length // tf
    fwd = np.concatenate([fc.reshape(n_tiles, tf, length), fs.reshape(n_tiles, tf, length)], axis=1)
    inv = np.concatenate([gc.reshape(length, n_tiles, tf), gs.reshape(length, n_tiles, tf)], axis=2)
    return (jnp.asarray(fwd.reshape(n_tiles * 2 * tf, length), BF16),
            jnp.asarray(inv.transpose(1, 0, 2), BF16))


def _shift_rows(x, up):
    n = x.shape[0]
    row = lax.broadcasted_iota(jnp.int32, x.shape, 0)
    if up:
        return jnp.where(row == n - 1, 0.0, pltpu.roll(x, n - 1, 0))
    return jnp.where(row == 0, 0.0, pltpu.roll(x, 1, 0))


def _conv3(x, w, b):
    return _shift_rows(x, False) * w[0:1] + x * w[1:2] + _shift_rows(x, True) * w[2:3] + b


def _hyena_conv_kernel(x0_ref, x1_ref, v_ref, w0_ref, w1_ref, wv_ref, b0_ref, b1_ref, bv_ref, bias_ref,
                       f_ref, g_ref, hf_ref, o_ref, gated_ref, gbf_ref, acc_ref, *, n_tiles, tf):
    j = pl.program_id(2)

    @pl.when(j == 0)
    def _():
        gated = _conv3(x1_ref[0], w1_ref[...], b1_ref[...]) * _conv3(v_ref[0], wv_ref[...], bv_ref[...])
        gated_ref[...] = gated
        gbf_ref[...] = gated.astype(BF16)
        acc_ref[...] = jnp.zeros_like(acc_ref)

    u = _dot(f_ref[...], gbf_ref[...])
    ur, ui = u[:tf], u[tf:]
    hr, hi = hf_ref[:tf, :], hf_ref[tf:, :]
    packed = (lax.broadcasted_iota(jnp.int32, ur.shape, 0) == 0) & (j == 0)
    yr = ur * hr - jnp.where(packed, 0.0, ui * hi)
    yi = jnp.where(packed, ui * hi, ur * hi + ui * hr)
    y = jnp.concatenate([yr, yi], axis=0).astype(BF16)
    acc_ref[...] += _dot(g_ref[0], y)

    @pl.when(j == n_tiles - 1)
    def _():
        gated = gated_ref[...]
        y = acc_ref[...] + gated * bias_ref[...]
        o_ref[0] = (_conv3(x0_ref[0], w0_ref[...], b0_ref[...]) * y).astype(BF16)


def hyena_mixer(zh, filt, short_w, short_b, bias):
    bsz, length, _ = zh.shape
    fwd, inv = _dft_matrices(length)
    tf = _dft_tile(length)
    n_tiles = length // tf
    hf = project(fwd, filt, tn=512, tm=min(512, 2 * length))
    cn = 256 if length > 512 else 512
    nct = D_HALF // cn

    def zblk(k):
        return pl.BlockSpec((1, length, cn), lambda b, c, j, k=k: (b, 0, k * nct + c))

    def wblk(k):
        return pl.BlockSpec((3, cn), lambda b, c, j, k=k: (0, k * nct + c))

    def bblk(k):
        return pl.BlockSpec((1, cn), lambda b, c, j, k=k: (0, k * nct + c))

    sb = short_b.reshape(1, -1)
    return pl.pallas_call(
        functools.partial(_hyena_conv_kernel, n_tiles=n_tiles, tf=tf),
        grid=(bsz, nct, n_tiles),
        in_specs=[zblk(0), zblk(1), zblk(2), wblk(0), wblk(1), wblk(2), bblk(0), bblk(1), bblk(2),
                  pl.BlockSpec((1, cn), lambda b, c, j: (0, c)),
                  pl.BlockSpec((2 * tf, length), lambda b, c, j: (j, 0)),
                  pl.BlockSpec((1, length, 2 * tf), lambda b, c, j: (j, 0, 0)),
                  pl.BlockSpec((2 * tf, cn), lambda b, c, j: (j, c))],
        out_specs=pl.BlockSpec((1, length, cn), lambda b, c, j: (b, 0, c)),
        out_shape=jax.ShapeDtypeStruct((bsz, length, D_HALF), BF16),
        scratch_shapes=[pltpu.VMEM((length, cn), F32), pltpu.VMEM((length, cn), BF16), pltpu.VMEM((length, cn), F32)],
        compiler_params=_params(3),
        name="hyena_conv",
    )(zh, zh, zh, short_w, short_w, short_w, sb, sb, sb, bias.reshape(1, -1), fwd, inv, hf)


RW_TL = 128
RW_PRE_TL = 128
RW_NB = 4
RW_PAIRS = RW_HEADS // 2
RW_YBLK = 64
RW_GROUP = 8
LANES = 128


def _block_ones():
    idx = np.arange(LANES) // RW_HD
    return (idx[:, None] == idx[None, :]).astype(np.float32)


def _segment_ones():
    return jnp.asarray(np.concatenate([_block_ones(), _block_ones()], axis=0), BF16)


def _segment_ones_pair():
    return jnp.asarray(np.kron(np.eye(2, dtype=np.float32), _block_ones()), BF16)


def _hi_lo(x):
    hi, lo = _split_bf16(x)
    return jnp.concatenate([hi, lo], axis=1)


def _head_sums(x, seg_ref):
    tiles = [_dot(_hi_lo(x[:, j * LANES:(j + 1) * LANES]), seg_ref[...]) for j in range(x.shape[1] // LANES)]
    return jnp.concatenate(tiles, axis=1)


def _rwkv_pre_kernel(z_ref, zp_ref, zn_ref, mu_ref, wl_ref, w0_ref, a0_ref, kk_ref, ka_ref, seg_ref,
                     r_ref, k_ref, v_ref, g_ref, an_ref, w0o_ref, kd0_ref, b0_ref, w1o_ref, kd1_ref, b1_ref,
                     *, n_tiles):
    i = pl.program_id(1)
    z = z_ref[0]
    row = lax.broadcasted_iota(jnp.int32, z.shape, 0)
    prev = jnp.where(i > 0, zp_ref[0, 7:8, :], 0.0)
    nxt = jnp.where(i < n_tiles - 1, zn_ref[0, 0:1, :], 0.0)
    zm1 = jnp.where(row == 0, prev, pltpu.roll(z, 1, 0))
    zp1 = jnp.where(row == z.shape[0] - 1, nxt, pltpu.roll(z, z.shape[0] - 1, 0))
    x = z + (0.5 * (zm1 + zp1) - z) * mu_ref[...]
    r = x[:, 0:D_HALF]
    k = x[:, D_HALF:2 * D_HALF]
    v = x[:, 2 * D_HALF:3 * D_HALF]
    lo = x[:, 3 * D_HALF:3 * D_HALF + RW_LORA]
    lane = lax.broadcasted_iota(jnp.int32, lo.shape, 1)
    act = jnp.where(lane < RW_W_RANK, jnp.tanh(lo),
                    jnp.where(lane < RW_W_RANK + RW_A_RANK, lo, jax.nn.sigmoid(lo)))
    up = _dot(act.astype(BF16), wl_ref[...])
    g = up[:, 4 * D_HALF:5 * D_HALF]
    kk = k * kk_ref[...]
    kk = kk * lax.rsqrt(_head_sums(kk * kk, seg_ref) + EPS)

    def put(ref, val):
        ref[0] = val

    put(r_ref, r)
    put(k_ref, k)
    put(v_ref, v)
    put(g_ref, g)
    put(an_ref, -kk)
    for d, (wo, kdo, bo) in enumerate(((w0o_ref, kd0_ref, b0_ref), (w1o_ref, kd1_ref, b1_ref))):
        logw = -jax.nn.softplus(-(w0_ref[d:d + 1, :] + up[:, d * D_HALF:(d + 1) * D_HALF])) - 0.5
        a = jax.nn.sigmoid(a0_ref[d:d + 1, :] + up[:, (2 + d) * D_HALF:(3 + d) * D_HALF])
        put(wo, jnp.exp(-jnp.exp(logw)))
        put(kdo, k * (1.0 + (a - 1.0) * ka_ref[...]))
        put(bo, kk * a)


def rwkv_pre(zr, mu, w_lora, w0, a0, k_k, k_a):
    bsz, length, width = zr.shape
    tl = RW_PRE_TL
    n_tiles = length // tl
    vec = lambda n: pl.BlockSpec((n, D_HALF), lambda b, i: (0, 0))
    out_blk = pl.BlockSpec((1, tl, D_HALF), lambda b, i: (b, i, 0))
    out_sds = jax.ShapeDtypeStruct((bsz, length, D_HALF), F32)
    return pl.pallas_call(
        functools.partial(_rwkv_pre_kernel, n_tiles=n_tiles),
        grid=(bsz, n_tiles),
        in_specs=[pl.BlockSpec((1, tl, width), lambda b, i: (b, i, 0)),
                  pl.BlockSpec((1, 8, width), lambda b, i: (b, jnp.maximum(i * (tl // 8) - 1, 0), 0)),
                  pl.BlockSpec((1, 8, width), lambda b, i: (b, jnp.minimum((i + 1) * (tl // 8), length // 8 - 1), 0)),
                  pl.BlockSpec((1, width), lambda b, i: (0, 0)),
                  pl.BlockSpec((RW_LORA, 5 * D_HALF), lambda b, i: (0, 0)),
                  vec(2), vec(2), vec(1), vec(1),
                  pl.BlockSpec((2 * LANES, LANES), lambda b, i: (0, 0))],
        out_specs=[out_blk] * 11,
        out_shape=[out_sds] * 11,
        compiler_params=_params(2),
        name="rwkv_pre",
    )(zr, zr, zr, mu, w_lora, w0, a0, k_k.reshape(1, -1), k_a.reshape(1, -1), _segment_ones())


def _rwkv_scan_kernel(r_ref, w_ref, k_ref, a_ref, b_ref, v_ref, s0_ref, seg_ref, seg2_ref, y_ref, fin_ref,
                      st_ref, lhs_ref, vlhs_ref, ylhs_ref, yacc_ref, *, reverse, n_chunks):
    c = pl.program_id(1)

    @pl.when(c == 0)
    def _():
        st_ref[...] = s0_ref[...]

    lane = lax.broadcasted_iota(jnp.int32, (RW_HD, LANES), 1)
    lane_in_head = lane % RW_HD
    diag = jnp.where(lane_in_head == lax.broadcasted_iota(jnp.int32, (RW_HD, LANES), 0), 1.0, 0.0)
    nt = RW_PAIRS
    chains = range(RW_NB)
    n_blocks = RW_TL // RW_YBLK

    def row(ref, cn, t, p):
        return ref[cn, pl.ds(t, 1), :][:, p * LANES:(p + 1) * LANES]

    def tile(sums, q):
        return sums[q * RW_HD:(q + 1) * RW_HD]

    def two_sums(tiles_ref, cn, lo, hi):
        return _dot(tiles_ref[cn, lo:hi].reshape((hi - lo) * RW_HD, 2 * LANES), seg2_ref[...])

    def step(t, u, vcol):
        for cn in chains:
            for p in range(nt):
                lhs_ref[cn, p] = _hi_lo(st_ref[cn, p] * row(a_ref, cn, t, p))
        sa = [_dot(lhs_ref[cn].reshape(nt * RW_HD, 2 * LANES), seg_ref[...]) for cn in chains]
        for cn in chains:
            for p in range(nt):
                s = (st_ref[cn, p] * row(w_ref, cn, t, p) + tile(sa[cn], p) * row(b_ref, cn, t, p)
                     + vcol[cn][p] * row(k_ref, cn, t, p))
                st_ref[cn, p] = s
                sr = (s * row(r_ref, cn, t, p)).astype(BF16)
                ylhs_ref[cn, p // 2, :, (p % 2) * LANES:(p % 2 + 1) * LANES] = sr
        ys = [two_sums(ylhs_ref, cn, 0, nt // 2) for cn in chains]
        hit = lane_in_head == t % RW_YBLK
        for cn in chains:
            for p in range(nt):
                y = tile(ys[cn], p // 2)[:, (p % 2) * LANES:(p % 2 + 1) * LANES]
                yacc_ref[cn, p] = jnp.where(hit, y, yacc_ref[cn, p])

    def step_pair(t0, t1):
        for cn in chains:
            for p in range(nt):
                vlhs_ref[cn, p] = jnp.concatenate([(diag * row(v_ref, cn, t0, p)).astype(BF16),
                                                   (diag * row(v_ref, cn, t1, p)).astype(BF16)], axis=1)
        vs = [two_sums(vlhs_ref, cn, 0, nt) for cn in chains]
        for u, t in enumerate((t0, t1)):
            step(t, u, [[tile(vs[cn], p)[:, u * LANES:(u + 1) * LANES] for p in range(nt)] for cn in chains])

    def step_group(g, blk):
        i0 = blk * RW_YBLK + g * RW_GROUP
        base = pl.multiple_of((RW_TL - RW_GROUP - i0) if reverse else i0, RW_GROUP)
        order = range(RW_GROUP - 1, -1, -1) if reverse else range(RW_GROUP)
        ts = [base + j for j in order]
        for u in range(0, RW_GROUP, 2):
            step_pair(ts[u], ts[u + 1])
        return blk

    def block(bi, carry):
        yacc_ref[...] = jnp.zeros_like(yacc_ref)
        lax.fori_loop(0, RW_YBLK // RW_GROUP, step_group, bi)
        tb = (n_blocks - 1 - bi) if reverse else bi
        rows = pl.ds(pl.multiple_of(tb * RW_YBLK, RW_YBLK), RW_YBLK)
        for cn in chains:
            for p in range(nt):
                yt = yacc_ref[cn, p].T
                for hh in range(2):
                    h = 2 * p + hh
                    y_ref[cn, rows, h * RW_HD:(h + 1) * RW_HD] = yt[hh * RW_YBLK:(hh + 1) * RW_YBLK]
        return carry

    lax.fori_loop(0, n_blocks, block, 0)

    @pl.when(c == n_chunks - 1)
    def _():
        fin_ref[...] = st_ref[...]


def rwkv_scan(r, w, k, a, b, v, s0, reverse):
    bsz, length, n = r.shape
    n_chunks = length // RW_TL

    def chunk(c):
        return (n_chunks - 1 - c) if reverse else c

    row_blk = pl.BlockSpec((RW_NB, RW_TL, n), lambda bi, c: (bi, chunk(c), 0))
    st_blk = pl.BlockSpec((RW_NB, RW_PAIRS, RW_HD, LANES), lambda bi, c: (bi, 0, 0, 0))
    tiles = lambda m: pltpu.VMEM((RW_NB, m, RW_HD, 2 * LANES), BF16)
    return pl.pallas_call(
        functools.partial(_rwkv_scan_kernel, reverse=reverse, n_chunks=n_chunks),
        grid=(bsz // RW_NB, n_chunks),
        in_specs=[row_blk] * 6 + [st_blk, pl.BlockSpec((2 * LANES, LANES), lambda bi, c: (0, 0)),
                                  pl.BlockSpec((2 * LANES, 2 * LANES), lambda bi, c: (0, 0))],
        out_specs=[row_blk, st_blk],
        out_shape=[jax.ShapeDtypeStruct((bsz, length, n), F32),
                   jax.ShapeDtypeStruct((bsz, RW_PAIRS, RW_HD, LANES), F32)],
        scratch_shapes=[pltpu.VMEM((RW_NB, RW_PAIRS, RW_HD, LANES), F32),
                        tiles(RW_PAIRS), tiles(RW_PAIRS), tiles(RW_PAIRS // 2),
                        pltpu.VMEM((RW_NB, RW_PAIRS, RW_HD, LANES), F32)],
        compiler_params=_params(2),
        name="rwkv_scan_rev" if reverse else "rwkv_scan_fwd",
    )(r, w, k, a, b, v, s0, _segment_ones(), _segment_ones_pair())


def _rwkv_post_kernel(yf_ref, yb_ref, r_ref, k_ref, v_ref, g_ref, rk_ref, lg_ref, lb_ref, seg_ref, o_ref):
    y = yf_ref[0] + yb_ref[0]
    mean = _head_sums(y, seg_ref) * (1.0 / RW_HD)
    yc = y - mean
    var = _head_sums(yc * yc, seg_ref) * (1.0 / RW_HD)
    y = yc * lax.rsqrt(var + GN_EPS) * lg_ref[...] + lb_ref[...]
    bonus = _head_sums(r_ref[0] * k_ref[0] * rk_ref[...], seg_ref) * v_ref[0]
    o_ref[0] = ((y + bonus) * g_ref[0]).astype(BF16)


def rwkv_post(yf, yb, r, k, v, g, r_k, ln_g, ln_b):
    bsz, length, n = r.shape
    tl = RW_TL
    blk = pl.BlockSpec((1, tl, n), lambda b, i: (b, i, 0))
    vec = pl.BlockSpec((1, n), lambda b, i: (0, 0))
    return pl.pallas_call(
        _rwkv_post_kernel,
        grid=(bsz, length // tl),
        in_specs=[blk] * 6 + [vec] * 3 + [pl.BlockSpec((2 * LANES, LANES), lambda b, i: (0, 0))],
        out_specs=blk,
        out_shape=jax.ShapeDtypeStruct((bsz, length, n), BF16),
        compiler_params=_params(2),
        name="rwkv_post",
    )(yf, yb, r, k, v, g, r_k.reshape(1, n), ln_g.reshape(1, n), ln_b.reshape(1, n), _segment_ones())


def rwkv_lora_weights(w2, a2, g2):
    w = jnp.zeros((RW_LORA, 5 * D_HALF), F32)
    for d in range(2):
        w = w.at[0:RW_W_RANK, d * D_HALF:(d + 1) * D_HALF].set(w2[d].astype(F32))
        w = w.at[RW_W_RANK:RW_W_RANK + RW_A_RANK, (2 + d) * D_HALF:(3 + d) * D_HALF].set(a2[d].astype(F32))
    w = w.at[RW_W_RANK + RW_A_RANK:, 4 * D_HALF:].set(g2.astype(F32))
    return w.astype(BF16)


def rwkv_mixer(zr, s0, rwp):
    mu, w0, w2, a0, a2, g2, k_k, k_a, r_k, ln_g, ln_b = rwp
    bsz, length = zr.shape[:2]
    mu_p = jnp.pad(mu.astype(F32), (0, RW_IN_PAD - RW_IN)).reshape(1, RW_IN_PAD)
    r, k, v, g, an, wd0, kd0, b0, wd1, kd1, b1 = rwkv_pre(zr, mu_p, rwkv_lora_weights(w2, a2, g2), w0, a0, k_k, k_a)
    if s0 is None:
        s0 = jnp.zeros((bsz, 2, RW_HEADS, RW_HD, RW_HD), F32)

    def pack_state(s):
        return s.astype(F32).reshape(bsz, RW_PAIRS, 2, RW_HD, RW_HD).transpose(0, 1, 3, 2, 4).reshape(
            bsz, RW_PAIRS, RW_HD, LANES)

    def unpack_state(s):
        return s.reshape(bsz, RW_PAIRS, RW_HD, 2, RW_HD).transpose(0, 1, 3, 2, 4).reshape(bsz, RW_HEADS, RW_HD, RW_HD)

    yf, fin_f = rwkv_scan(r, wd0, kd0, an, b0, v, pack_state(s0[:, 0]), reverse=False)
    yb, fin_b = rwkv_scan(r, wd1, kd1, an, b1, v, pack_state(s0[:, 1]), reverse=True)
    y = rwkv_post(yf, yb, r, k, v, g, r_k, ln_g, ln_b)
    return y, jnp.stack([unpack_state(fin_f), unpack_state(fin_b)], axis=1)


ROUTE_TILE = 512


def _rank_before(vals, n):
    idx = lax.broadcasted_iota(jnp.int32, vals.shape, 0)
    cnt = jnp.zeros(vals.shape, F32)
    for e in range(n):
        row = vals[e:e + 1]
        cnt = cnt + jnp.where((row > vals) | ((row == vals) & (idx > e)), 1.0, 0.0)
    return cnt


def _route_kernel(lg_ref, bias_ref, tril_ref, triu_ref, te_ref, gt_ref, rk_ref, cnt_ref, carry_ref):
    @pl.when(pl.program_id(0) == 0)
    def _():
        carry_ref[...] = jnp.zeros_like(carry_ref)

    t = lg_ref.shape[1]
    per = N_EXPERTS // N_ROUTE_GROUPS
    scores = jax.nn.sigmoid(lg_ref[...])
    sel = scores + bias_ref[...]
    sel3 = sel.reshape(N_ROUTE_GROUPS, per, t)
    m1 = jnp.max(sel3, axis=1, keepdims=True)
    within = lax.broadcasted_iota(jnp.int32, sel3.shape, 1)
    first = jnp.min(jnp.where(sel3 == m1, within, per), axis=1, keepdims=True)
    m2 = jnp.max(jnp.where(within == first, -jnp.inf, sel3), axis=1, keepdims=True)
    group_score = (m1 + m2).reshape(N_ROUTE_GROUPS, t)
    group_ok = _rank_before(group_score, N_ROUTE_GROUPS) < TOPK_ROUTE_GROUPS
    expert_ok = jnp.broadcast_to(group_ok.reshape(N_ROUTE_GROUPS, 1, t), sel3.shape).reshape(N_EXPERTS, t)
    masked = jnp.where(expert_ok, sel, NEG_INF)
    chosen = _rank_before(masked, N_EXPERTS) < TOP_K
    gates = jnp.where(chosen, scores, 0.0)
    gates = gates / jnp.sum(gates, axis=0, keepdims=True) * ROUTED_SCALE
    onehot = jnp.where(chosen, 1.0, 0.0).astype(BF16)
    order = _dot(tril_ref[...], onehot)
    before = carry_ref[:, 0:1] + _dot(onehot, triu_ref[...])
    carry_ref[...] = carry_ref[...] + jnp.sum(jnp.where(chosen, 1.0, 0.0), axis=1, keepdims=True)
    eidx = lax.broadcasted_iota(jnp.int32, chosen.shape, 0).astype(F32)
    for j in range(TOP_K):
        pick = chosen & (order == float(j + 1))
        te_ref[j:j + 1, :] = jnp.sum(jnp.where(pick, eidx, 0.0), axis=0, keepdims=True).astype(jnp.int32)
        gt_ref[j:j + 1, :] = jnp.sum(jnp.where(pick, gates, 0.0), axis=0, keepdims=True)
        rk_ref[j:j + 1, :] = jnp.sum(jnp.where(pick, before, 0.0), axis=0, keepdims=True).astype(jnp.int32)
    cnt_ref[...] = carry_ref[...]


def route(logits_t, b_router):
    e, n = logits_t.shape
    t = ROUTE_TILE
    tril = jnp.asarray(np.tril(np.ones((e, e))), BF16)
    triu = jnp.asarray(np.triu(np.ones((t, t)), 1), BF16)
    out_blk = pl.BlockSpec((TOP_K, t), lambda i: (0, i))
    te, gt, rk, cnt = pl.pallas_call(
        _route_kernel,
        grid=(n // t,),
        in_specs=[pl.BlockSpec((e, t), lambda i: (0, i)),
                  pl.BlockSpec((e, 1), lambda i: (0, 0)),
                  pl.BlockSpec((e, e), lambda i: (0, 0)),
                  pl.BlockSpec((t, t), lambda i: (0, 0))],
        out_specs=[out_blk, out_blk, out_blk, pl.BlockSpec((e, 128), lambda i: (0, 0))],
        out_shape=[jax.ShapeDtypeStruct((TOP_K, n), jnp.int32), jax.ShapeDtypeStruct((TOP_K, n), F32),
                   jax.ShapeDtypeStruct((TOP_K, n), jnp.int32), jax.ShapeDtypeStruct((e, 128), F32)],
        scratch_shapes=[pltpu.VMEM((e, 128), F32)],
        compiler_params=_params(1),
        name="route",
    )(logits_t, b_router.astype(F32).reshape(e, 1), tril, triu)
    return te, gt, rk, cnt[:, 0]


def _expert_kernel(be_ref, nu_ref, x_ref, wg_ref, wu_ref, wd_ref, o_ref, wgb_ref, wub_ref, wdb_ref):
    i = pl.program_id(0)
    used = i < nu_ref[0]
    fresh = (i == 0) | (be_ref[i] != be_ref[jnp.maximum(i - 1, 0)])

    @pl.when(used & fresh)
    def _():
        wgb_ref[...] = wg_ref[0, 0].astype(BF16)
        wub_ref[...] = wu_ref[0, 0].astype(BF16)
        wdb_ref[...] = wd_ref[0, 0].astype(BF16)

    @pl.when(used)
    def _():
        x = x_ref[...]
        gate = _dot(x, wgb_ref[...])
        h = (gate * jax.nn.sigmoid(gate) * _dot(x, wub_ref[...])).astype(BF16)
        o_ref[...] = _dot(h, wdb_ref[...]).astype(o_ref.dtype)

    @pl.when(jnp.logical_not(used))
    def _():
        o_ref[...] = jnp.zeros_like(o_ref)


def expert_ffn(xs, block_expert, n_used, w_gate, w_up, w_down, layer):
    s, d = xs.shape
    de = w_gate.shape[3]
    nb = s // MOE_ROWS
    grid_spec = pltpu.PrefetchScalarGridSpec(
        num_scalar_prefetch=2,
        grid=(nb,),
        in_specs=[pl.BlockSpec((MOE_ROWS, d), lambda i, be, nu: (i, 0)),
                  pl.BlockSpec((1, 1, d, de), lambda i, be, nu: (layer, be[i], 0, 0)),
                  pl.BlockSpec((1, 1, d, de), lambda i, be, nu: (layer, be[i], 0, 0)),
                  pl.BlockSpec((1, 1, de, d), lambda i, be, nu: (layer, be[i], 0, 0))],
        out_specs=pl.BlockSpec((MOE_ROWS, d), lambda i, be, nu: (i, 0)),
        scratch_shapes=[pltpu.VMEM((d, de), BF16), pltpu.VMEM((d, de), BF16), pltpu.VMEM((de, d), BF16)],
    )
    return pl.pallas_call(
        _expert_kernel,
        grid_spec=grid_spec,
        out_shape=jax.ShapeDtypeStruct((s, d), BF16),
        compiler_params=_params(1),
        name="expert_ffn",
    )(block_expert, n_used, xs, w_gate, w_up, w_down)


def _combine_kernel(x_ref, yg_ref, gt_ref, sh_ref, mod_ref, o_ref, *, gate_idx):
    routed = jnp.zeros(x_ref.shape, F32)
    for j in range(TOP_K):
        routed = routed + gt_ref[:, j:j + 1] * yg_ref[j].astype(F32)
    g = mod_ref[0][gate_idx:gate_idx + 1]
    o_ref[...] = x_ref[...] + g * (routed + sh_ref[...].astype(F32))


def moe_combine(x, yg, gates, shared, mods, gate_idx, n_ctx, tm=128):
    n, d = x.shape
    row = functools.partial(_cond_row, n_ctx_tiles=n_ctx // tm, tiles_per_sample=DEC_SEQ // tm)
    return pl.pallas_call(
        functools.partial(_combine_kernel, gate_idx=gate_idx),
        grid=(n // tm,),
        in_specs=[pl.BlockSpec((tm, d), lambda i: (i, 0)),
                  pl.BlockSpec((TOP_K, tm, d), lambda i: (0, i, 0)),
                  pl.BlockSpec((tm, TOP_K), lambda i: (i, 0)),
                  pl.BlockSpec((tm, d), lambda i: (i, 0)),
                  pl.BlockSpec((1, 6, d), lambda i: (row(i), 0, 0))],
        out_specs=pl.BlockSpec((tm, d), lambda i: (i, 0)),
        out_shape=jax.ShapeDtypeStruct((n, d), F32),
        compiler_params=_params(1),
        name="moe_combine",
    )(x, yg, gates, shared, mods)


def moe_ffn(x, h, logits_t, mods, gate_idx, n_ctx, layer, b_router, w_gate, w_up, w_down, ws_gate, ws_up, ws_down):
    n = h.shape[0]
    top_e, gates, rank, counts = route(logits_t, b_router)
    counts = counts.astype(jnp.int32)
    padded = (counts + MOE_ROWS - 1) // MOE_ROWS * MOE_ROWS
    pad_end = jnp.cumsum(padded)
    pad_start = pad_end - padded
    n_slots = n * TOP_K + N_EXPERTS * MOE_ROWS
    nb = n_slots // MOE_ROWS
    expert_ids = jnp.arange(N_EXPERTS, dtype=jnp.int32)
    start_of = jnp.sum(jnp.where(top_e[..., None] == expert_ids, pad_start, 0), axis=-1)
    pos = start_of + rank
    tok = jnp.broadcast_to(jnp.arange(n, dtype=jnp.int32)[None, :], pos.shape)
    slot_tok = (jnp.arange(n_slots, dtype=jnp.int32) % n).at[pos.reshape(-1)].set(tok.reshape(-1), unique_indices=True)
    block_start = jnp.arange(nb, dtype=jnp.int32) * MOE_ROWS
    block_expert = jnp.minimum(jnp.sum(pad_end[None, :] <= block_start[:, None], axis=1), N_EXPERTS - 1).astype(jnp.int32)
    n_used = (pad_end[-1:] // MOE_ROWS).astype(jnp.int32)
    xs = h[slot_tok]
    yb = expert_ffn(xs, block_expert, n_used, w_gate, w_up, w_down, layer)
    shared = expert_ffn(h, jnp.zeros((n // MOE_ROWS,), jnp.int32), jnp.full((1,), n // MOE_ROWS, jnp.int32),
                        ws_gate[:, None], ws_up[:, None], ws_down[:, None], layer)
    yg = yb[pos]
    return moe_combine(x, yg, gates.T, shared, mods, gate_idx, n_ctx)


def _layer_ab(h, n_ctx, bsz_ctx, bsz_lat, p):
    z = project(h, p['w_in'], tn=1024)
    s5w = s5_weights(*p['s5_disc'])
    outs = []
    extras = {}
    for name, lo, bsz, length in (('ctx', 0, bsz_ctx, SEQ), ('lat', n_ctx, bsz_lat, DEC_SEQ)):
        zg = z[lo:lo + bsz * length].reshape(bsz, length, 4 * D_HALF)
        u = zg[..., :D_HALF]
        q, k, v = (_heads_first(zg[..., (1 + j) * D_HALF:(2 + j) * D_HALF], bsz, length, NA_HEADS) for j in range(3))
        if name == 'ctx':
            y_a, s5_fin = s5_mixer(u, None, p['s5_glu'], s5w)
            y_b, k_n = context_attention(q, k, v, p['q_g'], p['k_g'])
            extras = {'k': k_n, 'v': v, 's5': s5_fin}
        else:
            y_a, _ = s5_mixer(u, p['s5_state'], p['s5_glu'], s5w)
            y_b = neighbourhood_attention(q, k, v, p['ck'], p['cv'], p['rpb'], p['q_g'], p['k_g'])
        outs.append(jnp.concatenate([y_a, y_b], axis=-1).reshape(bsz * length, 2 * D_HALF))
    return jnp.concatenate(outs, axis=0), extras


def _layer_cd(h, n_ctx, bsz_ctx, bsz_lat, p):
    w_in = p['w_in']
    zh_all = project(h, w_in[:, :3 * D_HALF], tn=1024)
    w_rw = jnp.pad(w_in[:, 3 * D_HALF:], ((0, 0), (0, RW_IN_PAD - RW_IN)))
    zr_all = project(h, w_rw, tn=RW_IN_PAD // 4)
    outs = []
    extras = {}
    for name, lo, bsz, length in (('ctx', 0, bsz_ctx, SEQ), ('lat', n_ctx, bsz_lat, DEC_SEQ)):
        zh = zh_all[lo:lo + bsz * length].reshape(bsz, length, 3 * D_HALF)
        zr = zr_all[lo:lo + bsz * length].reshape(bsz, length, RW_IN_PAD)
        filt = hyena_filter(length, *p['hy_filter'])
        y_c = hyena_mixer(zh, filt, p['hy_short_w'], p['hy_short_b'], p['hy_bias'])
        y_d, rw_fin = rwkv_mixer(zr, None if name == 'ctx' else p['rw_state'], p['rwp'])
        if name == 'ctx':
            extras = {'rw': rw_fin}
        outs.append(jnp.concatenate([y_c, y_d], axis=-1).reshape(bsz * length, 2 * D_HALF))
    return jnp.concatenate(outs, axis=0), extras


def kernel(x_prompt, x_sample, cache_na_k, cache_na_v, state_s5, state_rwkv, c, c_ctx, ada_w, ada_b, norm_mix, norm_ffn, ab_w_in, ab_w_out, s5_lam_re, s5_lam_im, s5_log_step, s5_b_re, s5_b_im, s5_c_re, s5_c_im, s5_d, s5_w_glu, s5_b_glu, na_q_norm, na_k_norm, na_rpb, cd_w_in, cd_w_out, hy_short_w, hy_short_b, hy_w1, hy_b1, hy_freq, hy_w2, hy_b2, hy_w3, hy_decay, hy_bias, rw_mu, rw_w0, rw_w2, rw_a0, rw_a2, rw_g2, rw_k_k, rw_k_a, rw_r_k, rw_ln_g, rw_ln_b, moe_router, moe_router_bias, moe_w_gate, moe_w_up, moe_w_down, moe_ws_gate, moe_ws_up, moe_ws_down):
    bsz_ctx, seq, d = x_prompt.shape
    bsz_lat, dec_seq, _ = x_sample.shape
    assert (seq, dec_seq, d) == (SEQ, DEC_SEQ, D_MODEL) and bsz_lat + 1 <= 8
    depth = ada_w.shape[0]
    n_ctx = bsz_ctx * seq
    x = jnp.concatenate([x_prompt.reshape(n_ctx, d), x_sample.reshape(bsz_lat * dec_seq, d)], axis=0).astype(F32)

    cond = jnp.concatenate([c_ctx[None, :], c, jnp.zeros((8 - 1 - bsz_lat, d), c.dtype)], axis=0).astype(F32)
    mods_all = ada_table(cond, ada_w, ada_b).reshape(depth, 8, 6, d)

    new_k, new_v, new_s5, new_rw = [], [], [], []
    for l in range(depth):
        mods = mods_all[l]
        i = l // 2
        h = modulate(x, norm_mix[l], mods, 0, 1, n_ctx)
        if l % 2 == 0:
            p = {'w_in': ab_w_in[i], 'q_g': na_q_norm[i], 'k_g': na_k_norm[i], 'rpb': na_rpb[i],
                 'ck': cache_na_k[:, i], 'cv': cache_na_v[:, i], 's5_state': state_s5[:, i],
                 's5_disc': (s5_lam_re[i], s5_lam_im[i], s5_log_step[i], s5_b_re[i], s5_b_im[i], s5_c_re[i], s5_c_im[i]),
                 's5_glu': (s5_d[i], s5_w_glu[i], s5_b_glu[i])}
            y, ex = _layer_ab(h, n_ctx, bsz_ctx, bsz_lat, p)
            new_k.append(ex['k'])
            new_v.append(ex['v'])
            new_s5.append(ex['s5'])
            w_out = ab_w_out[i]
        else:
            p = {'w_in': cd_w_in[i], 'hy_filter': (hy_w1[i], hy_b1[i], hy_freq[i], hy_w2[i], hy_b2[i], hy_w3[i], hy_decay[i]),
                 'hy_short_w': hy_short_w[i], 'hy_short_b': hy_short_b[i], 'hy_bias': hy_bias[i],
                 'rw_state': state_rwkv[:, i],
                 'rwp': (rw_mu[i], rw_w0[i], rw_w2[i], rw_a0[i], rw_a2[i], rw_g2[i], rw_k_k[i], rw_k_a[i], rw_r_k[i],
                         rw_ln_g[i], rw_ln_b[i])}
            y, ex = _layer_cd(h, n_ctx, bsz_ctx, bsz_lat, p)
            new_rw.append(ex['rw'])
            w_out = cd_w_out[i]
        x = project_residual(y, w_out, x, mods, 2, n_ctx)
        h, logits_t = modulate(x, norm_ffn[l], mods, 3, 4, n_ctx, w_router_t=moe_router[l].T)
        x = moe_ffn(x, h, logits_t, mods, 5, n_ctx, l, moe_router_bias[l], moe_w_gate, moe_w_up, moe_w_down,
                    moe_ws_gate, moe_ws_up, moe_ws_down)

    y_prompt = x[:n_ctx].reshape(bsz_ctx, seq, d)
    y_sample = x[n_ctx:].reshape(bsz_lat, dec_seq, d)
    return (y_prompt, y_sample, jnp.stack(new_k, axis=1), jnp.stack(new_v, axis=1),
            jnp.stack(new_s5, axis=1), jnp.stack(new_rw, axis=1))
```

```python
import functools
import math

import numpy as np
import jax
import jax.numpy as jnp
from jax import lax
from jax.experimental import pallas as pl
from jax.experimental.pallas import tpu as pltpu

F32 = jnp.float32
BF16 = jnp.bfloat16

D_MODEL = 2048
D_HALF = 1024
SEQ = 256
DEC_SEQ = 2048
GRID_W = 64
S5_GROUP = 16
S5_GROUPS = 64
S5_STATE = 64
S5_LANES = S5_GROUPS * S5_STATE
NA_HEADS = 16
NA_HD = 64
NA_KH = 8
NA_KW = 16
HY_BANDS = 16
RW_HEADS = 16
RW_HD = 64
RW_W_RANK = 64
RW_A_RANK = 64
RW_G_RANK = 128
RW_LORA = 256
RW_IN = 3 * D_HALF + RW_LORA
RW_IN_PAD = 3584
N_EXPERTS = 64
TOP_K = 8
N_ROUTE_GROUPS = 8
TOPK_ROUTE_GROUPS = 4
D_EXPERT = 512
ROUTED_SCALE = 2.5
EPS = 1e-6
GN_EPS = 64e-5
NEG_INF = -1e30

TOK_TILE = 256
MOE_ROWS = 512
VMEM_LIMIT = 56 * 1024 * 1024


def _params(n_axes, vmem=VMEM_LIMIT):
    return pltpu.CompilerParams(dimension_semantics=("arbitrary",) * n_axes, vmem_limit_bytes=vmem)


def _dot(a, b):
    return jnp.dot(a, b, preferred_element_type=F32)


def _dot_nt(a, b):
    return lax.dot_general(a, b, (((1,), (1,)), ((), ())), preferred_element_type=F32)


def _split_bf16(x):
    hi = x.astype(BF16)
    lo = (x - hi.astype(F32)).astype(BF16)
    return hi, lo


def _cond_row(i, n_ctx_tiles, tiles_per_sample):
    return jnp.where(i < n_ctx_tiles, 0, 1 + (i - n_ctx_tiles) // tiles_per_sample)


def _ada_kernel(c_ref, w_ref, b_ref, o_ref):
    c = c_ref[...]
    s = (c * jax.nn.sigmoid(c)).astype(BF16)
    o_ref[0] = _dot(s, w_ref[0].astype(BF16)) + b_ref[0]


def ada_table(cond8, ada_w, ada_b):
    depth, d, n = ada_w.shape
    tn = 1024
    return pl.pallas_call(
        _ada_kernel,
        grid=(depth, n // tn),
        in_specs=[pl.BlockSpec((8, d), lambda l, j: (0, 0)),
                  pl.BlockSpec((1, d, tn), lambda l, j: (l, 0, j)),
                  pl.BlockSpec((1, 1, tn), lambda l, j: (l, 0, j))],
        out_specs=pl.BlockSpec((1, 8, tn), lambda l, j: (l, 0, j)),
        out_shape=jax.ShapeDtypeStruct((depth, 8, n), F32),
        compiler_params=_params(2),
        name="ada_table",
    )(cond8, ada_w, ada_b.reshape(depth, 1, n))


def _modulate_kernel(x_ref, g_ref, mod_ref, h_ref, *, shift_idx, scale_idx):
    x = x_ref[...]
    y = x * lax.rsqrt(jnp.mean(x * x, axis=-1, keepdims=True) + EPS)
    m = mod_ref[0]
    h = y * g_ref[...] * (1.0 + m[scale_idx:scale_idx + 1]) + m[shift_idx:shift_idx + 1]
    h_ref[...] = h.astype(BF16)


def _modulate_router_kernel(x_ref, g_ref, mod_ref, wr_ref, h_ref, lg_ref, *, shift_idx, scale_idx):
    x = x_ref[...]
    y = x * lax.rsqrt(jnp.mean(x * x, axis=-1, keepdims=True) + EPS)
    m = mod_ref[0]
    h = y * g_ref[...] * (1.0 + m[scale_idx:scale_idx + 1]) + m[shift_idx:shift_idx + 1]
    h_hi, h_lo = _split_bf16(h)
    h_ref[...] = h_hi
    w_hi, w_lo = _split_bf16(wr_ref[...])
    lg_ref[...] = _dot_nt(w_hi, h_hi) + _dot_nt(w_hi, h_lo) + _dot_nt(w_lo, h_hi)


def modulate(x, gain, mods, shift_idx, scale_idx, n_ctx, w_router_t=None):
    n, d = x.shape
    tm = TOK_TILE
    row = functools.partial(_cond_row, n_ctx_tiles=n_ctx // tm, tiles_per_sample=DEC_SEQ // tm)
    in_specs = [pl.BlockSpec((tm, d), lambda i: (i, 0)),
                pl.BlockSpec((1, d), lambda i: (0, 0)),
                pl.BlockSpec((1, 6, d), lambda i: (row(i), 0, 0))]
    if w_router_t is None:
        return pl.pallas_call(
            functools.partial(_modulate_kernel, shift_idx=shift_idx, scale_idx=scale_idx),
            grid=(n // tm,),
            in_specs=in_specs,
            out_specs=pl.BlockSpec((tm, d), lambda i: (i, 0)),
            out_shape=jax.ShapeDtypeStruct((n, d), BF16),
            compiler_params=_params(1),
            name="modulate",
        )(x, gain.reshape(1, d), mods)
    e = w_router_t.shape[0]
    return pl.pallas_call(
        functools.partial(_modulate_router_kernel, shift_idx=shift_idx, scale_idx=scale_idx),
        grid=(n // tm,),
        in_specs=in_specs + [pl.BlockSpec((e, d), lambda i: (0, 0))],
        out_specs=[pl.BlockSpec((tm, d), lambda i: (i, 0)),
                   pl.BlockSpec((e, tm), lambda i: (0, i))],
        out_shape=[jax.ShapeDtypeStruct((n, d), BF16), jax.ShapeDtypeStruct((e, n), F32)],
        compiler_params=_params(1),
        name="modulate_router",
    )(x, gain.reshape(1, d), mods, w_router_t)


def _proj_kernel(x_ref, w_ref, o_ref, wbf_ref):
    @pl.when(pl.program_id(1) == 0)
    def _():
        wbf_ref[...] = w_ref[...].astype(BF16)

    o_ref[...] = _dot(x_ref[...], wbf_ref[...]).astype(o_ref.dtype)


def project(x, w, tn, tm=512, out_dtype=F32):
    m, k = x.shape
    n = w.shape[1]
    return pl.pallas_call(
        _proj_kernel,
        grid=(n // tn, m // tm),
        in_specs=[pl.BlockSpec((tm, k), lambda j, i: (i, 0)),
                  pl.BlockSpec((k, tn), lambda j, i: (0, j))],
        out_specs=pl.BlockSpec((tm, tn), lambda j, i: (i, j)),
        out_shape=jax.ShapeDtypeStruct((m, n), out_dtype),
        scratch_shapes=[pltpu.VMEM((k, tn), BF16)],
        compiler_params=_params(2),
        name="project",
    )(x, w)


def _proj_residual_kernel(y_ref, w_ref, x_ref, mod_ref, o_ref, wbf_ref, *, gate_idx):
    @pl.when(pl.program_id(1) == 0)
    def _():
        wbf_ref[...] = w_ref[...].astype(BF16)

    g = mod_ref[0][gate_idx:gate_idx + 1]
    o_ref[...] = x_ref[...] + g * _dot(y_ref[...], wbf_ref[...])


def project_residual(y, w, x, mods, gate_idx, n_ctx, tn=1024, tm=512):
    m, k = y.shape
    n = w.shape[1]
    row = functools.partial(_cond_row, n_ctx_tiles=n_ctx // tm, tiles_per_sample=DEC_SEQ // tm)
    return pl.pallas_call(
        functools.partial(_proj_residual_kernel, gate_idx=gate_idx),
        grid=(n // tn, m // tm),
        in_specs=[pl.BlockSpec((tm, k), lambda j, i: (i, 0)),
                  pl.BlockSpec((k, tn), lambda j, i: (0, j)),
                  pl.BlockSpec((tm, tn), lambda j, i: (i, j)),
                  pl.BlockSpec((1, 6, tn), lambda j, i: (row(i), 0, j))],
        out_specs=pl.BlockSpec((tm, tn), lambda j, i: (i, j)),
        out_shape=jax.ShapeDtypeStruct((m, n), F32),
        scratch_shapes=[pltpu.VMEM((k, tn), BF16)],
        compiler_params=_params(2),
        name="project_residual",
    )(y, w, x, mods)


S5_CHUNK = 64
S5_ROWS = 8
S5_BLK = 8
S5_SCAN_LANES = 1024


def _s5_kernel(u_ref, wbr_ref, wbi_ref, wcr_ref, wci_ref, lam_ref, x0_ref, y_ref, fin_ref,
               bur_ref, bui_ref, st_ref, *, n_chunks):
    d = pl.program_id(0)
    c = pl.program_id(2)
    tc = S5_CHUNK
    cin = S5_BLK * S5_GROUP
    cst = S5_BLK * S5_STATE

    @pl.when(c == 0)
    def _():
        st_ref[...] = x0_ref[0]

    u = u_ref[...].reshape(tc * S5_ROWS, D_HALF).astype(BF16)
    for k in range(S5_GROUPS // S5_BLK):
        uk = u[:, k * cin:(k + 1) * cin]
        bur_ref[:, k * cst:(k + 1) * cst] = _dot(uk, wbr_ref[0, k])
        bui_ref[:, k * cst:(k + 1) * cst] = _dot(uk, wbi_ref[0, k])

    for j in range(S5_LANES // S5_SCAN_LANES):
        sl = slice(j * S5_SCAN_LANES, (j + 1) * S5_SCAN_LANES)
        lr = jnp.broadcast_to(lam_ref[0, 0:1, sl], (S5_ROWS, S5_SCAN_LANES))
        li = jnp.broadcast_to(lam_ref[0, 1:2, sl], (S5_ROWS, S5_SCAN_LANES))

        def step(i, carry, sl=sl, lr=lr, li=li):
            sr, si = carry
            t = jnp.where(d == 0, i, tc - 1 - i)
            row = pl.multiple_of(t * S5_ROWS, S5_ROWS)
            nr = lr * sr - li * si + bur_ref[pl.ds(row, S5_ROWS), sl]
            ni = lr * si + li * sr + bui_ref[pl.ds(row, S5_ROWS), sl]
            bur_ref[pl.ds(row, S5_ROWS), sl] = nr
            bui_ref[pl.ds(row, S5_ROWS), sl] = ni
            return nr, ni

        sr, si = lax.fori_loop(0, tc, step, (st_ref[0, :, sl], st_ref[1, :, sl]), unroll=4)
        st_ref[0, :, sl] = sr
        st_ref[1, :, sl] = si

    xr = bur_ref[...].astype(BF16)
    xi = bui_ref[...].astype(BF16)
    for k in range(S5_GROUPS // S5_BLK):
        yk = _dot(xr[:, k * cst:(k + 1) * cst], wcr_ref[0, k]) + _dot(xi[:, k * cst:(k + 1) * cst], wci_ref[0, k])
        y_ref[0, :, :, k * cin:(k + 1) * cin] = yk.reshape(tc, S5_ROWS, cin)

    @pl.when(c == n_chunks - 1)
    def _():
        fin_ref[0] = st_ref[...]


def s5_scan(u_t, x0, wbr, wbi, wcr, wci, lam):
    length, bsz, _ = u_t.shape
    n_chunks = length // S5_CHUNK
    nblk = S5_GROUPS // S5_BLK
    cin = S5_BLK * S5_GROUP
    cst = S5_BLK * S5_STATE

    def chunk(d, c):
        return jnp.where(d == 0, c, n_chunks - 1 - c)

    return pl.pallas_call(
        functools.partial(_s5_kernel, n_chunks=n_chunks),
        grid=(2, bsz // S5_ROWS, n_chunks),
        in_specs=[pl.BlockSpec((S5_CHUNK, S5_ROWS, D_HALF), lambda d, b, c: (chunk(d, c), b, 0)),
                  pl.BlockSpec((1, nblk, cin, cst), lambda d, b, c: (d, 0, 0, 0)),
                  pl.BlockSpec((1, nblk, cin, cst), lambda d, b, c: (d, 0, 0, 0)),
                  pl.BlockSpec((1, nblk, cst, cin), lambda d, b, c: (d, 0, 0, 0)),
                  pl.BlockSpec((1, nblk, cst, cin), lambda d, b, c: (d, 0, 0, 0)),
                  pl.BlockSpec((1, 2, S5_LANES), lambda d, b, c: (d, 0, 0)),
                  pl.BlockSpec((1, 2, S5_ROWS, S5_LANES), lambda d, b, c: (d, 0, b, 0))],
        out_specs=[pl.BlockSpec((1, S5_CHUNK, S5_ROWS, D_HALF), lambda d, b, c: (d, chunk(d, c), b, 0)),
                   pl.BlockSpec((1, 2, S5_ROWS, S5_LANES), lambda d, b, c: (d, 0, b, 0))],
        out_shape=[jax.ShapeDtypeStruct((2, length, bsz, D_HALF), F32),
                   jax.ShapeDtypeStruct((2, 2, bsz, S5_LANES), F32)],
        scratch_shapes=[pltpu.VMEM((S5_CHUNK * S5_ROWS, S5_LANES), F32),
                        pltpu.VMEM((S5_CHUNK * S5_ROWS, S5_LANES), F32),
                        pltpu.VMEM((2, S5_ROWS, S5_LANES), F32)],
        compiler_params=_params(3),
        name="s5_scan",
    )(u_t, wbr, wbi, wcr, wci, lam, x0)


def s5_weights(lam_re, lam_im, log_step, b_re, b_im, c_re, c_im):
    lam = lax.complex(lam_re.astype(F32), lam_im.astype(F32))
    lam_bar = jnp.exp(lam * jnp.exp(log_step.astype(F32))[..., None])
    b_bar = ((lam_bar - 1.0) / lam)[..., None] * lax.complex(b_re.astype(F32), b_im.astype(F32))
    nblk = S5_GROUPS // S5_BLK
    eye = jnp.eye(S5_BLK, dtype=F32)

    def embed_in(b):
        b = b.reshape(2, nblk, S5_BLK, S5_STATE, S5_GROUP)
        w = jnp.einsum('dkgpc,gh->dkgchp', b, eye)
        return w.reshape(2, nblk, S5_BLK * S5_GROUP, S5_BLK * S5_STATE).astype(BF16)

    def embed_out(cm):
        cm = cm.reshape(2, nblk, S5_BLK, S5_GROUP, S5_STATE)
        w = jnp.einsum('dkgcp,gh->dkgphc', cm, eye)
        return w.reshape(2, nblk, S5_BLK * S5_STATE, S5_BLK * S5_GROUP).astype(BF16)

    lam_rows = jnp.stack([lam_bar.real.reshape(2, S5_LANES), lam_bar.imag.reshape(2, S5_LANES)], axis=1)
    return (embed_in(b_bar.real), embed_in(b_bar.imag),
            embed_out(c_re.astype(F32)), embed_out(-c_im.astype(F32)), lam_rows)


def _glu_kernel(yf_ref, yb_ref, u_ref, d_ref, w_ref, b_ref, o_ref, wbf_ref):
    @pl.when(pl.program_id(0) == 0)
    def _():
        wbf_ref[...] = w_ref[...].astype(BF16)

    y = jax.nn.gelu(u_ref[...] * d_ref[...] + yf_ref[0] + yb_ref[0])
    o_ref[...] = (y * jax.nn.sigmoid(_dot(y.astype(BF16), wbf_ref[...]) + b_ref[...])).astype(BF16)


def s5_glu(y_dirs, u, d_skip, w_glu, b_glu, tm=512):
    m, n = u.shape
    return pl.pallas_call(
        _glu_kernel,
        grid=(m // tm,),
        in_specs=[pl.BlockSpec((1, tm, n), lambda i: (0, i, 0)),
                  pl.BlockSpec((1, tm, n), lambda i: (1, i, 0)),
                  pl.BlockSpec((tm, n), lambda i: (i, 0)),
                  pl.BlockSpec((1, n), lambda i: (0, 0)),
                  pl.BlockSpec((n, n), lambda i: (0, 0)),
                  pl.BlockSpec((1, n), lambda i: (0, 0))],
        out_specs=pl.BlockSpec((tm, n), lambda i: (i, 0)),
        out_shape=jax.ShapeDtypeStruct((m, n), BF16),
        scratch_shapes=[pltpu.VMEM((n, n), BF16)],
        compiler_params=_params(1),
        name="s5_glu",
    )(y_dirs, y_dirs, u, d_skip.reshape(1, n), w_glu, b_glu.reshape(1, n))


def s5_mixer(u, x0, s5p, weights):
    bsz, length, _ = u.shape
    bp = -(-bsz // S5_ROWS) * S5_ROWS
    wbr, wbi, wcr, wci, lam = weights
    d_skip, w_glu, b_glu = s5p
    u_t = jnp.swapaxes(u, 0, 1)
    if x0 is None:
        x0_t = jnp.zeros((2, 2, bp, S5_LANES), F32)
    else:
        x0_t = x0.astype(F32).reshape(bsz, 2, 2, S5_LANES).transpose(1, 2, 0, 3)
    if bp != bsz:
        u_t = jnp.pad(u_t, ((0, 0), (0, bp - bsz), (0, 0)))
        x0_t = jnp.pad(x0_t, ((0, 0), (0, 0), (0, bp - bsz), (0, 0)))
    y_dirs, fin = s5_scan(u_t, x0_t, wbr, wbi, wcr, wci, lam)
    y = s5_glu(y_dirs.reshape(2, length * bp, D_HALF), u_t.reshape(length * bp, D_HALF), d_skip, w_glu, b_glu)
    y = jnp.swapaxes(y.reshape(length, bp, D_HALF)[:, :bsz], 0, 1)
    fin = fin[:, :, :bsz].transpose(2, 0, 1, 3).reshape(bsz, 2, 2, S5_GROUPS, S5_STATE)
    return y, fin


def _head_rms(x, g):
    return x * lax.rsqrt(jnp.mean(x * x, axis=-1, keepdims=True) + EPS) * g


def _ctx_attn_kernel(q_ref, k_ref, v_ref, qg_ref, kg_ref, o_ref, kn_ref):
    scale = NA_HD ** -0.5
    for h in range(NA_HEADS):
        q = _head_rms(q_ref[0, h], qg_ref[...])
        k = _head_rms(k_ref[0, h], kg_ref[...])
        kn_ref[0, h] = k
        s = _dot_nt(q.astype(BF16), k.astype(BF16)) * scale
        p = jnp.exp(s - jnp.max(s, axis=-1, keepdims=True))
        l = jnp.sum(p, axis=-1, keepdims=True)
        o = _dot(p.astype(BF16), v_ref[0, h].astype(BF16)) / l
        o_ref[0, :, h * NA_HD:(h + 1) * NA_HD] = o.astype(BF16)


def context_attention(q, k, v, q_g, k_g):
    bsz, nh, t, hd = q.shape
    blk = pl.BlockSpec((1, nh, t, hd), lambda b: (b, 0, 0, 0))
    vec = pl.BlockSpec((1, hd), lambda b: (0, 0))
    return pl.pallas_call(
        _ctx_attn_kernel,
        grid=(bsz,),
        in_specs=[blk, blk, blk, vec, vec],
        out_specs=[pl.BlockSpec((1, t, nh * hd), lambda b: (b, 0, 0)), blk],
        out_shape=[jax.ShapeDtypeStruct((bsz, t, nh * hd), BF16), jax.ShapeDtypeStruct((bsz, nh, t, hd), F32)],
        compiler_params=_params(1),
        name="context_attention",
    )(q, k, v, q_g.reshape(1, hd), k_g.reshape(1, hd))


NA_PAIR = 2


def _na_kernel(q_ref, k_ref, v_ref, ck_ref, cv_ref, bias_ref, qg_ref, kg_ref, o_ref, kn_ref, vn_ref, *, rows):
    scale = NA_HD ** -0.5
    win = NA_KH * GRID_W
    for hh in range(NA_PAIR):
        kn_ref[...] = _head_rms(k_ref[0, hh], kg_ref[...]).astype(BF16)
        vn_ref[...] = v_ref[0, hh].astype(BF16)
        ck = ck_ref[0, hh].astype(BF16)
        cv = cv_ref[0, hh].astype(BF16)
        for r in range(rows):
            rs = min(max(r - NA_KH // 2, 0), rows - NA_KH)
            cls = rs - r + NA_KH - 1
            q = _head_rms(q_ref[0, hh, r * GRID_W:(r + 1) * GRID_W, :], qg_ref[...]).astype(BF16)
            kw = kn_ref[rs * GRID_W:rs * GRID_W + win, :]
            vw = vn_ref[rs * GRID_W:rs * GRID_W + win, :]
            s_w = _dot_nt(q, kw) * scale + bias_ref[hh, cls]
            s_c = _dot_nt(q, ck) * scale
            m = jnp.maximum(jnp.max(s_w, axis=-1, keepdims=True), jnp.max(s_c, axis=-1, keepdims=True))
            p_w = jnp.exp(s_w - m)
            p_c = jnp.exp(s_c - m)
            l = jnp.sum(p_w, axis=-1, keepdims=True) + jnp.sum(p_c, axis=-1, keepdims=True)
            o = (_dot(p_w.astype(BF16), vw) + _dot(p_c.astype(BF16), cv)) / l
            o_ref[0, r * GRID_W:(r + 1) * GRID_W, hh * NA_HD:(hh + 1) * NA_HD] = o.astype(BF16)


def na_bias_table(rpb, rows):
    qc = np.arange(GRID_W)
    kc = np.arange(GRID_W)
    c0 = np.clip(qc - NA_KW // 2, 0, GRID_W - NA_KW)
    ok = (kc[None, :] >= c0[:, None]) & (kc[None, :] < c0[:, None] + NA_KW)
    dc = np.clip(kc[None, :] - qc[:, None] + NA_KW - 1, 0, 2 * NA_KW - 2)
    dr = np.arange(NA_KH)[:, None] + np.arange(NA_KH)[None, :]
    b = rpb.astype(F32)[:, dr][:, :, :, dc]
    b = jnp.where(ok[None, None, None], b, NEG_INF)
    return b.transpose(0, 1, 3, 2, 4).reshape(rpb.shape[0], NA_KH, GRID_W, NA_KH * GRID_W)


def neighbourhood_attention(q, k, v, ck, cv, rpb, q_g, k_g):
    bsz, nh, length, hd = q.shape
    past = ck.shape[2]
    rows = length // GRID_W
    bias = na_bias_table(rpb, rows)
    blk = pl.BlockSpec((1, NA_PAIR, length, hd), lambda b, h: (b, h, 0, 0))
    cblk = pl.BlockSpec((1, NA_PAIR, past, hd), lambda b, h: (b, h, 0, 0))
    vec = pl.BlockSpec((1, hd), lambda b, h: (0, 0))
    return pl.pallas_call(
        functools.partial(_na_kernel, rows=rows),
        grid=(bsz, nh // NA_PAIR),
        in_specs=[blk, blk, blk, cblk, cblk,
                  pl.BlockSpec((NA_PAIR, NA_KH, GRID_W, NA_KH * GRID_W), lambda b, h: (h, 0, 0, 0)),
                  vec, vec],
        out_specs=pl.BlockSpec((1, length, NA_PAIR * hd), lambda b, h: (b, 0, h)),
        out_shape=jax.ShapeDtypeStruct((bsz, length, nh * hd), BF16),
        scratch_shapes=[pltpu.VMEM((length, hd), BF16), pltpu.VMEM((length, hd), BF16)],
        compiler_params=_params(2),
        name="neighbourhood_attention",
    )(q, k, v, ck, cv, bias, q_g.reshape(1, hd), k_g.reshape(1, hd))


def _heads_first(t, bsz, length, nh):
    return t.reshape(bsz, length, nh, -1).transpose(0, 2, 1, 3)


HY_EMB_PAD = 64


def _dot3(a, b):
    a_hi, a_lo = _split_bf16(a)
    b_hi, b_lo = _split_bf16(b)
    return _dot(a_hi, b_hi) + _dot(a_hi, b_lo) + _dot(a_lo, b_hi)


def _hyena_filter_kernel(z_ref, w1_ref, b1_ref, f_ref, w2_ref, b2_ref, w3_ref, dec_ref, o_ref):
    length = z_ref.shape[0]
    h = jnp.sin(f_ref[0:1, :] * (_dot3(z_ref[...], w1_ref[...]) + b1_ref[...]))
    h = jnp.sin(f_ref[1:2, :] * (_dot3(h, w2_ref[...]) + b2_ref[...]))
    filt = _dot3(h, w3_ref[...])
    t = lax.broadcasted_iota(jnp.int32, filt.shape, 0).astype(F32)
    offset = jnp.abs(t - float(length // 2)) / float(length)
    filt = filt * jnp.exp(-offset * jnp.abs(dec_ref[...]))
    o_ref[...] = filt / (jnp.sum(jnp.abs(filt), axis=0, keepdims=True) + EPS)


def _hyena_embedding(length):
    t = np.arange(length, dtype=np.float32) / np.float32(length)
    bands = np.linspace(1e-4, HY_BANDS - 1, HY_BANDS, dtype=np.float32)
    ang = (2 * math.pi * t[:, None] * bands[None, :]).astype(np.float32)
    z = np.concatenate([t[:, None], np.cos(ang), np.sin(ang)], axis=-1).astype(np.float32)
    return np.pad(z, ((0, 0), (0, HY_EMB_PAD - z.shape[1])))


def hyena_filter(length, w1, b1, freq, w2, b2, w3, decay):
    z = jnp.asarray(_hyena_embedding(length))
    w1p = jnp.pad(w1.astype(F32), ((0, HY_EMB_PAD - w1.shape[0]), (0, 0)))
    n = w3.shape[1]
    return pl.pallas_call(
        _hyena_filter_kernel,
        out_shape=jax.ShapeDtypeStruct((length, n), F32),
        compiler_params=_params(0),
        name="hyena_filter",
    )(z, w1p, b1.reshape(1, -1), freq, w2, b2.reshape(1, -1), w3, decay.reshape(1, n))


def _dft_tile(length):
    return min(length, 512)


def _dft_matrices(length):
    n = 2 * length
    tf = _dft_tile(length)
    f = np.arange(length)
    s = np.arange(length)
    ang = 2 * np.pi * ((f[:, None] * s[None, :]) % n) / n
    fc = np.cos(ang)
    fs = -np.sin(ang)
    fs[0] = np.where(s % 2 == 0, 1.0, -1.0)
    nt = np.arange(length) + length // 2
    ang_i = 2 * np.pi * ((nt[:, None] * f[None, :]) % n) / n
    wf = np.where(f == 0, 1.0, 2.0)[None, :]
    gc = wf * np.cos(ang_i) / n
    gs = -wf * np.sin(ang_i) / n
    gs[:, 0] = np.where(nt % 2 == 0, 1.0, -1.0) / n
    n_tiles = length // tf
    fwd = np.concatenate([fc.reshape(n_tiles, tf, length), fs.reshape(n_tiles, tf, length)], axis=1)
    inv = np.concatenate([gc.reshape(length, n_tiles, tf), gs.reshape(length, n_tiles, tf)], axis=2)
    return (jnp.asarray(fwd.reshape(n_tiles * 2 * tf, length), BF16),
            jnp.asarray(inv.transpose(1, 0, 2), BF16))


def _shift_rows(x, up):
    n = x.shape[0]
    row = lax.broadcasted_iota(jnp.int32, x.shape, 0)
    if up:
        return jnp.where(row == n - 1, 0.0, pltpu.roll(x, n - 1, 0))
    return jnp.where(row == 0, 0.0, pltpu.roll(x, 1, 0))


def _conv3(x, w, b):
    return _shift_rows(x, False) * w[0:1] + x * w[1:2] + _shift_rows(x, True) * w[2:3] + b


def _hyena_conv_kernel(x0_ref, x1_ref, v_ref, w0_ref, w1_ref, wv_ref, b0_ref, b1_ref, bv_ref, bias_ref,
                       f_ref, g_ref, hf_ref, o_ref, gated_ref, gbf_ref, acc_ref, *, n_tiles, tf):
    j = pl.program_id(2)

    @pl.when(j == 0)
    def _():
        gated = _conv3(x1_ref[0], w1_ref[...], b1_ref[...]) * _conv3(v_ref[0], wv_ref[...], bv_ref[...])
        gated_ref[...] = gated
        gbf_ref[...] = gated.astype(BF16)
        acc_ref[...] = jnp.zeros_like(acc_ref)

    u = _dot(f_ref[...], gbf_ref[...])
    ur, ui = u[:tf], u[tf:]
    hr, hi = hf_ref[:tf, :], hf_ref[tf:, :]
    packed = (lax.broadcasted_iota(jnp.int32, ur.shape, 0) == 0) & (j == 0)
    yr = ur * hr - jnp.where(packed, 0.0, ui * hi)
    yi = jnp.where(packed, ui * hi, ur * hi + ui * hr)
    y = jnp.concatenate([yr, yi], axis=0).astype(BF16)
    acc_ref[...] += _dot(g_ref[0], y)

    @pl.when(j == n_tiles - 1)
    def _():
        gated = gated_ref[...]
        y = acc_ref[...] + gated * bias_ref[...]
        o_ref[0] = (_conv3(x0_ref[0], w0_ref[...], b0_ref[...]) * y).astype(BF16)


def hyena_mixer(zh, filt, short_w, short_b, bias):
    bsz, length, _ = zh.shape
    fwd, inv = _dft_matrices(length)
    tf = _dft_tile(length)
    n_tiles = length // tf
    hf = project(fwd, filt, tn=512, tm=min(512, 2 * length))
    cn = 256 if length > 512 else 512
    nct = D_HALF // cn

    def zblk(k):
        return pl.BlockSpec((1, length, cn), lambda b, c, j, k=k: (b, 0, k * nct + c))

    def wblk(k):
        return pl.BlockSpec((3, cn), lambda b, c, j, k=k: (0, k * nct + c))

    def bblk(k):
        return pl.BlockSpec((1, cn), lambda b, c, j, k=k: (0, k * nct + c))

    sb = short_b.reshape(1, -1)
    return pl.pallas_call(
        functools.partial(_hyena_conv_kernel, n_tiles=n_tiles, tf=tf),
        grid=(bsz, nct, n_tiles),
        in_specs=[zblk(0), zblk(1), zblk(2), wblk(0), wblk(1), wblk(2), bblk(0), bblk(1), bblk(2),
                  pl.BlockSpec((1, cn), lambda b, c, j: (0, c)),
                  pl.BlockSpec((2 * tf, length), lambda b, c, j: (j, 0)),
                  pl.BlockSpec((1, length, 2 * tf), lambda b, c, j: (j, 0, 0)),
                  pl.BlockSpec((2 * tf, cn), lambda b, c, j: (j, c))],
        out_specs=pl.BlockSpec((1, length, cn), lambda b, c, j: (b, 0, c)),
        out_shape=jax.ShapeDtypeStruct((bsz, length, D_HALF), BF16),
        scratch_shapes=[pltpu.VMEM((length, cn), F32), pltpu.VMEM((length, cn), BF16), pltpu.VMEM((length, cn), F32)],
        compiler_params=_params(3),
        name="hyena_conv",
    )(zh, zh, zh, short_w, short_w, short_w, sb, sb, sb, bias.reshape(1, -1), fwd, inv, hf)


RW_TL = 128
RW_PRE_TL = 128
RW_NB = 4
RW_PAIRS = RW_HEADS // 2
RW_YBLK = 64
RW_GROUP = 8
LANES = 128


def _block_ones():
    idx = np.arange(LANES) // RW_HD
    return (idx[:, None] == idx[None, :]).astype(np.float32)


def _segment_ones():
    return jnp.asarray(np.concatenate([_block_ones(), _block_ones()], axis=0), BF16)


def _segment_ones_pair():
    return jnp.asarray(np.kron(np.eye(2, dtype=np.float32), _block_ones()), BF16)


def _hi_lo(x):
    hi, lo = _split_bf16(x)
    return jnp.concatenate([hi, lo], axis=1)


def _head_sums(x, seg_ref):
    tiles = [_dot(_hi_lo(x[:, j * LANES:(j + 1) * LANES]), seg_ref[...]) for j in range(x.shape[1] // LANES)]
    return jnp.concatenate(tiles, axis=1)


def _rwkv_pre_kernel(z_ref, zp_ref, zn_ref, mu_ref, wl_ref, w0_ref, a0_ref, kk_ref, ka_ref, seg_ref,
                     r_ref, k_ref, v_ref, g_ref, an_ref, w0o_ref, kd0_ref, b0_ref, w1o_ref, kd1_ref, b1_ref,
                     *, n_tiles):
    i = pl.program_id(1)
    z = z_ref[0]
    row = lax.broadcasted_iota(jnp.int32, z.shape, 0)
    prev = jnp.where(i > 0, zp_ref[0, 7:8, :], 0.0)
    nxt = jnp.where(i < n_tiles - 1, zn_ref[0, 0:1, :], 0.0)
    zm1 = jnp.where(row == 0, prev, pltpu.roll(z, 1, 0))
    zp1 = jnp.where(row == z.shape[0] - 1, nxt, pltpu.roll(z, z.shape[0] - 1, 0))
    x = z + (0.5 * (zm1 + zp1) - z) * mu_ref[...]
    r = x[:, 0:D_HALF]
    k = x[:, D_HALF:2 * D_HALF]
    v = x[:, 2 * D_HALF:3 * D_HALF]
    lo = x[:, 3 * D_HALF:3 * D_HALF + RW_LORA]
    lane = lax.broadcasted_iota(jnp.int32, lo.shape, 1)
    act = jnp.where(lane < RW_W_RANK, jnp.tanh(lo),
                    jnp.where(lane < RW_W_RANK + RW_A_RANK, lo, jax.nn.sigmoid(lo)))
    up = _dot(act.astype(BF16), wl_ref[...])
    g = up[:, 4 * D_HALF:5 * D_HALF]
    kk = k * kk_ref[...]
    kk = kk * lax.rsqrt(_head_sums(kk * kk, seg_ref) + EPS)

    def put(ref, val):
        ref[0] = val

    put(r_ref, r)
    put(k_ref, k)
    put(v_ref, v)
    put(g_ref, g)
    put(an_ref, -kk)
    for d, (wo, kdo, bo) in enumerate(((w0o_ref, kd0_ref, b0_ref), (w1o_ref, kd1_ref, b1_ref))):
        logw = -jax.nn.softplus(-(w0_ref[d:d + 1, :] + up[:, d * D_HALF:(d + 1) * D_HALF])) - 0.5
        a = jax.nn.sigmoid(a0_ref[d:d + 1, :] + up[:, (2 + d) * D_HALF:(3 + d) * D_HALF])
        put(wo, jnp.exp(-jnp.exp(logw)))
        put(kdo, k * (1.0 + (a - 1.0) * ka_ref[...]))
        put(bo, kk * a)


def rwkv_pre(zr, mu, w_lora, w0, a0, k_k, k_a):
    bsz, length, width = zr.shape
    tl = RW_PRE_TL
    n_tiles = length // tl
    vec = lambda n: pl.BlockSpec((n, D_HALF), lambda b, i: (0, 0))
    out_blk = pl.BlockSpec((1, tl, D_HALF), lambda b, i: (b, i, 0))
    out_sds = jax.ShapeDtypeStruct((bsz, length, D_HALF), F32)
    return pl.pallas_call(
        functools.partial(_rwkv_pre_kernel, n_tiles=n_tiles),
        grid=(bsz, n_tiles),
        in_specs=[pl.BlockSpec((1, tl, width), lambda b, i: (b, i, 0)),
                  pl.BlockSpec((1, 8, width), lambda b, i: (b, jnp.maximum(i * (tl // 8) - 1, 0), 0)),
                  pl.BlockSpec((1, 8, width), lambda b, i: (b, jnp.minimum((i + 1) * (tl // 8), length // 8 - 1), 0)),
                  pl.BlockSpec((1, width), lambda b, i: (0, 0)),
                  pl.BlockSpec((RW_LORA, 5 * D_HALF), lambda b, i: (0, 0)),
                  vec(2), vec(2), vec(1), vec(1),
                  pl.BlockSpec((2 * LANES, LANES), lambda b, i: (0, 0))],
        out_specs=[out_blk] * 11,
        out_shape=[out_sds] * 11,
        compiler_params=_params(2),
        name="rwkv_pre",
    )(zr, zr, zr, mu, w_lora, w0, a0, k_k.reshape(1, -1), k_a.reshape(1, -1), _segment_ones())


def _rwkv_scan_kernel(r_ref, w_ref, k_ref, a_ref, b_ref, v_ref, s0_ref, seg_ref, seg2_ref, y_ref, fin_ref,
                      st_ref, lhs_ref, vlhs_ref, ylhs_ref, yacc_ref, *, reverse, n_chunks):
    c = pl.program_id(1)

    @pl.when(c == 0)
    def _():
        st_ref[...] = s0_ref[...]

    lane = lax.broadcasted_iota(jnp.int32, (RW_HD, LANES), 1)
    lane_in_head = lane % RW_HD
    diag = jnp.where(lane_in_head == lax.broadcasted_iota(jnp.int32, (RW_HD, LANES), 0), 1.0, 0.0)
    nt = RW_PAIRS
    chains = range(RW_NB)
    n_blocks = RW_TL // RW_YBLK

    def row(ref, cn, t, p):
        return ref[cn, pl.ds(t, 1), :][:, p * LANES:(p + 1) * LANES]

    def tile(sums, q):
        return sums[q * RW_HD:(q + 1) * RW_HD]

    def two_sums(tiles_ref, cn, lo, hi):
        return _dot(tiles_ref[cn, lo:hi].reshape((hi - lo) * RW_HD, 2 * LANES), seg2_ref[...])

    def step(t, u, vcol):
        for cn in chains:
            for p in range(nt):
                lhs_ref[cn, p] = _hi_lo(st_ref[cn, p] * row(a_ref, cn, t, p))
        sa = [_dot(lhs_ref[cn].reshape(nt * RW_HD, 2 * LANES), seg_ref[...]) for cn in chains]
        for cn in chains:
            for p in range(nt):
                s = (st_ref[cn, p] * row(w_ref, cn, t, p) + tile(sa[cn], p) * row(b_ref, cn, t, p)
                     + vcol[cn][p] * row(k_ref, cn, t, p))
                st_ref[cn, p] = s
                sr = (s * row(r_ref, cn, t, p)).astype(BF16)
                ylhs_ref[cn, p // 2, :, (p % 2) * LANES:(p % 2 + 1) * LANES] = sr
        ys = [two_sums(ylhs_ref, cn, 0, nt // 2) for cn in chains]
        hit = lane_in_head == t % RW_YBLK
        for cn in chains:
            for p in range(nt):
                y = tile(ys[cn], p // 2)[:, (p % 2) * LANES:(p % 2 + 1) * LANES]
                yacc_ref[cn, p] = jnp.where(hit, y, yacc_ref[cn, p])

    def step_pair(t0, t1):
        for cn in chains:
            for p in range(nt):
                vlhs_ref[cn, p] = jnp.concatenate([(diag * row(v_ref, cn, t0, p)).astype(BF16),
                                                   (diag * row(v_ref, cn, t1, p)).astype(BF16)], axis=1)
        vs = [two_sums(vlhs_ref, cn, 0, nt) for cn in chains]
        for u, t in enumerate((t0, t1)):
            step(t, u, [[tile(vs[cn], p)[:, u * LANES:(u + 1) * LANES] for p in range(nt)] for cn in chains])

    def step_group(g, blk):
        i0 = blk * RW_YBLK + g * RW_GROUP
        base = pl.multiple_of((RW_TL - RW_GROUP - i0) if reverse else i0, RW_GROUP)
        order = range(RW_GROUP - 1, -1, -1) if reverse else range(RW_GROUP)
        ts = [base + j for j in order]
        for u in range(0, RW_GROUP, 2):
            step_pair(ts[u], ts[u + 1])
        return blk

    def block(bi, carry):
        yacc_ref[...] = jnp.zeros_like(yacc_ref)
        lax.fori_loop(0, RW_YBLK // RW_GROUP, step_group, bi)
        tb = (n_blocks - 1 - bi) if reverse else bi
        rows = pl.ds(pl.multiple_of(tb * RW_YBLK, RW_YBLK), RW_YBLK)
        for cn in chains:
            for p in range(nt):
                yt = yacc_ref[cn, p].T
                for hh in range(2):
                    h = 2 * p + hh
                    y_ref[cn, rows, h * RW_HD:(h + 1) * RW_HD] = yt[hh * RW_YBLK:(hh + 1) * RW_YBLK]
        return carry

    lax.fori_loop(0, n_blocks, block, 0)

    @pl.when(c == n_chunks - 1)
    def _():
        fin_ref[...] = st_ref[...]


def rwkv_scan(r, w, k, a, b, v, s0, reverse):
    bsz, length, n = r.shape
    n_chunks = length // RW_TL

    def chunk(c):
        return (n_chunks - 1 - c) if reverse else c

    row_blk = pl.BlockSpec((RW_NB, RW_TL, n), lambda bi, c: (bi, chunk(c), 0))
    st_blk = pl.BlockSpec((RW_NB, RW_PAIRS, RW_HD, LANES), lambda bi, c: (bi, 0, 0, 0))
    tiles = lambda m: pltpu.VMEM((RW_NB, m, RW_HD, 2 * LANES), BF16)
    return pl.pallas_call(
        functools.partial(_rwkv_scan_kernel, reverse=reverse, n_chunks=n_chunks),
        grid=(bsz // RW_NB, n_chunks),
        in_specs=[row_blk] * 6 + [st_blk, pl.BlockSpec((2 * LANES, LANES), lambda bi, c: (0, 0)),
                                  pl.BlockSpec((2 * LANES, 2 * LANES), lambda bi, c: (0, 0))],
        out_specs=[row_blk, st_blk],
        out_shape=[jax.ShapeDtypeStruct((bsz, length, n), F32),
                   jax.ShapeDtypeStruct((bsz, RW_PAIRS, RW_HD, LANES), F32)],
        scratch_shapes=[pltpu.VMEM((RW_NB, RW_PAIRS, RW_HD, LANES), F32),
                        tiles(RW_PAIRS), tiles(RW_PAIRS), tiles(RW_PAIRS // 2),
                        pltpu.VMEM((RW_NB, RW_PAIRS, RW_HD, LANES), F32)],
        compiler_params=_params(2),
        name="rwkv_scan_rev" if reverse else "rwkv_scan_fwd",
    )(r, w, k, a, b, v, s0, _segment_ones(), _segment_ones_pair())


def _rwkv_post_kernel(yf_ref, yb_ref, r_ref, k_ref, v_ref, g_ref, rk_ref, lg_ref, lb_ref, seg_ref, o_ref):
    y = yf_ref[0] + yb_ref[0]
    mean = _head_sums(y, seg_ref) * (1.0 / RW_HD)
    yc = y - mean
    var = _head_sums(yc * yc, seg_ref) * (1.0 / RW_HD)
    y = yc * lax.rsqrt(var + GN_EPS) * lg_ref[...] + lb_ref[...]
    bonus = _head_sums(r_ref[0] * k_ref[0] * rk_ref[...], seg_ref) * v_ref[0]
    o_ref[0] = ((y + bonus) * g_ref[0]).astype(BF16)


def rwkv_post(yf, yb, r, k, v, g, r_k, ln_g, ln_b):
    bsz, length, n = r.shape
    tl = RW_TL
    blk = pl.BlockSpec((1, tl, n), lambda b, i: (b, i, 0))
    vec = pl.BlockSpec((1, n), lambda b, i: (0, 0))
    return pl.pallas_call(
        _rwkv_post_kernel,
        grid=(bsz, length // tl),
        in_specs=[blk] * 6 + [vec] * 3 + [pl.BlockSpec((2 * LANES, LANES), lambda b, i: (0, 0))],
        out_specs=blk,
        out_shape=jax.ShapeDtypeStruct((bsz, length, n), BF16),
        compiler_params=_params(2),
        name="rwkv_post",
    )(yf, yb, r, k, v, g, r_k.reshape(1, n), ln_g.reshape(1, n), ln_b.reshape(1, n), _segment_ones())


def rwkv_lora_weights(w2, a2, g2):
    w = jnp.zeros((RW_LORA, 5 * D_HALF), F32)
    for d in range(2):
        w = w.at[0:RW_W_RANK, d * D_HALF:(d + 1) * D_HALF].set(w2[d].astype(F32))
        w = w.at[RW_W_RANK:RW_W_RANK + RW_A_RANK, (2 + d) * D_HALF:(3 + d) * D_HALF].set(a2[d].astype(F32))
    w = w.at[RW_W_RANK + RW_A_RANK:, 4 * D_HALF:].set(g2.astype(F32))
    return w.astype(BF16)


def rwkv_mixer(zr, s0, rwp):
    mu, w0, w2, a0, a2, g2, k_k, k_a, r_k, ln_g, ln_b = rwp
    bsz, length = zr.shape[:2]
    mu_p = jnp.pad(mu.astype(F32), (0, RW_IN_PAD - RW_IN)).reshape(1, RW_IN_PAD)
    r, k, v, g, an, wd0, kd0, b0, wd1, kd1, b1 = rwkv_pre(zr, mu_p, rwkv_lora_weights(w2, a2, g2), w0, a0, k_k, k_a)
    if s0 is None:
        s0 = jnp.zeros((bsz, 2, RW_HEADS, RW_HD, RW_HD), F32)

    def pack_state(s):
        return s.astype(F32).reshape(bsz, RW_PAIRS, 2, RW_HD, RW_HD).transpose(0, 1, 3, 2, 4).reshape(
            bsz, RW_PAIRS, RW_HD, LANES)

    def unpack_state(s):
        return s.reshape(bsz, RW_PAIRS, RW_HD, 2, RW_HD).transpose(0, 1, 3, 2, 4).reshape(bsz, RW_HEADS, RW_HD, RW_HD)

    yf, fin_f = rwkv_scan(r, wd0, kd0, an, b0, v, pack_state(s0[:, 0]), reverse=False)
    yb, fin_b = rwkv_scan(r, wd1, kd1, an, b1, v, pack_state(s0[:, 1]), reverse=True)
    y = rwkv_post(yf, yb, r, k, v, g, r_k, ln_g, ln_b)
    return y, jnp.stack([unpack_state(fin_f), unpack_state(fin_b)], axis=1)


ROUTE_TILE = 512


def _rank_before(vals, n):
    idx = lax.broadcasted_iota(jnp.int32, vals.shape, 0)
    cnt = jnp.zeros(vals.shape, F32)
    for e in range(n):
        row = vals[e:e + 1]
        cnt = cnt + jnp.where((row > vals) | ((row == vals) & (idx > e)), 1.0, 0.0)
    return cnt


def _route_kernel(lg_ref, bias_ref, tril_ref, triu_ref, te_ref, gt_ref, rk_ref, cnt_ref, carry_ref):
    @pl.when(pl.program_id(0) == 0)
    def _():
        carry_ref[...] = jnp.zeros_like(carry_ref)

    t = lg_ref.shape[1]
    per = N_EXPERTS // N_ROUTE_GROUPS
    scores = jax.nn.sigmoid(lg_ref[...])
    sel = scores + bias_ref[...]
    sel3 = sel.reshape(N_ROUTE_GROUPS, per, t)
    m1 = jnp.max(sel3, axis=1, keepdims=True)
    within = lax.broadcasted_iota(jnp.int32, sel3.shape, 1)
    first = jnp.min(jnp.where(sel3 == m1, within, per), axis=1, keepdims=True)
    m2 = jnp.max(jnp.where(within == first, -jnp.inf, sel3), axis=1, keepdims=True)
    group_score = (m1 + m2).reshape(N_ROUTE_GROUPS, t)
    group_ok = _rank_before(group_score, N_ROUTE_GROUPS) < TOPK_ROUTE_GROUPS
    expert_ok = jnp.broadcast_to(group_ok.reshape(N_ROUTE_GROUPS, 1, t), sel3.shape).reshape(N_EXPERTS, t)
    masked = jnp.where(expert_ok, sel, NEG_INF)
    chosen = _rank_before(masked, N_EXPERTS) < TOP_K
    gates = jnp.where(chosen, scores, 0.0)
    gates = gates / jnp.sum(gates, axis=0, keepdims=True) * ROUTED_SCALE
    onehot = jnp.where(chosen, 1.0, 0.0).astype(BF16)
    order = _dot(tril_ref[...], onehot)
    before = carry_ref[:, 0:1] + _dot(onehot, triu_ref[...])
    carry_ref[...] = carry_ref[...] + jnp.sum(jnp.where(chosen, 1.0, 0.0), axis=1, keepdims=True)
    eidx = lax.broadcasted_iota(jnp.int32, chosen.shape, 0).astype(F32)
    for j in range(TOP_K):
        pick = chosen & (order == float(j + 1))
        te_ref[j:j + 1, :] = jnp.sum(jnp.where(pick, eidx, 0.0), axis=0, keepdims=True).astype(jnp.int32)
        gt_ref[j:j + 1, :] = jnp.sum(jnp.where(pick, gates, 0.0), axis=0, keepdims=True)
        rk_ref[j:j + 1, :] = jnp.sum(jnp.where(pick, before, 0.0), axis=0, keepdims=True).astype(jnp.int32)
    cnt_ref[...] = carry_ref[...]


def route(logits_t, b_router):
    e, n = logits_t.shape
    t = ROUTE_TILE
    tril = jnp.asarray(np.tril(np.ones((e, e))), BF16)
    triu = jnp.asarray(np.triu(np.ones((t, t)), 1), BF16)
    out_blk = pl.BlockSpec((TOP_K, t), lambda i: (0, i))
    te, gt, rk, cnt = pl.pallas_call(
        _route_kernel,
        grid=(n // t,),
        in_specs=[pl.BlockSpec((e, t), lambda i: (0, i)),
                  pl.BlockSpec((e, 1), lambda i: (0, 0)),
                  pl.BlockSpec((e, e), lambda i: (0, 0)),
                  pl.BlockSpec((t, t), lambda i: (0, 0))],
        out_specs=[out_blk, out_blk, out_blk, pl.BlockSpec((e, 128), lambda i: (0, 0))],
        out_shape=[jax.ShapeDtypeStruct((TOP_K, n), jnp.int32), jax.ShapeDtypeStruct((TOP_K, n), F32),
                   jax.ShapeDtypeStruct((TOP_K, n), jnp.int32), jax.ShapeDtypeStruct((e, 128), F32)],
        scratch_shapes=[pltpu.VMEM((e, 128), F32)],
        compiler_params=_params(1),
        name="route",
    )(logits_t, b_router.astype(F32).reshape(e, 1), tril, triu)
    return te, gt, rk, cnt[:, 0]


def _expert_kernel(be_ref, nu_ref, x_ref, wg_ref, wu_ref, wd_ref, o_ref, wgb_ref, wub_ref, wdb_ref):
    i = pl.program_id(0)
    used = i < nu_ref[0]
    fresh = (i == 0) | (be_ref[i] != be_ref[jnp.maximum(i - 1, 0)])

    @pl.when(used & fresh)
    def _():
        wgb_ref[...] = wg_ref[0, 0].astype(BF16)
        wub_ref[...] = wu_ref[0, 0].astype(BF16)
        wdb_ref[...] = wd_ref[0, 0].astype(BF16)

    @pl.when(used)
    def _():
        x = x_ref[...]
        gate = _dot(x, wgb_ref[...])
        h = (gate * jax.nn.sigmoid(gate) * _dot(x, wub_ref[...])).astype(BF16)
        o_ref[...] = _dot(h, wdb_ref[...]).astype(o_ref.dtype)

    @pl.when(jnp.logical_not(used))
    def _():
        o_ref[...] = jnp.zeros_like(o_ref)


def expert_ffn(xs, block_expert, n_used, w_gate, w_up, w_down, layer):
    s, d = xs.shape
    de = w_gate.shape[3]
    nb = s // MOE_ROWS
    grid_spec = pltpu.PrefetchScalarGridSpec(
        num_scalar_prefetch=2,
        grid=(nb,),
        in_specs=[pl.BlockSpec((MOE_ROWS, d), lambda i, be, nu: (i, 0)),
                  pl.BlockSpec((1, 1, d, de), lambda i, be, nu: (layer, be[i], 0, 0)),
                  pl.BlockSpec((1, 1, d, de), lambda i, be, nu: (layer, be[i], 0, 0)),
                  pl.BlockSpec((1, 1, de, d), lambda i, be, nu: (layer, be[i], 0, 0))],
        out_specs=pl.BlockSpec((MOE_ROWS, d), lambda i, be, nu: (i, 0)),
        scratch_shapes=[pltpu.VMEM((d, de), BF16), pltpu.VMEM((d, de), BF16), pltpu.VMEM((de, d), BF16)],
    )
    return pl.pallas_call(
        _expert_kernel,
        grid_spec=grid_spec,
        out_shape=jax.ShapeDtypeStruct((s, d), BF16),
        compiler_params=_params(1),
        name="expert_ffn",
    )(block_expert, n_used, xs, w_gate, w_up, w_down)


def _combine_kernel(x_ref, yg_ref, gt_ref, sh_ref, mod_ref, o_ref, *, gate_idx):
    routed = jnp.zeros(x_ref.shape, F32)
    for j in range(TOP_K):
        routed = routed + gt_ref[:, j:j + 1] * yg_ref[j].astype(F32)
    g = mod_ref[0][gate_idx:gate_idx + 1]
    o_ref[...] = x_ref[...] + g * (routed + sh_ref[...].astype(F32))


def moe_combine(x, yg, gates, shared, mods, gate_idx, n_ctx, tm=128):
    n, d = x.shape
    row = functools.partial(_cond_row, n_ctx_tiles=n_ctx // tm, tiles_per_sample=DEC_SEQ // tm)
    return pl.pallas_call(
        functools.partial(_combine_kernel, gate_idx=gate_idx),
        grid=(n // tm,),
        in_specs=[pl.BlockSpec((tm, d), lambda i: (i, 0)),
                  pl.BlockSpec((TOP_K, tm, d), lambda i: (0, i, 0)),
                  pl.BlockSpec((tm, TOP_K), lambda i: (i, 0)),
                  pl.BlockSpec((tm, d), lambda i: (i, 0)),
                  pl.BlockSpec((1, 6, d), lambda i: (row(i), 0, 0))],
        out_specs=pl.BlockSpec((tm, d), lambda i: (i, 0)),
        out_shape=jax.ShapeDtypeStruct((n, d), F32),
        compiler_params=_params(1),
        name="moe_combine",
    )(x, yg, gates, shared, mods)


def moe_ffn(x, h, logits_t, mods, gate_idx, n_ctx, layer, b_router, w_gate, w_up, w_down, ws_gate, ws_up, ws_down):
    n = h.shape[0]
    top_e, gates, rank, counts = route(logits_t, b_router)
    counts = counts.astype(jnp.int32)
    padded = (counts + MOE_ROWS - 1) // MOE_ROWS * MOE_ROWS
    pad_end = jnp.cumsum(padded)
    pad_start = pad_end - padded
    n_slots = n * TOP_K + N_EXPERTS * MOE_ROWS
    nb = n_slots // MOE_ROWS
    expert_ids = jnp.arange(N_EXPERTS, dtype=jnp.int32)
    start_of = jnp.sum(jnp.where(top_e[..., None] == expert_ids, pad_start, 0), axis=-1)
    pos = start_of + rank
    tok = jnp.broadcast_to(jnp.arange(n, dtype=jnp.int32)[None, :], pos.shape)
    slot_tok = (jnp.arange(n_slots, dtype=jnp.int32) % n).at[pos.reshape(-1)].set(tok.reshape(-1), unique_indices=True)
    block_start = jnp.arange(nb, dtype=jnp.int32) * MOE_ROWS
    block_expert = jnp.minimum(jnp.sum(pad_end[None, :] <= block_start[:, None], axis=1), N_EXPERTS - 1).astype(jnp.int32)
    n_used = (pad_end[-1:] // MOE_ROWS).astype(jnp.int32)
    xs = h[slot_tok]
    yb = expert_ffn(xs, block_expert, n_used, w_gate, w_up, w_down, layer)
    shared = expert_ffn(h, jnp.zeros((n // MOE_ROWS,), jnp.int32), jnp.full((1,), n // MOE_ROWS, jnp.int32),
                        ws_gate[:, None], ws_up[:, None], ws_down[:, None], layer)
    yg = yb[pos]
    return moe_combine(x, yg, gates.T, shared, mods, gate_idx, n_ctx)


def _layer_ab(h, n_ctx, bsz_ctx, bsz_lat, p):
    z = project(h, p['w_in'], tn=1024)
    s5w = s5_weights(*p['s5_disc'])
    outs = []
    extras = {}
    for name, lo, bsz, length in (('ctx', 0, bsz_ctx, SEQ), ('lat', n_ctx, bsz_lat, DEC_SEQ)):
        zg = z[lo:lo + bsz * length].reshape(bsz, length, 4 * D_HALF)
        u = zg[..., :D_HALF]
        q, k, v = (_heads_first(zg[..., (1 + j) * D_HALF:(2 + j) * D_HALF], bsz, length, NA_HEADS) for j in range(3))
        if name == 'ctx':
            y_a, s5_fin = s5_mixer(u, None, p['s5_glu'], s5w)
            y_b, k_n = context_attention(q, k, v, p['q_g'], p['k_g'])
            extras = {'k': k_n, 'v': v, 's5': s5_fin}
        else:
            y_a, _ = s5_mixer(u, p['s5_state'], p['s5_glu'], s5w)
            y_b = neighbourhood_attention(q, k, v, p['ck'], p['cv'], p['rpb'], p['q_g'], p['k_g'])
        outs.append(jnp.concatenate([y_a, y_b], axis=-1).reshape(bsz * length, 2 * D_HALF))
    return jnp.concatenate(outs, axis=0), extras


def _layer_cd(h, n_ctx, bsz_ctx, bsz_lat, p):
    w_in = p['w_in']
    zh_all = project(h, w_in[:, :3 * D_HALF], tn=1024)
    w_rw = jnp.pad(w_in[:, 3 * D_HALF:], ((0, 0), (0, RW_IN_PAD - RW_IN)))
    zr_all = project(h, w_rw, tn=RW_IN_PAD // 4)
    outs = []
    extras = {}
    for name, lo, bsz, length in (('ctx', 0, bsz_ctx, SEQ), ('lat', n_ctx, bsz_lat, DEC_SEQ)):
        zh = zh_all[lo:lo + bsz * length].reshape(bsz, length, 3 * D_HALF)
        zr = zr_all[lo:lo + bsz * length].reshape(bsz, length, RW_IN_PAD)
        filt = hyena_filter(length, *p['hy_filter'])
        y_c = hyena_mixer(zh, filt, p['hy_short_w'], p['hy_short_b'], p['hy_bias'])
        y_d, rw_fin = rwkv_mixer(zr, None if name == 'ctx' else p['rw_state'], p['rwp'])
        if name == 'ctx':
            extras = {'rw': rw_fin}
        outs.append(jnp.concatenate([y_c, y_d], axis=-1).reshape(bsz * length, 2 * D_HALF))
    return jnp.concatenate(outs, axis=0), extras


def kernel(x_prompt, x_sample, cache_na_k, cache_na_v, state_s5, state_rwkv, c, c_ctx, ada_w, ada_b, norm_mix, norm_ffn, ab_w_in, ab_w_out, s5_lam_re, s5_lam_im, s5_log_step, s5_b_re, s5_b_im, s5_c_re, s5_c_im, s5_d, s5_w_glu, s5_b_glu, na_q_norm, na_k_norm, na_rpb, cd_w_in, cd_w_out, hy_short_w, hy_short_b, hy_w1, hy_b1, hy_freq, hy_w2, hy_b2, hy_w3, hy_decay, hy_bias, rw_mu, rw_w0, rw_w2, rw_a0, rw_a2, rw_g2, rw_k_k, rw_k_a, rw_r_k, rw_ln_g, rw_ln_b, moe_router, moe_router_bias, moe_w_gate, moe_w_up, moe_w_down, moe_ws_gate, moe_ws_up, moe_ws_down):
    bsz_ctx, seq, d = x_prompt.shape
    bsz_lat, dec_seq, _ = x_sample.shape
    assert (seq, dec_seq, d) == (SEQ, DEC_SEQ, D_MODEL) and bsz_lat + 1 <= 8
    depth = ada_w.shape[0]
    n_ctx = bsz_ctx * seq
    x = jnp.concatenate([x_prompt.reshape(n_ctx, d), x_sample.reshape(bsz_lat * dec_seq, d)], axis=0).astype(F32)

    cond = jnp.concatenate([c_ctx[None, :], c, jnp.zeros((8 - 1 - bsz_lat, d), c.dtype)], axis=0).astype(F32)
    mods_all = ada_table(cond, ada_w, ada_b).reshape(depth, 8, 6, d)

    new_k, new_v, new_s5, new_rw = [], [], [], []
    for l in range(depth):
        mods = mods_all[l]
        i = l // 2
        h = modulate(x, norm_mix[l], mods, 0, 1, n_ctx)
        if l % 2 == 0:
            p = {'w_in': ab_w_in[i], 'q_g': na_q_norm[i], 'k_g': na_k_norm[i], 'rpb': na_rpb[i],
                 'ck': cache_na_k[:, i], 'cv': cache_na_v[:, i], 's5_state': state_s5[:, i],
                 's5_disc': (s5_lam_re[i], s5_lam_im[i], s5_log_step[i], s5_b_re[i], s5_b_im[i], s5_c_re[i], s5_c_im[i]),
                 's5_glu': (s5_d[i], s5_w_glu[i], s5_b_glu[i])}
            y, ex = _layer_ab(h, n_ctx, bsz_ctx, bsz_lat, p)
            new_k.append(ex['k'])
            new_v.append(ex['v'])
            new_s5.append(ex['s5'])
            w_out = ab_w_out[i]
        else:
            p = {'w_in': cd_w_in[i], 'hy_filter': (hy_w1[i], hy_b1[i], hy_freq[i], hy_w2[i], hy_b2[i], hy_w3[i], hy_decay[i]),
                 'hy_short_w': hy_short_w[i], 'hy_short_b': hy_short_b[i], 'hy_bias': hy_bias[i],
                 'rw_state': state_rwkv[:, i],
                 'rwp': (rw_mu[i], rw_w0[i], rw_w2[i], rw_a0[i], rw_a2[i], rw_g2[i], rw_k_k[i], rw_k_a[i], rw_r_k[i],
                         rw_ln_g[i], rw_ln_b[i])}
            y, ex = _layer_cd(h, n_ctx, bsz_ctx, bsz_lat, p)
            new_rw.append(ex['rw'])
            w_out = cd_w_out[i]
        x = project_residual(y, w_out, x, mods, 2, n_ctx)
        h, logits_t = modulate(x, norm_ffn[l], mods, 3, 4, n_ctx, w_router_t=moe_router[l].T)
        x = moe_ffn(x, h, logits_t, mods, 5, n_ctx, l, moe_router_bias[l], moe_w_gate, moe_w_up, moe_w_down,
                    moe_ws_gate, moe_ws_up, moe_ws_down)

    y_prompt = x[:n_ctx].reshape(bsz_ctx, seq, d)
    y_sample = x[n_ctx:].reshape(bsz_lat, dec_seq, d)
    return (y_prompt, y_sample, jnp.stack(new_k, axis=1), jnp.stack(new_v, axis=1),
            jnp.stack(new_s5, axis=1), jnp.stack(new_rw, axis=1))
```
